```python
import math
import jax, jax.numpy as jnp
from jax import lax
import numpy as np

D_MODEL = 1024
BATCH = 8
SEQ = 2048
DEPTH = 4
DEC_BATCH = 16
DEC_SEQ = 4096
PAST_LEN = 128

D_FF = 2816
CONV_K = 4
CONV_PAD = (2, 1)
CHUNK = 128
LRU_W = 512
LRU_BLOCKS = 8
LRU_BLOCK_W = LRU_W // LRU_BLOCKS
LRU_C = 8.0
SSD_HEADS = 8
SSD_HEAD_DIM = 64
SSD_W = SSD_HEADS * SSD_HEAD_DIM
SSD_GROUPS = 2
SSD_STATE = 64
SSD_CONV_W = SSD_W + 2 * SSD_GROUPS * SSD_STATE
RET_HEADS = 4
RET_QK = 64
RET_V = 128
RET_W = RET_HEADS * RET_V
ROPE_BASE = 10000.0
MLSTM_HEADS = 4
MLSTM_QK = 64
MLSTM_V = 128
MLSTM_W = MLSTM_HEADS * MLSTM_V
N_BRANCH = 4
BRANCH_W = 512
DN_ALPHA = (2.0 * DEPTH) ** 0.25
DN_BETA = (8.0 * DEPTH) ** -0.25
NORM_EPS = 1e-5
IN_SPLITS = (LRU_W, LRU_W,
             SSD_W, SSD_CONV_W, 2 * SSD_HEADS,
             RET_HEADS * RET_QK, RET_HEADS * RET_QK, RET_W, RET_W,
             2 * MLSTM_HEADS * MLSTM_QK, MLSTM_W, MLSTM_W, 4 * MLSTM_HEADS,
             N_BRANCH * D_MODEL)
IN_COLS = sum(IN_SPLITS)

kernel_name = 'hybrid_bidir_rglru_ssd_retention_mlstm_trunk'


def _split_points():
    pts, acc = [], 0
    for s in IN_SPLITS[:-1]:
        acc += s
        pts.append(acc)
    return pts


def _flip(a):
    return jnp.flip(a, axis=1)


def _layer_norm(x, g, b):
    xf = x.astype(jnp.float32)
    mu = jnp.mean(xf, axis=-1, keepdims=True)
    var = jnp.mean(jnp.square(xf - mu), axis=-1, keepdims=True)
    return ((xf - mu) * lax.rsqrt(var + NORM_EPS) * g + b).astype(x.dtype)


def _head_norm(x, g):
    mu = jnp.mean(x, axis=-1, keepdims=True)
    var = jnp.mean(jnp.square(x - mu), axis=-1, keepdims=True)
    y = (x - mu) * lax.rsqrt(var + NORM_EPS)
    return y.reshape(x.shape[0], x.shape[1], -1) * g


def _rms_norm(x, g):
    return x * lax.rsqrt(jnp.mean(jnp.square(x), axis=-1, keepdims=True) + NORM_EPS) * g


def _dwconv(x, w, b):
    y = lax.conv_general_dilated(x, w[:, None, :], window_strides=(1,), padding=[CONV_PAD],
                                 dimension_numbers=('NWC', 'WIO', 'NWC'),
                                 feature_group_count=x.shape[-1])
    return y + b


def _swiglu(x, w_up, w_down):
    gate, up = jnp.split(x @ w_up, 2, axis=-1)
    return (jax.nn.silu(gate) * up) @ w_down


def _rope(x):
    seq, d = x.shape[1], x.shape[-1]
    inv = ROPE_BASE ** (-jnp.arange(0, d, 2, dtype=jnp.float32) / d)
    ang = jnp.arange(seq, dtype=jnp.float32)[:, None] * inv[None, :]
    cos = jnp.cos(ang)[None, :, None, :]
    sin = jnp.sin(ang)[None, :, None, :]
    x1, x2 = x[..., : d // 2], x[..., d // 2:]
    return jnp.concatenate([x1 * cos - x2 * sin, x1 * sin + x2 * cos], axis=-1)


def _lin_combine(left, right):
    a1, b1 = left
    a2, b2 = right
    return a1 * a2, a2 * b1 + b2


def _rglru_direction(xc, gate_w, gate_b, lam, reverse):
    bsz, seq, _ = xc.shape
    xb = xc.reshape(bsz, seq, LRU_BLOCKS, LRU_BLOCK_W)
    pre = jnp.einsum('blhi,ghij->gblhj', xb, gate_w).reshape(2, bsz, seq, LRU_W) + gate_b[:, None, None, :]
    r = jax.nn.sigmoid(pre[0])
    i = jax.nn.sigmoid(pre[1])
    log_a = -LRU_C * r * jax.nn.softplus(-lam)
    a = jnp.exp(log_a)
    b = jnp.sqrt(-jnp.expm1(2.0 * log_a)) * (i * xc)
    _, h = lax.associative_scan(_lin_combine, (a, b), reverse=reverse, axis=1)
    return h


def _chunked_decay_scan(q, k, v, log_a, include_diag):
    bsz, seq, nh, dk = q.shape
    dv = v.shape[-1]
    n = seq // CHUNK
    qc = q.reshape(bsz, n, CHUNK, nh, dk)
    kc = k.reshape(bsz, n, CHUNK, nh, dk)
    vc = v.reshape(bsz, n, CHUNK, nh, dv)
    cum = jnp.cumsum(log_a.reshape(bsz, n, CHUNK, nh), axis=2)
    mask = jnp.tril(jnp.ones((CHUNK, CHUNK), dtype=bool), k=0 if include_diag else -1)
    diff = cum[:, :, :, None, :] - cum[:, :, None, :, :]
    decay = jnp.exp(jnp.where(mask[None, None, :, :, None], diff, -jnp.inf))
    scores = jnp.einsum('bnihd,bnjhd->bnijh', qc, kc) * decay
    y_intra = jnp.einsum('bnijh,bnjhv->bnihv', scores, vc)
    total = cum[:, :, -1, :]
    w_state = jnp.exp(total[:, :, None, :] - cum)
    s_chunk = jnp.einsum('bnjh,bnjhd,bnjhv->bnhdv', w_state, kc, vc)

    def step(h, inp):
        s_n, tot_n = inp
        return jnp.exp(tot_n)[..., None, None] * h + s_n, h

    h0 = jnp.zeros((bsz, nh, dk, dv), jnp.float32)
    _, h_prev = lax.scan(step, h0, (jnp.moveaxis(s_chunk, 1, 0), jnp.moveaxis(total, 1, 0)))
    h_prev = jnp.moveaxis(h_prev, 0, 1)
    y_inter = jnp.einsum('bnihd,bnhdv->bnihv', qc * jnp.exp(cum)[..., None], h_prev)
    return (y_intra + y_inter).reshape(bsz, seq, nh, dv)


def _mlstm_chunked(q, k, v, log_i, log_f):
    bsz, seq, nh, dk = q.shape
    dv = v.shape[-1]
    n = seq // CHUNK
    qc = q.reshape(bsz, n, CHUNK, nh, dk)
    kc = k.reshape(bsz, n, CHUNK, nh, dk)
    vc = v.reshape(bsz, n, CHUNK, nh, dv)
    li = log_i.reshape(bsz, n, CHUNK, nh)
    b = jnp.cumsum(log_f.reshape(bsz, n, CHUNK, nh), axis=2)
    b_last = b[:, :, -1, :]
    w_state = b_last[:, :, None, :] - b + li
    m_loc = jnp.max(w_state, axis=2)
    e_state = jnp.exp(w_state - m_loc[:, :, None, :])
    s_chunk = jnp.einsum('bnjh,bnjhd,bnjhv->bnhdv', e_state, kc, vc)
    n_chunk = jnp.einsum('bnjh,bnjhd->bnhd', e_state, kc)

    def step(carry, inp):
        c_mat, n_vec, m = carry
        s_n, ns_n, bl_n, ml_n = inp
        m_new = jnp.maximum(bl_n + m, ml_n)
        dec = jnp.exp(bl_n + m - m_new)
        inj = jnp.exp(ml_n - m_new)
        c_new = dec[..., None, None] * c_mat + inj[..., None, None] * s_n
        n_new = dec[..., None] * n_vec + inj[..., None] * ns_n
        return (c_new, n_new, m_new), (c_mat, n_vec, m)

    init = (jnp.zeros((bsz, nh, dk, dv), jnp.float32), jnp.zeros((bsz, nh, dk), jnp.float32),
            jnp.zeros((bsz, nh), jnp.float32))
    xs = (jnp.moveaxis(s_chunk, 1, 0), jnp.moveaxis(n_chunk, 1, 0), jnp.moveaxis(b_last, 1, 0), jnp.moveaxis(m_loc, 1, 0))
    _, (c_prev, n_prev, m_prev) = lax.scan(step, init, xs)
    c_prev = jnp.moveaxis(c_prev, 0, 1)
    n_prev = jnp.moveaxis(n_prev, 0, 1)
    m_prev = jnp.moveaxis(m_prev, 0, 1)
    mask = jnp.tril(jnp.ones((CHUNK, CHUNK), dtype=bool))
    intra = jnp.where(mask[None, None, :, :, None],
                      b[:, :, :, None, :] - b[:, :, None, :, :] + li[:, :, None, :, :], -jnp.inf)
    inter = b + m_prev[:, :, None, :]
    m_row = jnp.maximum(inter, jnp.max(intra, axis=3))
    smat = jnp.einsum('bnihd,bnjhd->bnijh', qc, kc) * jnp.exp(intra - m_row[:, :, :, None, :])
    e_inter = jnp.exp(inter - m_row)
    num = jnp.einsum('bnijh,bnjhv->bnihv', smat, vc) + e_inter[..., None] * jnp.einsum('bnihd,bnhdv->bnihv', qc, c_prev)
    den = jnp.sum(smat, axis=3) + e_inter * jnp.einsum('bnihd,bnhd->bnih', qc, n_prev)
    h = num / jnp.maximum(jnp.abs(den), jnp.exp(-m_row))[..., None]
    return h.reshape(bsz, seq, nh, dv)


def _token_mixers(x, w_in, lru_conv_w, lru_conv_b, lru_gate_w, lru_gate_b, lru_lambda,
                  ssd_conv_w, ssd_conv_b, ssd_dt_bias, ssd_a_log, ssd_d, ssd_norm_w,
                  ret_norm_w, mlstm_conv_w, mlstm_conv_b, mlstm_gate_b, mlstm_norm_w,
                  w_branch, w_out):
    f32 = jnp.float32
    bsz, seq, _ = x.shape
    (xa, ga, z, xbc, dt_raw, rq, rk, rv, rg, mqk, mv, mo, mgate, mix_g) = jnp.split(x @ w_in, _split_points(), axis=-1)

    xa = _dwconv(xa, lru_conv_w, lru_conv_b).astype(f32)
    h_lru = (_rglru_direction(xa, lru_gate_w[0], lru_gate_b[0], lru_lambda[0], False)
             + _rglru_direction(xa, lru_gate_w[1], lru_gate_b[1], lru_lambda[1], True))
    y_a = jax.nn.gelu(ga.astype(f32)) * h_lru

    xbc = jax.nn.silu(_dwconv(xbc, ssd_conv_w, ssd_conv_b).astype(f32))
    xs, bm, cm = jnp.split(xbc, [SSD_W, SSD_W + SSD_GROUPS * SSD_STATE], axis=-1)
    xs = xs.reshape(bsz, seq, SSD_HEADS, SSD_HEAD_DIM)
    heads_per_group = SSD_HEADS // SSD_GROUPS
    bm = jnp.repeat(bm.reshape(bsz, seq, SSD_GROUPS, SSD_STATE), heads_per_group, axis=2)
    cm = jnp.repeat(cm.reshape(bsz, seq, SSD_GROUPS, SSD_STATE), heads_per_group, axis=2)
    dt = jax.nn.softplus(dt_raw.astype(f32).reshape(bsz, seq, 2, SSD_HEADS) + ssd_dt_bias)
    log_a = dt * -jnp.exp(ssd_a_log.astype(f32))
    y_fwd = _chunked_decay_scan(cm, bm * dt[:, :, 0, :, None], xs, log_a[:, :, 0], True)
    y_bwd = _flip(_chunked_decay_scan(_flip(cm), _flip(bm * dt[:, :, 1, :, None]), _flip(xs),
                                      _flip(log_a[:, :, 1]), True))
    y_b = (y_fwd + y_bwd + ssd_d[:, None] * xs).reshape(bsz, seq, SSD_W)
    y_b = _rms_norm(y_b * jax.nn.silu(z.astype(f32)), ssd_norm_w)

    rq = _rope(rq.astype(f32).reshape(bsz, seq, RET_HEADS, RET_QK)) * RET_QK ** -0.5
    rk = _rope(rk.astype(f32).reshape(bsz, seq, RET_HEADS, RET_QK))
    rv = rv.astype(f32).reshape(bsz, seq, RET_HEADS, RET_V)
    log_gamma = jnp.broadcast_to(jnp.log1p(-jnp.exp2(-5.0 - jnp.arange(RET_HEADS, dtype=f32))), (bsz, seq, RET_HEADS))
    o_ret = (_chunked_decay_scan(rq, rk, rv, log_gamma, True)
             + _flip(_chunked_decay_scan(_flip(rq), _flip(rk), _flip(rv), log_gamma, False)))
    y_c = jax.nn.silu(rg.astype(f32)) * _head_norm(o_ret, ret_norm_w)

    mqk = jax.nn.silu(_dwconv(mqk, mlstm_conv_w, mlstm_conv_b).astype(f32))
    mq, mk = jnp.split(mqk, 2, axis=-1)
    mq = mq.reshape(bsz, seq, MLSTM_HEADS, MLSTM_QK)
    mk = mk.reshape(bsz, seq, MLSTM_HEADS, MLSTM_QK) * MLSTM_QK ** -0.5
    mv = mv.astype(f32).reshape(bsz, seq, MLSTM_HEADS, MLSTM_V)
    gt = mgate.astype(f32).reshape(bsz, seq, 2, 2, MLSTM_HEADS) + mlstm_gate_b
    h_fwd = _mlstm_chunked(mq, mk, mv, gt[:, :, 0, 0], jax.nn.log_sigmoid(gt[:, :, 0, 1]))
    h_bwd = _flip(_mlstm_chunked(_flip(mq), _flip(mk), _flip(mv), _flip(gt[:, :, 1, 0]),
                                 _flip(jax.nn.log_sigmoid(gt[:, :, 1, 1]))))
    y_d = jax.nn.sigmoid(mo.astype(f32)) * _head_norm(h_fwd + h_bwd, mlstm_norm_w)

    gates = jax.nn.sigmoid(mix_g.astype(f32)).reshape(bsz, seq, N_BRANCH, D_MODEL)
    dt_x = x.dtype
    merged = (gates[:, :, 0] * (y_a.astype(dt_x) @ w_branch[0])
              + gates[:, :, 1] * (y_b.astype(dt_x) @ w_branch[1])
              + gates[:, :, 2] * (y_c.astype(dt_x) @ w_branch[2])
              + gates[:, :, 3] * (y_d.astype(dt_x) @ w_branch[3]))
    return merged.astype(dt_x) @ w_out


def _trunk(x, w_in, lru_conv_w, lru_conv_b, lru_gate_w, lru_gate_b, lru_lambda,
           ssd_conv_w, ssd_conv_b, ssd_dt_bias, ssd_a_log, ssd_d, ssd_norm_w,
           ret_norm_w, mlstm_conv_w, mlstm_conv_b, mlstm_gate_b, mlstm_norm_w,
           w_branch, w_out, w_ffn_in, w_ffn_out, ln_g, ln_b):
    for l in range(DEPTH):
        x = _layer_norm(DN_ALPHA * x + 0.5 * _swiglu(x, w_ffn_in[l, 0], w_ffn_out[l, 0]), ln_g[l, 0], ln_b[l, 0])
        mix = _token_mixers(x, w_in[l], lru_conv_w[l], lru_conv_b[l], lru_gate_w[l], lru_gate_b[l], lru_lambda[l],
                            ssd_conv_w[l], ssd_conv_b[l], ssd_dt_bias[l], ssd_a_log[l], ssd_d[l], ssd_norm_w[l],
                            ret_norm_w[l], mlstm_conv_w[l], mlstm_conv_b[l], mlstm_gate_b[l], mlstm_norm_w[l],
                            w_branch[l], w_out[l])
        x = _layer_norm(DN_ALPHA * x + mix, ln_g[l, 1], ln_b[l, 1])
        x = _layer_norm(DN_ALPHA * x + 0.5 * _swiglu(x, w_ffn_in[l, 1], w_ffn_out[l, 1]), ln_g[l, 2], ln_b[l, 2])
    return x


def setup_inputs(seed: int = 0) -> dict:
    key = jax.random.key(seed)
    ks = jax.random.split(key, 26)
    f32 = jnp.float32

    def nrm(k, shape, scale):
        return scale * jax.random.normal(k, shape, f32)

    D = D_MODEL
    u = jax.random.uniform(ks[6], (DEPTH, 2, LRU_W), f32, 0.9, 0.999)
    p = u ** (1.0 / LRU_C)
    dt0 = jnp.exp(jax.random.uniform(ks[10], (DEPTH, 2, SSD_HEADS), f32, math.log(1e-3), math.log(1e-1)))
    gate_i = nrm(ks[17], (DEPTH, 2, 1, MLSTM_HEADS), 0.1)
    gate_f = jnp.linspace(3.0, 6.0, MLSTM_HEADS, dtype=f32) + nrm(ks[18], (DEPTH, 2, 1, MLSTM_HEADS), 0.1)
    return {
        'x_prompt': nrm(ks[0], (BATCH, SEQ, D), 1.0),
        'x_sample': nrm(ks[1], (DEC_BATCH, DEC_SEQ, D), 1.0),
        'w_in': nrm(ks[2], (DEPTH, D, IN_COLS), D ** -0.5),
        'lru_conv_w': nrm(ks[3], (DEPTH, CONV_K, LRU_W), CONV_K ** -0.5),
        'lru_conv_b': nrm(ks[4], (DEPTH, LRU_W), 0.02),
        'lru_gate_w': nrm(ks[5], (DEPTH, 2, 2, LRU_BLOCKS, LRU_BLOCK_W, LRU_BLOCK_W), LRU_BLOCK_W ** -0.5),
        'lru_gate_b': nrm(ks[7], (DEPTH, 2, 2, LRU_W), 0.02),
        'lru_lambda': jnp.log(p) - jnp.log1p(-p),
        'ssd_conv_w': nrm(ks[8], (DEPTH, CONV_K, SSD_CONV_W), CONV_K ** -0.5),
        'ssd_conv_b': nrm(ks[9], (DEPTH, SSD_CONV_W), 0.02),
        'ssd_dt_bias': dt0 + jnp.log(-jnp.expm1(-dt0)),
        'ssd_a_log': jnp.log(jax.random.uniform(ks[11], (DEPTH, 2, SSD_HEADS), f32, 1.0, 16.0)),
        'ssd_d': 1.0 + nrm(ks[12], (DEPTH, SSD_HEADS), 0.02),
        'ssd_norm_w': 1.0 + nrm(ks[13], (DEPTH, SSD_W), 0.02),
        'ret_norm_w': 1.0 + nrm(ks[14], (DEPTH, RET_W), 0.02),
        'mlstm_conv_w': nrm(ks[15], (DEPTH, CONV_K, 2 * MLSTM_HEADS * MLSTM_QK), CONV_K ** -0.5),
        'mlstm_conv_b': nrm(ks[16], (DEPTH, 2 * MLSTM_HEADS * MLSTM_QK), 0.02),
        'mlstm_gate_b': jnp.concatenate([gate_i, gate_f], axis=2),
        'mlstm_norm_w': 1.0 + nrm(ks[19], (DEPTH, MLSTM_W), 0.02),
        'w_branch': nrm(ks[20], (DEPTH, N_BRANCH, BRANCH_W, D), DN_BETA * BRANCH_W ** -0.5),
        'w_out': nrm(ks[21], (DEPTH, D, D), DN_BETA * D ** -0.5),
        'w_ffn_in': nrm(ks[22], (DEPTH, 2, D, 2 * D_FF), D ** -0.5),
        'w_ffn_out': nrm(ks[23], (DEPTH, 2, D_FF, D), DN_BETA * D_FF ** -0.5),
        'ln_g': 1.0 + nrm(ks[24], (DEPTH, 3, D), 0.02),
        'ln_b': nrm(ks[25], (DEPTH, 3, D), 0.02),
    }


def reference(x_prompt, x_sample, w_in, lru_conv_w, lru_conv_b, lru_gate_w, lru_gate_b, lru_lambda,
              ssd_conv_w, ssd_conv_b, ssd_dt_bias, ssd_a_log, ssd_d, ssd_norm_w,
              ret_norm_w, mlstm_conv_w, mlstm_conv_b, mlstm_gate_b, mlstm_norm_w,
              w_branch, w_out, w_ffn_in, w_ffn_out, ln_g, ln_b):
    y_prompt = _trunk(x_prompt, w_in, lru_conv_w, lru_conv_b, lru_gate_w, lru_gate_b, lru_lambda,
                      ssd_conv_w, ssd_conv_b, ssd_dt_bias, ssd_a_log, ssd_d, ssd_norm_w,
                      ret_norm_w, mlstm_conv_w, mlstm_conv_b, mlstm_gate_b, mlstm_norm_w,
                      w_branch, w_out, w_ffn_in, w_ffn_out, ln_g, ln_b)
    y_sample = _trunk(x_sample, w_in, lru_conv_w, lru_conv_b, lru_gate_w, lru_gate_b, lru_lambda,
                      ssd_conv_w, ssd_conv_b, ssd_dt_bias, ssd_a_log, ssd_d, ssd_norm_w,
                      ret_norm_w, mlstm_conv_w, mlstm_conv_b, mlstm_gate_b, mlstm_norm_w,
                      w_branch, w_out, w_ffn_in, w_ffn_out, ln_g, ln_b)
    return (y_prompt, y_sample)
```

```python
import functools
import math

import jax
import jax.numpy as jnp
from jax import lax
from jax.experimental import pallas as pl
from jax.experimental.pallas import tpu as pltpu

F32 = jnp.float32
BF16 = jnp.bfloat16
HIGHEST = lax.Precision.HIGHEST

D_MODEL = 1024
DEPTH = 4
D_FF = 2816
CHUNK = 128
LRU_W = 512
LRU_BLOCKS = 8
LRU_C = 8.0
SSD_HEADS = 8
SSD_HEAD_DIM = 64
SSD_W = 512
SSD_STATE = 64
SSD_GROUPS = 2
SSD_CONV_W = SSD_W + 2 * SSD_GROUPS * SSD_STATE
RET_HEADS = 4
RET_QK = 64
RET_V = 128
ROPE_BASE = 10000.0
ML_HEADS = 4
ML_QK = 64
ML_V = 128
DN_ALPHA = (2.0 * DEPTH) ** 0.25
NORM_EPS = 1e-5

VMEM_LIMIT_BYTES = 56 * 1024 * 1024
FF_CHUNK = 256
TOKEN_TILE = 512
SEQ_TILE = 512
NCH = SEQ_TILE // CHUNK
SUBLANES = 8

PROJ_OUTS = (("xa", 512), ("ga", 512), ("z", 512), ("xbc", 768), ("dt", 128), ("rq", 256), ("rk", 256),
             ("rv", 512), ("rg", 512), ("mqk", 512), ("mv", 512), ("mo", 512), ("mg", 128))
PROJ_W = sum(w for _, w in PROJ_OUTS)


def _cparams(sem):
    return pltpu.CompilerParams(dimension_semantics=sem, vmem_limit_bytes=VMEM_LIMIT_BYTES)


def _const_spec(shape):
    nd = len(shape)
    return pl.BlockSpec(shape, lambda *_: (0,) * nd, pipeline_mode=pl.Buffered(1))


def _bdot(a, b):
    return jnp.dot(a, b, preferred_element_type=F32)


def _xdot(a, b):
    return jnp.dot(a, b, preferred_element_type=F32, precision=HIGHEST)


def _sigmoid(x):
    return 1.0 / (1.0 + jnp.exp(-x))


def _silu(x):
    return x * _sigmoid(x)


def _softplus(x):
    return jnp.maximum(x, 0.0) + jnp.log1p(jnp.exp(-jnp.abs(x)))


def _layer_norm(y, g, b):
    mu = jnp.mean(y, axis=-1, keepdims=True)
    yc = y - mu
    var = jnp.mean(yc * yc, axis=-1, keepdims=True)
    return yc * lax.rsqrt(var + NORM_EPS) * g + b


def _ffn_kernel(x_ref, wg_ref, wu_ref, wd_ref, g_ref, b_ref, o_ref, acc_ref):
    x = x_ref[...]
    xb = x.astype(BF16)
    nf = wg_ref.shape[0]
    for f in range(nf):
        gate = _bdot(xb, wg_ref[f])
        up = _bdot(xb, wu_ref[f])
        h = (_silu(gate) * up).astype(BF16)
        part = _bdot(h, wd_ref[f])
        if f == 0:
            acc_ref[...] = part
        else:
            acc_ref[...] += part
    y = DN_ALPHA * x + 0.5 * acc_ref[...]
    o_ref[...] = _layer_norm(y, g_ref[...], b_ref[...])


def _ffn_call(x, wg, wu, wd, g, b):
    m = x.shape[0]
    tm = TOKEN_TILE
    return pl.pallas_call(
        _ffn_kernel,
        out_shape=jax.ShapeDtypeStruct((m, D_MODEL), F32),
        grid=(m // tm,),
        in_specs=[pl.BlockSpec((tm, D_MODEL), lambda i: (i, 0)),
                  _const_spec(wg.shape), _const_spec(wu.shape), _const_spec(wd.shape),
                  _const_spec(g.shape), _const_spec(b.shape)],
        out_specs=pl.BlockSpec((tm, D_MODEL), lambda i: (i, 0)),
        scratch_shapes=[pltpu.VMEM((tm, D_MODEL), F32)],
        compiler_params=_cparams(("parallel",)),
        name="ffn",
    )(x, wg, wu, wd, g, b)


def _proj_kernel(x_ref, w_ref, *o_refs):
    xb = x_ref[...].astype(BF16)
    off = 0
    for o_ref in o_refs:
        w = o_ref.shape[1]
        o_ref[...] = _bdot(xb, w_ref[:, off:off + w])
        off += w


def _proj_call(x, wp):
    m = x.shape[0]
    tm = TOKEN_TILE
    return pl.pallas_call(
        _proj_kernel,
        out_shape=[jax.ShapeDtypeStruct((m, w), F32) for _, w in PROJ_OUTS],
        grid=(m // tm,),
        in_specs=[pl.BlockSpec((tm, D_MODEL), lambda i: (i, 0)), _const_spec(wp.shape)],
        out_specs=[pl.BlockSpec((tm, w), lambda i: (i, 0)) for _, w in PROJ_OUTS],
        compiler_params=_cparams(("parallel",)),
        name="proj",
    )(x, wp)


def _merge_kernel(x_ref, ya_ref, yb_ref, yc_ref, yd_ref, wg_ref, wb_ref, wo_ref, g_ref, b_ref, o_ref):
    x = x_ref[...]
    xb = x.astype(BF16)
    merged = None
    for i, y_ref in enumerate((ya_ref, yb_ref, yc_ref, yd_ref)):
        gate = _sigmoid(_bdot(xb, wg_ref[:, i * D_MODEL:(i + 1) * D_MODEL]))
        term = gate * _bdot(y_ref[...], wb_ref[i])
        merged = term if merged is None else merged + term
    out = _bdot(merged.astype(BF16), wo_ref[...])
    o_ref[...] = _layer_norm(DN_ALPHA * x + out, g_ref[...], b_ref[...])


def _merge_call(x, ys, wg, wb, wo, g, b):
    m = x.shape[0]
    tm = TOKEN_TILE
    return pl.pallas_call(
        _merge_kernel,
        out_shape=jax.ShapeDtypeStruct((m, D_MODEL), F32),
        grid=(m // tm,),
        in_specs=[pl.BlockSpec((tm, D_MODEL), lambda i: (i, 0))]
                 + [pl.BlockSpec((tm, 512), lambda i: (i, 0)) for _ in range(4)]
                 + [_const_spec(wg.shape), _const_spec(wb.shape), _const_spec(wo.shape),
                    _const_spec(g.shape), _const_spec(b.shape)],
        out_specs=pl.BlockSpec((tm, D_MODEL), lambda i: (i, 0)),
        compiler_params=_cparams(("parallel",)),
        name="merge",
    )(x, *ys, wg, wb, wo, g, b)


def _tile_pos(p, t, nt):
    return t + (1 - p) * (nt - 1 - 2 * t)


def _both_spec(w, nt):
    return pl.BlockSpec((SEQ_TILE, w), lambda b, p, t: (b * nt + _tile_pos(p, t, nt), 0))


def _fwd_spec(w, nt):
    return pl.BlockSpec((SEQ_TILE, w), lambda b, p, t: (b * nt + p * t, 0))


def _halo_specs(w, nt, n_rows):
    per = SEQ_TILE // SUBLANES
    last = n_rows // SUBLANES - 1
    prev = pl.BlockSpec((SUBLANES, w),
                        lambda b, p, t: (jnp.maximum((b * nt + _tile_pos(p, t, nt)) * per - 1, 0), 0))
    nxt = pl.BlockSpec((SUBLANES, w),
                       lambda b, p, t: (jnp.minimum((b * nt + _tile_pos(p, t, nt) + 1) * per, last), 0))
    return prev, nxt


def _mixer_call(kernel, name, nb, seq, operands, in_specs, scratch):
    nt = seq // SEQ_TILE
    return pl.pallas_call(
        functools.partial(kernel, nt=nt),
        out_shape=jax.ShapeDtypeStruct((nb * seq, 512), BF16),
        grid=(nb, 2, nt),
        in_specs=in_specs,
        out_specs=pl.BlockSpec((SEQ_TILE, 512), lambda b, p, t: (b * nt + p * t, 0)),
        scratch_shapes=scratch,
        compiler_params=_cparams(("arbitrary", "arbitrary", "arbitrary")),
        name=name,
    )(*operands)


def _dwconv_tile(x, prev8, next8, w, b, is_first, is_last):
    n = x.shape[0]
    row = lax.broadcasted_iota(jnp.int32, x.shape, 0)
    pm1 = jnp.where(is_first, 0.0, prev8[7:8, :])
    pm2 = jnp.where(is_first, 0.0, prev8[6:7, :])
    np1 = jnp.where(is_last, 0.0, next8[0:1, :])
    xm1 = jnp.where(row == 0, pm1, pltpu.roll(x, 1, 0))
    xm2 = jnp.where(row == 0, pm2, jnp.where(row == 1, pm1, pltpu.roll(x, 2, 0)))
    xp1 = jnp.where(row == n - 1, np1, pltpu.roll(x, n - 1, 0))
    return w[0:1, :] * xm2 + w[1:2, :] * xm1 + w[2:3, :] * x + w[3:4, :] * xp1 + b


def _tri_masks():
    ii = lax.broadcasted_iota(jnp.int32, (CHUNK, CHUNK), 0)
    jj = lax.broadcasted_iota(jnp.int32, (CHUNK, CHUNK), 1)
    return jj <= ii, jj >= ii


def _chunk_rows(ci):
    return pl.ds(pl.multiple_of(ci * CHUNK, CHUNK), CHUNK)


def _head_norm(x):
    mu = jnp.mean(x, axis=-1, keepdims=True)
    xc = x - mu
    var = jnp.mean(xc * xc, axis=-1, keepdims=True)
    return xc * lax.rsqrt(var + NORM_EPS)


def _scan_chunk(a, b, reverse):
    n = a.shape[0]
    row = lax.broadcasted_iota(jnp.int32, a.shape, 0)
    s = 1
    while s < n:
        if reverse:
            a_sh = pltpu.roll(a, n - s, 0)
            b_sh = pltpu.roll(b, n - s, 0)
            valid = row < n - s
        else:
            a_sh = pltpu.roll(a, s, 0)
            b_sh = pltpu.roll(b, s, 0)
            valid = row >= s
        b = a * jnp.where(valid, b_sh, 0.0) + b
        a = a * jnp.where(valid, a_sh, 1.0)
        s *= 2
    return a, b


def _lru_kernel(xa_ref, xap_ref, xan_ref, ga_ref, cw_ref, cb_ref, wg_ref, gb_ref, lam_ref, o_ref,
                xc_ref, carry_ref, hb_ref, *, nt):
    p = pl.program_id(1)
    t = pl.program_id(2)
    tt = _tile_pos(p, t, nt)

    @pl.when(t == 0)
    def _():
        carry_ref[...] = jnp.zeros_like(carry_ref)

    xc_ref[...] = _dwconv_tile(xa_ref[...], xap_ref[...], xan_ref[...], cw_ref[...], cb_ref[...],
                               tt == 0, tt == nt - 1)

    def direction(ci, d):
        xc = xc_ref[_chunk_rows(ci), :]
        pre = _bdot(xc.astype(BF16), wg_ref[d]) + gb_ref[d]
        r = _sigmoid(pre[:, :LRU_W])
        i = _sigmoid(pre[:, LRU_W:])
        log_a = (-LRU_C * _softplus(-lam_ref[d])) * r
        a = jnp.exp(log_a)
        b = jnp.sqrt(-jnp.tanh(log_a) * (a * a + 1.0)) * (i * xc)
        big_a, big_b = _scan_chunk(a, b, reverse=(d == 1))
        h = big_a * carry_ref[...] + big_b
        carry_ref[...] = h[0:1, :] if d == 1 else h[CHUNK - 1:CHUNK, :]
        return h

    @pl.when(p == 0)
    def _():
        def body(i, c):
            ci = NCH - 1 - i
            hb_ref[tt * NCH + ci] = direction(ci, 1)
            return c
        lax.fori_loop(0, NCH, body, 0)

    @pl.when(p == 1)
    def _():
        def body(ci, c):
            h = direction(ci, 0) + hb_ref[tt * NCH + ci]
            o_ref[_chunk_rows(ci), :] = (jax.nn.gelu(ga_ref[_chunk_rows(ci), :]) * h).astype(o_ref.dtype)
            return c
        lax.fori_loop(0, NCH, body, 0)


def _lru_call(nb, seq, xa, ga, cw, cb, wg, gb, lam):
    nt = seq // SEQ_TILE
    prev, nxt = _halo_specs(LRU_W, nt, nb * seq)
    in_specs = [_both_spec(LRU_W, nt), prev, nxt, _fwd_spec(LRU_W, nt),
                _const_spec(cw.shape), _const_spec(cb.shape), _const_spec(wg.shape),
                _const_spec(gb.shape), _const_spec(lam.shape)]
    scratch = [pltpu.VMEM((SEQ_TILE, LRU_W), F32), pltpu.VMEM((1, LRU_W), F32),
               pltpu.VMEM((seq // CHUNK, CHUNK, LRU_W), F32)]
    return _mixer_call(_lru_kernel, "lru", nb, seq, (xa, xa, xa, ga, cw, cb, wg, gb, lam), in_specs, scratch)


def _ssd_kernel(xbc_ref, xbcp_ref, xbcn_ref, dt_ref, z_ref, cw_ref, cb_ref, dtb_ref, alog_ref, dexp_ref, nw_ref,
                o_ref, xc_ref, h_ref, hb_ref, *, nt):
    p = pl.program_id(1)
    t = pl.program_id(2)
    tt = _tile_pos(p, t, nt)
    nhd = 2 * SSD_HEADS

    @pl.when(t == 0)
    def _():
        h_ref[...] = jnp.zeros_like(h_ref)

    xc_ref[...] = _silu(_dwconv_tile(xbc_ref[...], xbcp_ref[...], xbcn_ref[...], cw_ref[...], cb_ref[...],
                                     tt == 0, tt == nt - 1))
    lower, upper = _tri_masks()
    le = lower.astype(F32)
    ge = upper.astype(F32)
    lane = lax.broadcasted_iota(jnp.int32, (1, CHUNK), 1)
    neg_a = jnp.where(lane < nhd, -jnp.exp(alog_ref[...]), 0.0)

    def gates(ci):
        dt_col = _softplus(dt_ref[_chunk_rows(ci), :] + dtb_ref[...])
        la_col = dt_col * neg_a
        dt_row = dt_col.T[0:nhd, :]
        la_row = la_col.T[0:nhd, :]
        rowid = lax.broadcasted_iota(jnp.int32, (nhd, CHUNK), 0)
        cum_row = jnp.where(rowid < SSD_HEADS, _xdot(la_row, ge), _xdot(la_row, le))
        colid = lax.broadcasted_iota(jnp.int32, (CHUNK, CHUNK), 1)
        cum_col = jnp.where(colid < SSD_HEADS, _xdot(le, la_col), _xdot(ge, la_col))
        tot_row = jnp.sum(la_row, axis=1, keepdims=True)
        return dt_row, cum_row, cum_col, tot_row

    def update_state(ci, d, dt_row, cum_row, tot_row):
        xc = xc_ref[_chunk_rows(ci), :]
        bm_t = xc[:, SSD_W:SSD_W + 2 * SSD_STATE].T
        w_row = jnp.exp(tot_row - cum_row) * dt_row
        for h in range(SSD_HEADS):
            hd = d * SSD_HEADS + h
            g = h // (SSD_HEADS // SSD_GROUPS)
            kt = (bm_t[g * SSD_STATE:(g + 1) * SSD_STATE, :] * w_row[hd:hd + 1, :]).astype(BF16)
            s = _bdot(kt, xc[:, h * SSD_HEAD_DIM:(h + 1) * SSD_HEAD_DIM].astype(BF16))
            h_ref[hd] = jnp.exp(tot_row[hd:hd + 1, :]) * h_ref[hd] + s

    @pl.when(p == 0)
    def _():
        def body(i, c):
            ci = NCH - 1 - i
            gc = tt * NCH + ci
            hb_ref[gc] = h_ref[SSD_HEADS:nhd]
            dt_row, cum_row, _, tot_row = gates(ci)
            update_state(ci, 1, dt_row, cum_row, tot_row)
            return c
        lax.fori_loop(0, NCH, body, 0)

    @pl.when(p == 1)
    def _():
        def body(ci, c):
            gc = tt * NCH + ci
            rows = _chunk_rows(ci)
            xc = xc_ref[rows, :]
            dt_row, cum_row, cum_col, tot_row = gates(ci)
            e_col = jnp.exp(cum_col)
            bm = xc[:, SSD_W:SSD_W + 2 * SSD_STATE]
            cm = xc[:, SSD_W + 2 * SSD_STATE:]
            bm_t = bm.T
            ys = []
            for h in range(SSD_HEADS):
                g = h // (SSD_HEADS // SSD_GROUPS)
                hf, hb = h, SSD_HEADS + h
                cm_g = cm[:, g * SSD_STATE:(g + 1) * SSD_STATE].astype(BF16)
                gmat = _bdot(cm_g, bm_t[g * SSD_STATE:(g + 1) * SSD_STATE, :].astype(BF16))
                dec_f = jnp.where(lower, jnp.exp(cum_col[:, hf:hf + 1] - cum_row[hf:hf + 1, :]), 0.0)
                dec_b = jnp.where(upper, jnp.exp(cum_col[:, hb:hb + 1] - cum_row[hb:hb + 1, :]), 0.0)
                mmat = gmat * (dec_f * dt_row[hf:hf + 1, :] + dec_b * dt_row[hb:hb + 1, :])
                xs_h = xc[:, h * SSD_HEAD_DIM:(h + 1) * SSD_HEAD_DIM]
                y = _bdot(mmat.astype(BF16), xs_h.astype(BF16))
                y = y + _bdot(cm_g, h_ref[hf].astype(BF16)) * e_col[:, hf:hf + 1]
                y = y + _bdot(cm_g, hb_ref[gc, h].astype(BF16)) * e_col[:, hb:hb + 1]
                ys.append(y)
            y = jnp.concatenate(ys, axis=1) + dexp_ref[...] * xc[:, :SSD_W]
            y = y * _silu(z_ref[rows, :])
            y = y * lax.rsqrt(jnp.mean(y * y, axis=-1, keepdims=True) + NORM_EPS) * nw_ref[...]
            o_ref[rows, :] = y.astype(o_ref.dtype)
            update_state(ci, 0, dt_row, cum_row, tot_row)
            return c
        lax.fori_loop(0, NCH, body, 0)


def _ssd_call(nb, seq, xbc, dt, z, cw, cb, dtb, alog, dexp, nw):
    nt = seq // SEQ_TILE
    prev, nxt = _halo_specs(SSD_CONV_W, nt, nb * seq)
    in_specs = [_both_spec(SSD_CONV_W, nt), prev, nxt, _both_spec(CHUNK, nt), _fwd_spec(SSD_W, nt)] + \
               [_const_spec(a.shape) for a in (cw, cb, dtb, alog, dexp, nw)]
    scratch = [pltpu.VMEM((SEQ_TILE, SSD_CONV_W), F32),
               pltpu.VMEM((2 * SSD_HEADS, SSD_STATE, SSD_HEAD_DIM), F32),
               pltpu.VMEM((seq // CHUNK, SSD_HEADS, SSD_STATE, SSD_HEAD_DIM), F32)]
    return _mixer_call(_ssd_kernel, "ssd", nb, seq, (xbc, xbc, xbc, dt, z, cw, cb, dtb, alog, dexp, nw),
                       in_specs, scratch)


def _rope_rot(x):
    w = x.shape[1]
    lane = lax.broadcasted_iota(jnp.int32, x.shape, 1)
    first = (lane % RET_QK) < RET_QK // 2
    return jnp.where(first, pltpu.roll(x, w - RET_QK // 2, 1), pltpu.roll(x, RET_QK // 2, 1))


def _ret_kernel(rq_ref, rk_ref, rv_ref, rg_ref, cos_ref, sin_ref, dmat_ref, gin_ref, gst_ref, g128_ref, nw_ref,
                o_ref, q_ref, kt_ref, h_ref, hb_ref, *, nt):
    p = pl.program_id(1)
    t = pl.program_id(2)
    tt = _tile_pos(p, t, nt)
    nh = RET_HEADS

    @pl.when(t == 0)
    def _():
        h_ref[...] = jnp.zeros_like(h_ref)

    cos = cos_ref[...]
    sin = sin_ref[...]
    k = rk_ref[...]
    k = k * cos + _rope_rot(k) * sin
    for c in range(NCH):
        kt_ref[c] = k[c * CHUNK:(c + 1) * CHUNK, :].T

    def update_state(ci, d):
        v = rv_ref[_chunk_rows(ci), :]
        kt = kt_ref[ci]
        for h in range(nh):
            ks = (kt[h * RET_QK:(h + 1) * RET_QK, :] * gst_ref[d, h]).astype(BF16)
            s = _bdot(ks, v[:, h * RET_V:(h + 1) * RET_V].astype(BF16))
            h_ref[d, h] = g128_ref[h] * h_ref[d, h] + s

    @pl.when(p == 0)
    def _():
        def body(i, c):
            ci = NCH - 1 - i
            hb_ref[tt * NCH + ci] = h_ref[1]
            update_state(ci, 1)
            return c
        lax.fori_loop(0, NCH, body, 0)

    @pl.when(p == 1)
    def _():
        q = rq_ref[...]
        q_ref[...] = (q * cos + _rope_rot(q) * sin) * (RET_QK ** -0.5)

        def body(ci, c):
            gc = tt * NCH + ci
            rows = _chunk_rows(ci)
            q = q_ref[rows, :]
            v = rv_ref[rows, :]
            kt = kt_ref[ci]
            outs = []
            for h in range(nh):
                qh = q[:, h * RET_QK:(h + 1) * RET_QK].astype(BF16)
                s = _bdot(qh, kt[h * RET_QK:(h + 1) * RET_QK, :].astype(BF16))
                o = _bdot((s * dmat_ref[h]).astype(BF16), v[:, h * RET_V:(h + 1) * RET_V].astype(BF16))
                o = o + _bdot(qh, h_ref[0, h].astype(BF16)) * gin_ref[0, h]
                o = o + _bdot(qh, hb_ref[gc, h].astype(BF16)) * gin_ref[1, h]
                outs.append(_head_norm(o))
            y = jnp.concatenate(outs, axis=1) * nw_ref[...]
            o_ref[rows, :] = (_silu(rg_ref[rows, :]) * y).astype(o_ref.dtype)
            update_state(ci, 0)
            return c
        lax.fori_loop(0, NCH, body, 0)


def _ret_call(nb, seq, rq, rk, rv, rg, cos, sin, dmat, gin, gst, g128, nw):
    nt = seq // SEQ_TILE
    w = RET_HEADS * RET_QK
    tab_both = pl.BlockSpec((SEQ_TILE, w), lambda b, p, t: (_tile_pos(p, t, nt), 0))
    in_specs = [_fwd_spec(w, nt), _both_spec(w, nt), _both_spec(512, nt), _fwd_spec(512, nt), tab_both, tab_both] + \
               [_const_spec(a.shape) for a in (dmat, gin, gst, g128, nw)]
    scratch = [pltpu.VMEM((SEQ_TILE, w), F32), pltpu.VMEM((NCH, w, CHUNK), F32),
               pltpu.VMEM((2, RET_HEADS, RET_QK, RET_V), F32),
               pltpu.VMEM((seq // CHUNK, RET_HEADS, RET_QK, RET_V), F32)]
    return _mixer_call(_ret_kernel, "ret", nb, seq, (rq, rk, rv, rg, cos, sin, dmat, gin, gst, g128, nw),
                       in_specs, scratch)


def _mlstm_kernel(qk_ref, qkp_ref, qkn_ref, v_ref, mg_ref, mo_ref, cw_ref, cb_ref, gb_ref, nw_ref,
                  o_ref, qk_s, c_ref, m_ref, cb_stash, mb_stash, *, nt):
    p = pl.program_id(1)
    t = pl.program_id(2)
    tt = _tile_pos(p, t, nt)
    nh = ML_HEADS
    nhd = 2 * nh
    kw = nh * ML_QK

    @pl.when(t == 0)
    def _():
        c_ref[...] = jnp.zeros_like(c_ref)
        m_ref[...] = jnp.zeros_like(m_ref)

    qk_s[...] = _silu(_dwconv_tile(qk_ref[...], qkp_ref[...], qkn_ref[...], cw_ref[...], cb_ref[...],
                                   tt == 0, tt == nt - 1))
    lower, upper = _tri_masks()
    le = lower.astype(F32)
    ge = upper.astype(F32)
    ones_blk = jnp.ones((CHUNK, ML_V), BF16)

    def gates(ci):
        g_col = mg_ref[_chunk_rows(ci), :] + gb_ref[...]
        g_row = g_col.T
        li_row = g_row[0:nhd, :]
        lf_row = -_softplus(-g_row[nhd:2 * nhd, :])
        rowid = lax.broadcasted_iota(jnp.int32, (nhd, CHUNK), 0)
        b_row = jnp.where(rowid < nh, _xdot(lf_row, ge), _xdot(lf_row, le))
        tot = jnp.sum(lf_row, axis=1, keepdims=True)
        return g_col, li_row, b_row, tot

    def update_state(ci, d, li_row, b_row, tot):
        w_row = tot - b_row + li_row
        m_loc = jnp.max(w_row, axis=1, keepdims=True)
        e_row = jnp.exp(w_row - m_loc)
        m_prev = m_ref[...]
        m_new = jnp.maximum(tot + m_prev, m_loc)
        dec = jnp.exp(tot + m_prev - m_new)
        inj = jnp.exp(m_loc - m_new)
        rows = _chunk_rows(ci)
        kt = (qk_s[rows, kw:2 * kw] * (ML_QK ** -0.5)).T
        v = v_ref[rows, :]
        for h in range(nh):
            hd = d * nh + h
            ks = (kt[h * ML_QK:(h + 1) * ML_QK, :] * e_row[hd:hd + 1, :]).astype(BF16)
            vaug = jnp.concatenate([v[:, h * ML_V:(h + 1) * ML_V].astype(BF16), ones_blk], axis=1)
            s = _bdot(ks, vaug)
            c_ref[hd] = dec[hd:hd + 1, 0:1] * c_ref[hd] + inj[hd:hd + 1, 0:1] * s
        rowid = lax.broadcasted_iota(jnp.int32, m_prev.shape, 0)
        keep = (rowid >= nh) if d == 0 else (rowid < nh)
        m_ref[...] = jnp.where(keep, m_prev, m_new)

    def direction_out(ci, d, g_col, li_row, b_row, c_aug, m_prev):
        rows = _chunk_rows(ci)
        q = qk_s[rows, 0:kw]
        kt = (qk_s[rows, kw:2 * kw] * (ML_QK ** -0.5)).T
        v = v_ref[rows, :]
        lf_col = -_softplus(-g_col)
        b_col = _xdot(le, lf_col) if d == 0 else _xdot(ge, lf_col)
        mask = lower if d == 0 else upper
        outs = []
        for h in range(nh):
            hd = d * nh + h
            qh = q[:, h * ML_QK:(h + 1) * ML_QK].astype(BF16)
            bi = b_col[:, nhd + hd:nhd + hd + 1]
            intra = jnp.where(mask, bi + (li_row[hd:hd + 1, :] - b_row[hd:hd + 1, :]), -jnp.inf)
            inter = bi + m_prev[hd:hd + 1, 0:1]
            m_row = jnp.maximum(inter, jnp.max(intra, axis=1, keepdims=True))
            s = _bdot(qh, kt[h * ML_QK:(h + 1) * ML_QK, :].astype(BF16))
            smat = (s * jnp.exp(intra - m_row)).astype(BF16)
            vaug = jnp.concatenate([v[:, h * ML_V:(h + 1) * ML_V].astype(BF16), ones_blk], axis=1)
            nd = _bdot(smat, vaug) + jnp.exp(inter - m_row) * _bdot(qh, c_aug[h].astype(BF16))
            den = jnp.maximum(jnp.abs(nd[:, ML_V:]), jnp.exp(-m_row))
            outs.append(nd[:, :ML_V] / den)
        return outs

    @pl.when(p == 0)
    def _():
        def body(i, c):
            ci = NCH - 1 - i
            gc = tt * NCH + ci
            cb_stash[gc] = c_ref[nh:nhd]
            mb_stash[gc] = m_ref[...]
            _, li_row, b_row, tot = gates(ci)
            update_state(ci, 1, li_row, b_row, tot)
            return c
        lax.fori_loop(0, NCH, body, 0)

    @pl.when(p == 1)
    def _():
        def body(ci, c):
            gc = tt * NCH + ci
            rows = _chunk_rows(ci)
            g_col, li_row, b_row, tot = gates(ci)
            hf = direction_out(ci, 0, g_col, li_row, b_row, c_ref[0:nh], m_ref[...])
            hb = direction_out(ci, 1, g_col, li_row, b_row, cb_stash[gc], mb_stash[gc])
            y = jnp.concatenate([_head_norm(f + b) for f, b in zip(hf, hb)], axis=1) * nw_ref[...]
            o_ref[rows, :] = (_sigmoid(mo_ref[rows, :]) * y).astype(o_ref.dtype)
            update_state(ci, 0, li_row, b_row, tot)
            return c
        lax.fori_loop(0, NCH, body, 0)


def _mlstm_call(nb, seq, mqk, mv, mg, mo, cw, cb, gb, nw):
    nt = seq // SEQ_TILE
    kw2 = 2 * ML_HEADS * ML_QK
    prev, nxt = _halo_specs(kw2, nt, nb * seq)
    in_specs = [_both_spec(kw2, nt), prev, nxt, _both_spec(512, nt), _both_spec(CHUNK, nt), _fwd_spec(512, nt)] + \
               [_const_spec(a.shape) for a in (cw, cb, gb, nw)]
    scratch = [pltpu.VMEM((SEQ_TILE, kw2), F32),
               pltpu.VMEM((2 * ML_HEADS, ML_QK, 2 * ML_V), F32),
               pltpu.VMEM((2 * ML_HEADS, CHUNK), F32),
               pltpu.VMEM((seq // CHUNK, ML_HEADS, ML_QK, 2 * ML_V), F32),
               pltpu.VMEM((seq // CHUNK, 2 * ML_HEADS, CHUNK), F32)]
    return _mixer_call(_mlstm_kernel, "mlstm", nb, seq, (mqk, mqk, mqk, mv, mg, mo, cw, cb, gb, nw),
                       in_specs, scratch)


def _pad_cols(a, width):
    return jnp.pad(a, [(0, 0)] * (a.ndim - 1) + [(0, width - a.shape[-1])])


def _prepare(w_in, lru_conv_w, lru_conv_b, lru_gate_w, lru_gate_b, lru_lambda, ssd_conv_w, ssd_conv_b, ssd_dt_bias,
             ssd_a_log, ssd_d, ssd_norm_w, ret_norm_w, mlstm_conv_w, mlstm_conv_b, mlstm_gate_b, mlstm_norm_w,
             w_branch, w_out, w_ffn_in, w_ffn_out, ln_g, ln_b):
    sp = {}
    off = 0
    for name, w in (("xa", 512), ("ga", 512), ("z", 512), ("xbc", 768), ("dt", 16), ("rq", 256), ("rk", 256),
                    ("rv", 512), ("rg", 512), ("mqk", 512), ("mv", 512), ("mo", 512), ("mg", 16), ("mix", 4096)):
        sp[name] = w_in[:, :, off:off + w]
        off += w
    mg = sp["mg"].reshape(DEPTH, D_MODEL, 2, 2, ML_HEADS)
    mg = jnp.moveaxis(mg, 3, 2).reshape(DEPTH, D_MODEL, 16)
    cols = {"dt": _pad_cols(sp["dt"], CHUNK), "mg": _pad_cols(mg, CHUNK)}
    wp = jnp.concatenate([cols.get(n, sp[n]) for n, _ in PROJ_OUTS], axis=-1).astype(BF16)
    gb = jnp.moveaxis(mlstm_gate_b, 2, 1).reshape(DEPTH, 1, 16)

    nf = D_FF // FF_CHUNK
    w_gate = w_ffn_in[..., :D_FF].reshape(DEPTH, 2, D_MODEL, nf, FF_CHUNK)
    w_up = w_ffn_in[..., D_FF:].reshape(DEPTH, 2, D_MODEL, nf, FF_CHUNK)

    eye = jnp.eye(LRU_BLOCKS, dtype=F32)
    dense = jnp.einsum("ldghij,hk->ldghikj", lru_gate_w, eye).reshape(DEPTH, 2, 2, LRU_W, LRU_W)
    lru_wg = jnp.concatenate([dense[:, :, 0], dense[:, :, 1]], axis=-1)

    return dict(
        wp=wp, wmix=sp["mix"].astype(BF16),
        ffn_g=jnp.moveaxis(w_gate, 3, 2).astype(BF16), ffn_u=jnp.moveaxis(w_up, 3, 2).astype(BF16),
        ffn_d=w_ffn_out.reshape(DEPTH, 2, nf, FF_CHUNK, D_MODEL).astype(BF16),
        ln_g=ln_g[:, :, None, :], ln_b=ln_b[:, :, None, :],
        wb=w_branch.astype(BF16), wo=w_out.astype(BF16),
        lru_cw=lru_conv_w, lru_cb=lru_conv_b[:, None, :], lru_wg=lru_wg.astype(BF16),
        lru_gb=lru_gate_b.reshape(DEPTH, 2, 1, 2 * LRU_W), lru_lam=lru_lambda[:, :, None, :],
        ssd_cw=ssd_conv_w, ssd_cb=ssd_conv_b[:, None, :],
        ssd_dtb=_pad_cols(ssd_dt_bias.reshape(DEPTH, 1, 16), CHUNK),
        ssd_alog=_pad_cols(ssd_a_log.reshape(DEPTH, 1, 16), CHUNK),
        ssd_dexp=jnp.repeat(ssd_d, SSD_HEAD_DIM, axis=-1)[:, None, :], ssd_nw=ssd_norm_w[:, None, :],
        ret_nw=ret_norm_w[:, None, :],
        ml_cw=mlstm_conv_w, ml_cb=mlstm_conv_b[:, None, :], ml_gb=_pad_cols(gb, CHUNK),
        ml_nw=mlstm_norm_w[:, None, :],
    )


def _ret_tables(seq):
    d = RET_QK
    inv = ROPE_BASE ** (-jnp.arange(0, d, 2, dtype=F32) / d)
    ang = jnp.arange(seq, dtype=F32)[:, None] * inv[None, :]
    cos = jnp.tile(jnp.cos(ang), (1, 2 * RET_HEADS))
    sin = jnp.tile(jnp.concatenate([-jnp.sin(ang), jnp.sin(ang)], axis=1), (1, RET_HEADS))
    lg = jnp.log1p(-jnp.exp2(-5.0 - jnp.arange(RET_HEADS, dtype=F32)))[:, None, None]
    pos = jnp.arange(CHUNK, dtype=F32)
    dmat = jnp.exp(lg * jnp.abs(pos[:, None] - pos[None, :])[None])
    ones = jnp.ones((1, 1, RET_V), F32)
    gin = jnp.stack([jnp.exp(lg * (pos + 1.0)[None, :, None]) * ones,
                     jnp.exp(lg * (CHUNK - pos)[None, :, None]) * ones])
    ones = jnp.ones((1, RET_QK, 1), F32)
    gst = jnp.stack([jnp.exp(lg * (CHUNK - 1.0 - pos)[None, None, :]) * ones,
                     jnp.exp(lg * pos[None, None, :]) * ones])
    g128 = jnp.exp(lg * float(CHUNK)) * jnp.ones((1, RET_QK, RET_V), F32)
    return cos, sin, dmat, gin, gst, g128


def _trunk(x3, pw):
    nb, seq, _ = x3.shape
    x = x3.reshape(nb * seq, D_MODEL)
    tabs = _ret_tables(seq)
    for l in range(DEPTH):
        x = _ffn_call(x, pw["ffn_g"][l, 0], pw["ffn_u"][l, 0], pw["ffn_d"][l, 0], pw["ln_g"][l, 0], pw["ln_b"][l, 0])
        pr = dict(zip([n for n, _ in PROJ_OUTS], _proj_call(x, pw["wp"][l])))
        ya = _lru_call(nb, seq, pr["xa"], pr["ga"], pw["lru_cw"][l], pw["lru_cb"][l], pw["lru_wg"][l],
                       pw["lru_gb"][l], pw["lru_lam"][l])
        yb = _ssd_call(nb, seq, pr["xbc"], pr["dt"], pr["z"], pw["ssd_cw"][l], pw["ssd_cb"][l], pw["ssd_dtb"][l],
                       pw["ssd_alog"][l], pw["ssd_dexp"][l], pw["ssd_nw"][l])
        yc = _ret_call(nb, seq, pr["rq"], pr["rk"], pr["rv"], pr["rg"], *tabs, pw["ret_nw"][l])
        yd = _mlstm_call(nb, seq, pr["mqk"], pr["mv"], pr["mg"], pr["mo"], pw["ml_cw"][l], pw["ml_cb"][l],
                         pw["ml_gb"][l], pw["ml_nw"][l])
        x = _merge_call(x, (ya, yb, yc, yd), pw["wmix"][l], pw["wb"][l], pw["wo"][l], pw["ln_g"][l, 1],
                        pw["ln_b"][l, 1])
        x = _ffn_call(x, pw["ffn_g"][l, 1], pw["ffn_u"][l, 1], pw["ffn_d"][l, 1], pw["ln_g"][l, 2], pw["ln_b"][l, 2])
    return x.reshape(nb, seq, D_MODEL)


def kernel(x_prompt, x_sample, w_in, lru_conv_w, lru_conv_b, lru_gate_w, lru_gate_b, lru_lambda, ssd_conv_w,
           ssd_conv_b, ssd_dt_bias, ssd_a_log, ssd_d, ssd_norm_w, ret_norm_w, mlstm_conv_w, mlstm_conv_b,
           mlstm_gate_b, mlstm_norm_w, w_branch, w_out, w_ffn_in, w_ffn_out, ln_g, ln_b):
    pw = _prepare(w_in, lru_conv_w, lru_conv_b, lru_gate_w, lru_gate_b, lru_lambda, ssd_conv_w, ssd_conv_b,
                  ssd_dt_bias, ssd_a_log, ssd_d, ssd_norm_w, ret_norm_w, mlstm_conv_w, mlstm_conv_b, mlstm_gate_b,
                  mlstm_norm_w, w_branch, w_out, w_ffn_in, w_ffn_out, ln_g, ln_b)
    return _trunk(x_prompt, pw), _trunk(x_sample, pw)
```

```python
import functools

import jax
import jax.numpy as jnp
from jax import lax
from jax.experimental import pallas as pl
from jax.experimental.pallas import tpu as pltpu

F32 = jnp.float32
BF16 = jnp.bfloat16
HIGHEST = lax.Precision.HIGHEST

D_MODEL = 1024
DEPTH = 4
D_FF = 2816
CHUNK = 128
LRU_W = 512
LRU_BLOCKS = 8
LRU_C = 8.0
SSD_HEADS = 8
SSD_HEAD_DIM = 64
SSD_W = 512
SSD_STATE = 64
SSD_GROUPS = 2
SSD_CONV_W = SSD_W + 2 * SSD_GROUPS * SSD_STATE
RET_HEADS = 4
RET_QK = 64
RET_V = 128
ROPE_BASE = 10000.0
ML_HEADS = 4
ML_QK = 64
ML_V = 128
DN_ALPHA = (2.0 * DEPTH) ** 0.25
NORM_EPS = 1e-5
TINY = 1.1754944e-38

VMEM_LIMIT_BYTES = 56 * 1024 * 1024
FF_CHUNK = 256
TOKEN_TILE = 512
SEQ_TILE = 512
NCH = SEQ_TILE // CHUNK
SUBLANES = 8
LANES = 128
LRU_UNROLL = 2
SSD_UNROLL = NCH
RET_UNROLL = NCH
ML_UNROLL = NCH

PROJ_OUTS = (("xa", 512), ("ga", 512), ("z", 512), ("xbc", 768), ("dt", 128), ("rq", 256), ("rk", 256),
             ("rv", 512), ("rg", 512), ("mqk", 512), ("mv", 512), ("mo", 512), ("mgi", 128), ("mgf", 128))
SLAB_OUTS = ("xa", "ga", "xbc", "mqk")


def _cparams(sem):
    return pltpu.CompilerParams(dimension_semantics=sem, vmem_limit_bytes=VMEM_LIMIT_BYTES)


def _const_spec(shape):
    nd = len(shape)
    return pl.BlockSpec(shape, lambda *_: (0,) * nd, pipeline_mode=pl.Buffered(1))


def _bdot(a, b):
    return jnp.dot(a, b, preferred_element_type=F32)


def _xdot(a, b):
    return jnp.dot(a, b, preferred_element_type=F32, precision=HIGHEST)


def _sigmoid(x):
    return 0.5 * jnp.tanh(0.5 * x) + 0.5


def _silu(x):
    return x * _sigmoid(x)


def _softplus(x):
    return jnp.maximum(x, 0.0) + jnp.log1p(jnp.exp(-jnp.abs(x)))


def _layer_norm(y, g, b):
    mu = jnp.mean(y, axis=-1, keepdims=True)
    yc = y - mu
    var = jnp.mean(yc * yc, axis=-1, keepdims=True)
    return yc * lax.rsqrt(var + NORM_EPS) * g + b


def _ffn_kernel(x_ref, wg_ref, wu_ref, wd_ref, g_ref, b_ref, o_ref, acc_ref):
    x = x_ref[...]
    xb = x.astype(BF16)
    nf = wg_ref.shape[0]
    for f in range(nf):
        gate = _bdot(xb, wg_ref[f])
        up = _bdot(xb, wu_ref[f])
        h = (_silu(gate) * up).astype(BF16)
        part = _bdot(h, wd_ref[f])
        if f == 0:
            acc_ref[...] = part
        else:
            acc_ref[...] += part
    y = DN_ALPHA * x + 0.5 * acc_ref[...]
    o_ref[...] = _layer_norm(y, g_ref[...], b_ref[...])


def _ffn_call(x, wg, wu, wd, g, b):
    m = x.shape[0]
    tm = TOKEN_TILE
    return pl.pallas_call(
        _ffn_kernel,
        out_shape=jax.ShapeDtypeStruct((m, D_MODEL), F32),
        grid=(m // tm,),
        in_specs=[pl.BlockSpec((tm, D_MODEL), lambda i: (i, 0)),
                  _const_spec(wg.shape), _const_spec(wu.shape), _const_spec(wd.shape),
                  _const_spec(g.shape), _const_spec(b.shape)],
        out_specs=pl.BlockSpec((tm, D_MODEL), lambda i: (i, 0)),
        scratch_shapes=[pltpu.VMEM((tm, D_MODEL), F32)],
        compiler_params=_cparams(("parallel",)),
        name="ffn",
    )(x, wg, wu, wd, g, b)


def _proj_kernel(x_ref, w_ref, *o_refs):
    xb = x_ref[...].astype(BF16)
    off = 0
    for o_ref in o_refs:
        if len(o_ref.shape) == 3:
            for k in range(o_ref.shape[0]):
                o_ref[k] = _bdot(xb, w_ref[:, off:off + LANES])
                off += LANES
        else:
            w = o_ref.shape[1]
            o_ref[...] = _bdot(xb, w_ref[:, off:off + w])
            off += w


def _proj_call(x, wp):
    m = x.shape[0]
    tm = TOKEN_TILE
    shapes, specs = [], []
    for name, w in PROJ_OUTS:
        if name in SLAB_OUTS:
            shapes.append(jax.ShapeDtypeStruct((w // LANES, m, LANES), F32))
            specs.append(pl.BlockSpec((w // LANES, tm, LANES), lambda i: (0, i, 0)))
        else:
            shapes.append(jax.ShapeDtypeStruct((m, w), F32))
            specs.append(pl.BlockSpec((tm, w), lambda i: (i, 0)))
    return pl.pallas_call(
        _proj_kernel,
        out_shape=shapes,
        grid=(m // tm,),
        in_specs=[pl.BlockSpec((tm, D_MODEL), lambda i: (i, 0)), _const_spec(wp.shape)],
        out_specs=specs,
        compiler_params=_cparams(("parallel",)),
        name="proj",
    )(x, wp)


def _merge_kernel(x_ref, ya_ref, yb_ref, yc_ref, yd_ref, wg_ref, wb_ref, wo_ref, g_ref, b_ref, o_ref):
    x = x_ref[...]
    xb = x.astype(BF16)
    merged = None
    for i, y_ref in enumerate((ya_ref, yb_ref, yc_ref, yd_ref)):
        gate = _sigmoid(_bdot(xb, wg_ref[:, i * D_MODEL:(i + 1) * D_MODEL]))
        term = gate * _bdot(y_ref[...], wb_ref[i])
        merged = term if merged is None else merged + term
    out = _bdot(merged.astype(BF16), wo_ref[...])
    o_ref[...] = _layer_norm(DN_ALPHA * x + out, g_ref[...], b_ref[...])


def _merge_call(x, ys, wg, wb, wo, g, b):
    m = x.shape[0]
    tm = TOKEN_TILE
    return pl.pallas_call(
        _merge_kernel,
        out_shape=jax.ShapeDtypeStruct((m, D_MODEL), F32),
        grid=(m // tm,),
        in_specs=[pl.BlockSpec((tm, D_MODEL), lambda i: (i, 0))]
                 + [pl.BlockSpec((tm, 512), lambda i: (i, 0)) for _ in range(4)]
                 + [_const_spec(wg.shape), _const_spec(wb.shape), _const_spec(wo.shape),
                    _const_spec(g.shape), _const_spec(b.shape)],
        out_specs=pl.BlockSpec((tm, D_MODEL), lambda i: (i, 0)),
        compiler_params=_cparams(("parallel",)),
        name="merge",
    )(x, *ys, wg, wb, wo, g, b)


def _tile_pos(p, t, nt):
    return t + (1 - p) * (nt - 1 - 2 * t)


def _p0_tile(p, t, nt):
    return (1 - p) * (nt - 1 - t)


def _both_spec(w, nt):
    return pl.BlockSpec((SEQ_TILE, w), lambda b, p, t: (b * nt + _tile_pos(p, t, nt), 0))


def _fwd_spec(w, nt):
    return pl.BlockSpec((SEQ_TILE, w), lambda b, p, t: (b * nt + p * t, 0))


def _slab_spec_p0(ns, nt):
    return pl.BlockSpec((ns, SEQ_TILE, LANES), lambda b, p, t: (0, b * nt + _p0_tile(p, t, nt), 0))


def _slab_spec_p1(ns, nt):
    return pl.BlockSpec((ns, SEQ_TILE, LANES), lambda b, p, t: (0, b * nt + p * t, 0))


def _slab_halo_specs_p0(ns, nt, n_rows):
    per = SEQ_TILE // SUBLANES
    last = n_rows // SUBLANES - 1
    prev = pl.BlockSpec((ns, SUBLANES, LANES),
                        lambda b, p, t: (0, jnp.maximum((b * nt + _p0_tile(p, t, nt)) * per - 1, 0), 0))
    nxt = pl.BlockSpec((ns, SUBLANES, LANES),
                       lambda b, p, t: (0, jnp.minimum((b * nt + _p0_tile(p, t, nt) + 1) * per, last), 0))
    return prev, nxt


def _conv_to_cache(x_ref, prev_ref, next_ref, xpad_ref, cache_ref, w_ref, b_ref, tt, nt, act):
    ns = x_ref.shape[0]
    for k in range(ns):
        xpad_ref[k, 0:SUBLANES, :] = jnp.where(tt == 0, 0.0, prev_ref[k])
        xpad_ref[k, SUBLANES:SUBLANES + SEQ_TILE, :] = x_ref[k]
        xpad_ref[k, SUBLANES + SEQ_TILE:2 * SUBLANES + SEQ_TILE, :] = jnp.where(tt == nt - 1, 0.0, next_ref[k])
    base = tt * SEQ_TILE
    for k in range(ns):
        w = w_ref[:, k * LANES:(k + 1) * LANES]
        bias = b_ref[:, k * LANES:(k + 1) * LANES]
        for r in range(0, SEQ_TILE, CHUNK):
            taps = [xpad_ref[k, SUBLANES - 2 + j + r:SUBLANES - 2 + j + r + CHUNK, :] for j in range(4)]
            y = w[0:1, :] * taps[0] + w[1:2, :] * taps[1] + w[2:3, :] * taps[2] + w[3:4, :] * taps[3] + bias
            cache_ref[k, pl.ds(pl.multiple_of(base + r, CHUNK), CHUNK), :] = act(y)


def _mixer_call(kernel, name, nb, seq, operands, in_specs, scratch):
    nt = seq // SEQ_TILE
    return pl.pallas_call(
        functools.partial(kernel, nt=nt),
        out_shape=jax.ShapeDtypeStruct((nb * seq, 512), BF16),
        grid=(nb, 2, nt),
        in_specs=in_specs,
        out_specs=pl.BlockSpec((SEQ_TILE, 512), lambda b, p, t: (b * nt + p * t, 0)),
        scratch_shapes=scratch,
        compiler_params=_cparams(("arbitrary", "arbitrary", "arbitrary")),
        name=name,
    )(*operands)


def _tri_masks():
    ii = lax.broadcasted_iota(jnp.int32, (CHUNK, CHUNK), 0)
    jj = lax.broadcasted_iota(jnp.int32, (CHUNK, CHUNK), 1)
    return jj <= ii, jj >= ii


def _chunk_rows(ci):
    return pl.ds(pl.multiple_of(ci * CHUNK, CHUNK), CHUNK)


def _seq_rows(tt, ci):
    return pl.ds(pl.multiple_of(tt * SEQ_TILE + ci * CHUNK, CHUNK), CHUNK)


def _head_norm(x):
    mu = jnp.mean(x, axis=-1, keepdims=True)
    xc = x - mu
    var = jnp.mean(xc * xc, axis=-1, keepdims=True)
    return xc * lax.rsqrt(var + NORM_EPS)


def _scan_rows(a, b, reverse):
    n = a.shape[0]
    row = lax.broadcasted_iota(jnp.int32, a.shape, 0)
    s = 1
    while s < n:
        if reverse:
            a_sh = pltpu.roll(a, n - s, 0)
            b_sh = pltpu.roll(b, n - s, 0)
            valid = row < n - s
        else:
            a_sh = pltpu.roll(a, s, 0)
            b_sh = pltpu.roll(b, s, 0)
            valid = row >= s
        b = a * jnp.where(valid, b_sh, 0.0) + b
        a = a * jnp.where(valid, a_sh, 1.0)
        s *= 2
    return a, b


LRU_GROUPS = CHUNK // SUBLANES


def _load_interleaved(ref, row0):
    ns = ref.shape[0]
    return jnp.concatenate(
        [jnp.concatenate([ref[k, pl.ds(row0 + s, LRU_GROUPS, stride=SUBLANES), :] for k in range(ns)], axis=1)
         for s in range(SUBLANES)], axis=0)


def _lru_kernel(xa_ref, xap_ref, xan_ref, ga_ref, cw_ref, cb_ref, wg_ref, gb_ref, lam_ref, o_ref,
                xpad_ref, xc_ref, carry_ref, hb_ref, y_ref, *, nt):
    p = pl.program_id(1)
    t = pl.program_id(2)
    tt = _tile_pos(p, t, nt)
    ng = LRU_GROUPS

    @pl.when(t == 0)
    def _():
        carry_ref[...] = jnp.zeros_like(carry_ref)

    def direction(ci, d):
        xc = _load_interleaved(xc_ref, pl.multiple_of(tt * SEQ_TILE + ci * CHUNK, CHUNK))
        pre = _bdot(xc.astype(BF16), wg_ref[d]) + gb_ref[d]
        r = _sigmoid(pre[:, :LRU_W])
        i = _sigmoid(pre[:, LRU_W:])
        log_a = (-LRU_C * _softplus(-lam_ref[d])) * r
        a = jnp.exp(log_a)
        v = -jnp.tanh(log_a) * (a * a + 1.0)
        b = (v * lax.rsqrt(jnp.maximum(v, TINY))) * (i * xc)
        sa = [a[s * ng:(s + 1) * ng, :] for s in range(SUBLANES)]
        sb = [b[s * ng:(s + 1) * ng, :] for s in range(SUBLANES)]
        order = range(SUBLANES - 2, -1, -1) if d == 1 else range(1, SUBLANES)
        for s in order:
            q = s + 1 if d == 1 else s - 1
            sb[s] = sa[s] * sb[q] + sb[s]
            sa[s] = sa[s] * sa[q]
        edge = 0 if d == 1 else SUBLANES - 1
        ga, gb = _scan_rows(sa[edge], sb[edge], reverse=(d == 1))
        row = lax.broadcasted_iota(jnp.int32, ga.shape, 0)
        if d == 1:
            ga = jnp.where(row == ng - 1, 1.0, pltpu.roll(ga, ng - 1, 0))
            gb = jnp.where(row == ng - 1, 0.0, pltpu.roll(gb, ng - 1, 0))
        else:
            ga = jnp.where(row == 0, 1.0, pltpu.roll(ga, 1, 0))
            gb = jnp.where(row == 0, 0.0, pltpu.roll(gb, 1, 0))
        h_in = ga * carry_ref[...] + gb
        hs = [sa[s] * h_in + sb[s] for s in range(SUBLANES)]
        carry_ref[...] = hs[0][0:1, :] if d == 1 else hs[SUBLANES - 1][ng - 1:ng, :]
        return jnp.concatenate(hs, axis=0)

    @pl.when(p == 0)
    def _():
        _conv_to_cache(xa_ref, xap_ref, xan_ref, xpad_ref, xc_ref, cw_ref, cb_ref, tt, nt, lambda v: v)

        def body(i, c):
            ci = NCH - 1 - i
            hb_ref[tt * NCH + ci] = direction(ci, 1)
            return c
        lax.fori_loop(0, NCH, body, 0, unroll=LRU_UNROLL)

    @pl.when(p == 1)
    def _():
        def body(ci, c):
            h = direction(ci, 0) + hb_ref[tt * NCH + ci]
            y = jax.nn.gelu(_load_interleaved(ga_ref, pl.multiple_of(ci * CHUNK, CHUNK))) * h
            for s in range(SUBLANES):
                for k in range(LRU_W // LANES):
                    y_ref[k, pl.ds(s, ng, stride=SUBLANES), :] = y[s * ng:(s + 1) * ng, k * LANES:(k + 1) * LANES]
            o_ref[_chunk_rows(ci), :] = jnp.concatenate(
                [y_ref[k] for k in range(LRU_W // LANES)], axis=1).astype(o_ref.dtype)
            return c
        lax.fori_loop(0, NCH, body, 0, unroll=LRU_UNROLL)


def _lru_call(nb, seq, xa, ga, cw, cb, wg, gb, lam):
    nt = seq // SEQ_TILE
    ns = LRU_W // LANES
    prev, nxt = _slab_halo_specs_p0(ns, nt, nb * seq)
    in_specs = [_slab_spec_p0(ns, nt), prev, nxt, _slab_spec_p1(ns, nt),
                _const_spec(cw.shape), _const_spec(cb.shape), _const_spec(wg.shape),
                _const_spec(gb.shape), _const_spec(lam.shape)]
    scratch = [pltpu.VMEM((ns, SEQ_TILE + 2 * SUBLANES, LANES), F32), pltpu.VMEM((ns, seq, LANES), F32),
               pltpu.VMEM((1, LRU_W), F32), pltpu.VMEM((seq // CHUNK, CHUNK, LRU_W), F32),
               pltpu.VMEM((ns, CHUNK, LANES), F32)]
    return _mixer_call(_lru_kernel, "lru", nb, seq, (xa, xa, xa, ga, cw, cb, wg, gb, lam), in_specs, scratch)


def _ssd_kernel(xbc_ref, xbcp_ref, xbcn_ref, dt_ref, z_ref, cw_ref, cb_ref, dtb_ref, alog_ref, dexp_ref, nw_ref,
                o_ref, xpad_ref, xc_ref, hf_ref, hbk_ref, stash_ref, *, nt):
    p = pl.program_id(1)
    t = pl.program_id(2)
    tt = _tile_pos(p, t, nt)
    nh = SSD_HEADS
    nhd = 2 * nh
    npair = nh // 2
    hpg = nh // SSD_GROUPS

    @pl.when(t == 0)
    def _():
        hf_ref[...] = jnp.zeros_like(hf_ref)
        hbk_ref[...] = jnp.zeros_like(hbk_ref)

    lower, upper = _tri_masks()
    le = lower.astype(F32)
    ge = upper.astype(F32)
    lane = lax.broadcasted_iota(jnp.int32, (1, CHUNK), 1)
    neg_a = jnp.where(lane < nhd, -jnp.exp(alog_ref[...]), 0.0)
    rowid = lax.broadcasted_iota(jnp.int32, (nhd, CHUNK), 0)
    colid = lax.broadcasted_iota(jnp.int32, (CHUNK, CHUNK), 1)
    left = colid < SSD_HEAD_DIM

    def row_gates(ci):
        dt_col = _softplus(dt_ref[_chunk_rows(ci), :] + dtb_ref[...])
        la_col = dt_col * neg_a
        dt_row = dt_col.T[0:nhd, :]
        la_row = la_col.T[0:nhd, :]
        cum_row = jnp.where(rowid < nh, _xdot(la_row, ge), _xdot(la_row, le))
        tot_row = jnp.sum(la_row, axis=1, keepdims=True)
        return la_col, dt_row, cum_row, tot_row

    def update_state(rows, d, h_ref, dt_row, cum_row, tot_row):
        bm_t = xc_ref[4, rows, :].T
        w_row = jnp.exp(tot_row - cum_row) * dt_row
        dec = jnp.broadcast_to(jnp.exp(tot_row), (nhd, CHUNK))
        shp = (SSD_STATE, LANES)
        keep = lax.broadcasted_iota(jnp.int32, shp, 1) < SSD_HEAD_DIM
        for pr in range(npair):
            g = (2 * pr) // hpg
            xs_p = xc_ref[pr, rows, :].astype(BF16)
            bg = bm_t[g * SSD_STATE:(g + 1) * SSD_STATE, :]
            hd0 = d * nh + 2 * pr
            s0 = _bdot((bg * w_row[hd0:hd0 + 1, :]).astype(BF16), xs_p)
            s1 = _bdot((bg * w_row[hd0 + 1:hd0 + 2, :]).astype(BF16), xs_p)
            dec_p = jnp.where(keep, jnp.broadcast_to(dec[hd0:hd0 + 1, :], shp),
                              jnp.broadcast_to(dec[hd0 + 1:hd0 + 2, :], shp))
            h_ref[pr] = dec_p * h_ref[pr] + jnp.where(keep, s0, s1)

    @pl.when(p == 0)
    def _():
        _conv_to_cache(xbc_ref, xbcp_ref, xbcn_ref, xpad_ref, xc_ref, cw_ref, cb_ref, tt, nt, _silu)

        def body(i, c):
            ci = NCH - 1 - i
            stash_ref[tt * NCH + ci] = hbk_ref[...]
            _, dt_row, cum_row, tot_row = row_gates(ci)
            update_state(_seq_rows(tt, ci), 1, hbk_ref, dt_row, cum_row, tot_row)
            return c
        lax.fori_loop(0, NCH, body, 0, unroll=SSD_UNROLL)

    @pl.when(p == 1)
    def _():
        def body(ci, c):
            gc = tt * NCH + ci
            rows = _seq_rows(tt, ci)
            orow = _chunk_rows(ci)
            la_col, dt_row, cum_row, tot_row = row_gates(ci)
            cum_col = jnp.where(colid < nh, _xdot(le, la_col), _xdot(ge, la_col))
            bm_t = xc_ref[4, rows, :].T
            cm = xc_ref[5, rows, :]
            gmats, y_inter = [], []
            for g in range(SSD_GROUPS):
                cm_g = cm[:, g * SSD_STATE:(g + 1) * SSD_STATE].astype(BF16)
                gmats.append(_bdot(cm_g, bm_t[g * SSD_STATE:(g + 1) * SSD_STATE, :].astype(BF16)))
                prs = range(g * hpg // 2, (g + 1) * hpg // 2)
                h_cat = jnp.concatenate([hf_ref[pr] for pr in prs] + [stash_ref[gc, pr] for pr in prs], axis=1)
                y_inter.append(_bdot(cm_g, h_cat.astype(BF16)))
            y_parts = []
            for pr in range(npair):
                g = (2 * pr) // hpg
                xs_f = xc_ref[pr, rows, :]
                xs_p = xs_f.astype(BF16)
                ys, ef, eb = [], [], []
                for hl in range(2):
                    hf = 2 * pr + hl
                    hb = nh + hf
                    ci_f = jnp.broadcast_to(cum_col[:, hf:hf + 1], (CHUNK, CHUNK))
                    ci_b = jnp.broadcast_to(cum_col[:, hb:hb + 1], (CHUNK, CHUNK))
                    dec_f = jnp.where(lower, jnp.exp(ci_f - cum_row[hf:hf + 1, :]), 0.0)
                    dec_b = jnp.where(upper, jnp.exp(ci_b - cum_row[hb:hb + 1, :]), 0.0)
                    mmat = gmats[g] * (dec_f * dt_row[hf:hf + 1, :] + dec_b * dt_row[hb:hb + 1, :])
                    ys.append(_bdot(mmat.astype(BF16), xs_p))
                    ef.append(jnp.exp(ci_f))
                    eb.append(jnp.exp(ci_b))
                k = pr - g * hpg // 2
                yi = y_inter[g]
                y = (jnp.where(left, ys[0], ys[1])
                     + yi[:, k * LANES:(k + 1) * LANES] * jnp.where(left, ef[0], ef[1])
                     + yi[:, (hpg // 2 + k) * LANES:(hpg // 2 + k + 1) * LANES] * jnp.where(left, eb[0], eb[1])
                     + dexp_ref[:, pr * LANES:(pr + 1) * LANES] * xs_f)
                y_parts.append(y)
            y = jnp.concatenate(y_parts, axis=1)
            y = y * _silu(z_ref[orow, :])
            y = y * lax.rsqrt(jnp.mean(y * y, axis=-1, keepdims=True) + NORM_EPS) * nw_ref[...]
            o_ref[orow, :] = y.astype(o_ref.dtype)
            update_state(rows, 0, hf_ref, dt_row, cum_row, tot_row)
            return c
        lax.fori_loop(0, NCH, body, 0, unroll=SSD_UNROLL)


def _ssd_call(nb, seq, xbc, dt, z, cw, cb, dtb, alog, dexp, nw):
    nt = seq // SEQ_TILE
    ns = SSD_CONV_W // LANES
    prev, nxt = _slab_halo_specs_p0(ns, nt, nb * seq)
    in_specs = [_slab_spec_p0(ns, nt), prev, nxt, _both_spec(CHUNK, nt), _fwd_spec(SSD_W, nt)] + \
               [_const_spec(a.shape) for a in (cw, cb, dtb, alog, dexp, nw)]
    state = (SSD_HEADS // 2, SSD_STATE, LANES)
    scratch = [pltpu.VMEM((ns, SEQ_TILE + 2 * SUBLANES, LANES), F32), pltpu.VMEM((ns, seq, LANES), F32),
               pltpu.VMEM(state, F32), pltpu.VMEM(state, F32), pltpu.VMEM((seq // CHUNK,) + state, F32)]
    return _mixer_call(_ssd_kernel, "ssd", nb, seq, (xbc, xbc, xbc, dt, z, cw, cb, dtb, alog, dexp, nw),
                       in_specs, scratch)


def _ret_kernel(rq_ref, rk_ref, rv_ref, rg_ref, cos_ref, sin_ref, dmat_ref, gin_ref, gst_ref, g128_ref, nw_ref,
                o_ref, q_ref, kt_ref, h_ref, hb_ref, *, nt):
    p = pl.program_id(1)
    t = pl.program_id(2)
    tt = _tile_pos(p, t, nt)
    nh = RET_HEADS
    half = RET_QK // 2
    hw = nh * half

    @pl.when(t == 0)
    def _():
        h_ref[...] = jnp.zeros_like(h_ref)

    def rope(x):
        a, b = x[:, :hw], x[:, hw:]
        c, s = cos_ref[...], sin_ref[...]
        return jnp.concatenate([a * c - b * s, b * c + a * s], axis=1)

    def head_rows(kt, h):
        return jnp.concatenate([kt[h * half:(h + 1) * half, :], kt[hw + h * half:hw + (h + 1) * half, :]], axis=0)

    def widen(hc, h):
        n = hc.shape[1]
        sizes = [h * half, hw - half, hw - (h + 1) * half]
        z0, z1, z2 = [jnp.zeros((r, n), hc.dtype) for r in sizes]
        parts = [z0, hc[:half, :], z1, hc[half:, :], z2]
        return jnp.concatenate([q for q in parts if q.shape[0] > 0], axis=0)

    def update_state(gc, vrow, d):
        v = rv_ref[vrow, :]
        kt = kt_ref[gc]
        for h in range(nh):
            ks = (head_rows(kt, h) * gst_ref[d, h]).astype(BF16)
            s = _bdot(ks, v[:, h * RET_V:(h + 1) * RET_V].astype(BF16))
            h_ref[d, h] = g128_ref[h] * h_ref[d, h] + s

    @pl.when(p == 0)
    def _():
        k = rope(rk_ref[...])
        for c in range(NCH):
            kt_ref[tt * NCH + c] = k[c * CHUNK:(c + 1) * CHUNK, :].T

        def body(i, c):
            ci = NCH - 1 - i
            gc = tt * NCH + ci
            hb_ref[gc] = h_ref[1]
            update_state(gc, _chunk_rows(ci), 1)
            return c
        lax.fori_loop(0, NCH, body, 0, unroll=RET_UNROLL)

    @pl.when(p == 1)
    def _():
        q_ref[...] = rope(rq_ref[...]) * (RET_QK ** -0.5)
        lane = lax.broadcasted_iota(jnp.int32, (CHUNK, 2 * hw), 1) % hw

        def body(ci, c):
            gc = tt * NCH + ci
            rows = _chunk_rows(ci)
            q = q_ref[rows, :]
            v = rv_ref[rows, :]
            kt = kt_ref[gc].astype(BF16)
            outs = []
            for h in range(nh):
                qh = jnp.where((lane >= h * half) & (lane < (h + 1) * half), q, 0.0).astype(BF16)
                s = _bdot(qh, kt)
                o = _bdot((s * dmat_ref[h]).astype(BF16), v[:, h * RET_V:(h + 1) * RET_V].astype(BF16))
                h_cat = jnp.concatenate([widen(h_ref[0, h], h), widen(hb_ref[gc, h], h)], axis=1).astype(BF16)
                oi = _bdot(qh, h_cat)
                o = o + oi[:, :RET_V] * gin_ref[0, h] + oi[:, RET_V:] * gin_ref[1, h]
                outs.append(_head_norm(o))
            y = jnp.concatenate(outs, axis=1) * nw_ref[...]
            o_ref[rows, :] = (_silu(rg_ref[rows, :]) * y).astype(o_ref.dtype)
            update_state(gc, rows, 0)
            return c
        lax.fori_loop(0, NCH, body, 0, unroll=RET_UNROLL)


def _ret_call(nb, seq, rq, rk, rv, rg, cos, sin, dmat, gin, gst, g128, nw):
    nt = seq // SEQ_TILE
    w = RET_HEADS * RET_QK
    tab_both = pl.BlockSpec((SEQ_TILE, w // 2), lambda b, p, t: (_tile_pos(p, t, nt), 0))
    p0_spec = pl.BlockSpec((SEQ_TILE, w), lambda b, p, t: (b * nt + _p0_tile(p, t, nt), 0))
    in_specs = [_fwd_spec(w, nt), p0_spec, _both_spec(512, nt), _fwd_spec(512, nt), tab_both, tab_both] + \
               [_const_spec(a.shape) for a in (dmat, gin, gst, g128, nw)]
    scratch = [pltpu.VMEM((SEQ_TILE, w), F32), pltpu.VMEM((seq // CHUNK, w, CHUNK), F32),
               pltpu.VMEM((2, RET_HEADS, RET_QK, RET_V), F32),
               pltpu.VMEM((seq // CHUNK, RET_HEADS, RET_QK, RET_V), F32)]
    return _mixer_call(_ret_kernel, "ret", nb, seq, (rq, rk, rv, rg, cos, sin, dmat, gin, gst, g128, nw),
                       in_specs, scratch)


def _mlstm_kernel(qk_ref, qkp_ref, qkn_ref, v_ref, gi_ref, gf_ref, mo_ref, cw_ref, cb_ref, gbi_ref, gbf_ref, nwt_ref,
                  o_ref, xpad_ref, qk_s, c_ref, m_ref, c_stash, m_stash, *, nt):
    p = pl.program_id(1)
    t = pl.program_id(2)
    tt = _tile_pos(p, t, nt)
    nh = ML_HEADS
    nhd = 2 * nh

    @pl.when(t == 0)
    def _():
        c_ref[...] = jnp.zeros_like(c_ref)
        m_ref[...] = jnp.zeros_like(m_ref)

    valid_b, valid_f = _tri_masks()
    le = valid_b.astype(F32)
    ge = valid_f.astype(F32)
    rowid8 = lax.broadcasted_iota(jnp.int32, (nhd, CHUNK), 0)
    colid = lax.broadcasted_iota(jnp.int32, (CHUNK, CHUNK), 1)
    rowid = lax.broadcasted_iota(jnp.int32, (CHUNK, CHUNK), 0)
    top = rowid < ML_QK
    ones_blk = jnp.ones((CHUNK, CHUNK), F32)

    def row_gates(ci):
        rows = _chunk_rows(ci)
        gi_col = gi_ref[rows, :] + gbi_ref[...]
        lf_col = -_softplus(-(gf_ref[rows, :] + gbf_ref[...]))
        li_row = gi_col.T[0:nhd, :]
        lf_row = lf_col.T[0:nhd, :]
        b_row = jnp.where(rowid8 < nh, _xdot(lf_row, ge), _xdot(lf_row, le))
        tot = jnp.sum(lf_row, axis=1, keepdims=True)
        return gi_col, lf_col, li_row, b_row, tot

    def update_state(srow, vrow, d, li_row, b_row, tot):
        w_row = tot - b_row + li_row
        m_loc = jnp.max(w_row, axis=1, keepdims=True)
        e_row = jnp.exp(w_row - m_loc)
        m_prev = m_ref[...]
        m_new = jnp.maximum(tot + m_prev, m_loc)
        dec = jnp.exp(tot + m_prev - m_new)
        inj = jnp.exp(m_loc - m_new)
        v = v_ref[vrow, :]
        for pr in range(nh // 2):
            k_pair = (qk_s[nh // 2 + pr, srow, :] * (ML_QK ** -0.5)).astype(BF16)
            for hl in range(2):
                h = 2 * pr + hl
                hd = d * nh + h
                e_b = jnp.broadcast_to(e_row[hd:hd + 1, :], (CHUNK, CHUNK))
                vt = v[:, h * ML_V:(h + 1) * ML_V].T
                lhs = jnp.concatenate([vt * e_b, e_b], axis=0).astype(BF16)
                s = _bdot(lhs, k_pair)
                c_ref[hd] = (jnp.broadcast_to(dec[hd:hd + 1, :], (2 * ML_V, CHUNK)) * c_ref[hd]
                             + jnp.broadcast_to(inj[hd:hd + 1, :], (2 * ML_V, CHUNK)) * s)
        keep = (rowid8 >= nh) if d == 0 else (rowid8 < nh)
        m_ref[...] = jnp.where(keep, m_prev, m_new)

    def direction_out(d, qts, k_pairs, vts, r_col, b_row, c_aug, m_prev):
        mask = valid_f if d == 0 else valid_b
        outs = []
        for h in range(nh):
            hd = d * nh + h
            rj = jnp.where(mask, r_col[:, hd:hd + 1], -jnp.inf)
            u = jnp.maximum(m_prev[hd:hd + 1, :], jnp.max(rj, axis=0, keepdims=True))
            st = _bdot(k_pairs[h // 2], qts[h])
            pt = (st * jnp.exp(rj - u)).astype(BF16)
            lhs = jnp.concatenate([vts[h], ones_blk], axis=0).astype(BF16)
            nd = _bdot(lhs, pt) + jnp.exp(m_prev[hd:hd + 1, :] - u) * _bdot(c_aug[h].astype(BF16), qts[h])
            floor = jnp.exp(-(b_row[hd:hd + 1, :] + u))
            outs.append(nd[0:ML_V, :] / jnp.maximum(jnp.abs(nd[ML_V:, :]), floor))
        return outs

    @pl.when(p == 0)
    def _():
        _conv_to_cache(qk_ref, qkp_ref, qkn_ref, xpad_ref, qk_s, cw_ref, cb_ref, tt, nt, _silu)

        def body(i, c):
            ci = NCH - 1 - i
            gc = tt * NCH + ci
            c_stash[gc] = c_ref[nh:nhd]
            m_stash[gc] = m_ref[...]
            _, _, li_row, b_row, tot = row_gates(ci)
            update_state(_seq_rows(tt, ci), _chunk_rows(ci), 1, li_row, b_row, tot)
            return c
        lax.fori_loop(0, NCH, body, 0, unroll=ML_UNROLL)

    @pl.when(p == 1)
    def _():
        def body(ci, c):
            gc = tt * NCH + ci
            srow = _seq_rows(tt, ci)
            rows = _chunk_rows(ci)
            gi_col, lf_col, li_row, b_row, tot = row_gates(ci)
            b_col = jnp.where(colid < nh, _xdot(le, lf_col), _xdot(ge, lf_col))
            r_col = gi_col - b_col
            qts, k_pairs, vts = [], [], []
            v = v_ref[rows, :]
            for pr in range(nh // 2):
                qt = qk_s[pr, srow, :].T
                qts.append(jnp.where(top, qt, 0.0).astype(BF16))
                qts.append(jnp.where(top, 0.0, qt).astype(BF16))
                k_pairs.append((qk_s[nh // 2 + pr, srow, :] * (ML_QK ** -0.5)).astype(BF16))
            for h in range(nh):
                vts.append(v[:, h * ML_V:(h + 1) * ML_V].T)
            hf = direction_out(0, qts, k_pairs, vts, r_col, b_row, c_ref[0:nh], m_ref[...])
            hb = direction_out(1, qts, k_pairs, vts, r_col, b_row, c_stash[gc], m_stash[gc])
            ys = []
            for h in range(nh):
                x = hf[h] + hb[h]
                mu = jnp.mean(x, axis=0, keepdims=True)
                xc = x - mu
                var = jnp.mean(xc * xc, axis=0, keepdims=True)
                ys.append((xc * lax.rsqrt(var + NORM_EPS) * nwt_ref[h]).T)
            y = jnp.concatenate(ys, axis=1)
            o_ref[rows, :] = (_sigmoid(mo_ref[rows, :]) * y).astype(o_ref.dtype)
            update_state(srow, rows, 0, li_row, b_row, tot)
            return c
        lax.fori_loop(0, NCH, body, 0, unroll=ML_UNROLL)


def _mlstm_call(nb, seq, mqk, mv, mgi, mgf, mo, cw, cb, gbi, gbf, nwt):
    nt = seq // SEQ_TILE
    ns = 2 * ML_HEADS * ML_QK // LANES
    prev, nxt = _slab_halo_specs_p0(ns, nt, nb * seq)
    in_specs = [_slab_spec_p0(ns, nt), prev, nxt, _both_spec(512, nt), _both_spec(CHUNK, nt), _both_spec(CHUNK, nt),
                _fwd_spec(512, nt)] + [_const_spec(a.shape) for a in (cw, cb, gbi, gbf, nwt)]
    scratch = [pltpu.VMEM((ns, SEQ_TILE + 2 * SUBLANES, LANES), F32), pltpu.VMEM((ns, seq, LANES), F32),
               pltpu.VMEM((2 * ML_HEADS, 2 * ML_V, CHUNK), F32),
               pltpu.VMEM((2 * ML_HEADS, CHUNK), F32),
               pltpu.VMEM((seq // CHUNK, ML_HEADS, 2 * ML_V, CHUNK), F32),
               pltpu.VMEM((seq // CHUNK, 2 * ML_HEADS, CHUNK), F32)]
    return _mixer_call(_mlstm_kernel, "mlstm", nb, seq, (mqk, mqk, mqk, mv, mgi, mgf, mo, cw, cb, gbi, gbf, nwt),
                       in_specs, scratch)


def _pad_cols(a, width):
    return jnp.pad(a, [(0, 0)] * (a.ndim - 1) + [(0, width - a.shape[-1])])


def _prepare(w_in, lru_conv_w, lru_conv_b, lru_gate_w, lru_gate_b, lru_lambda, ssd_conv_w, ssd_conv_b, ssd_dt_bias,
             ssd_a_log, ssd_d, ssd_norm_w, ret_norm_w, mlstm_conv_w, mlstm_conv_b, mlstm_gate_b, mlstm_norm_w,
             w_branch, w_out, w_ffn_in, w_ffn_out, ln_g, ln_b):
    sp = {}
    off = 0
    for name, w in (("xa", 512), ("ga", 512), ("z", 512), ("xbc", 768), ("dt", 16), ("rq", 256), ("rk", 256),
                    ("rv", 512), ("rg", 512), ("mqk", 512), ("mv", 512), ("mo", 512), ("mg", 16), ("mix", 4096)):
        sp[name] = w_in[:, :, off:off + w]
        off += w
    mg = sp["mg"].reshape(DEPTH, D_MODEL, 2, 2, ML_HEADS)

    def halves_major(w):
        w = w.reshape(DEPTH, D_MODEL, RET_HEADS, 2, RET_QK // 2)
        return jnp.moveaxis(w, 3, 2).reshape(DEPTH, D_MODEL, RET_HEADS * RET_QK)

    cols = {"dt": _pad_cols(sp["dt"], CHUNK), "rq": halves_major(sp["rq"]), "rk": halves_major(sp["rk"]),
            "mgi": _pad_cols(mg[:, :, :, 0, :].reshape(DEPTH, D_MODEL, 8), CHUNK),
            "mgf": _pad_cols(mg[:, :, :, 1, :].reshape(DEPTH, D_MODEL, 8), CHUNK)}
    wp = jnp.concatenate([cols[n] if n in cols else sp[n] for n, _ in PROJ_OUTS], axis=-1).astype(BF16)
    gbi = _pad_cols(mlstm_gate_b[:, :, 0, :].reshape(DEPTH, 1, 8), CHUNK)
    gbf = _pad_cols(mlstm_gate_b[:, :, 1, :].reshape(DEPTH, 1, 8), CHUNK)
    ml_nwt = jnp.broadcast_to(mlstm_norm_w.reshape(DEPTH, ML_HEADS, ML_V, 1), (DEPTH, ML_HEADS, ML_V, CHUNK))

    nf = D_FF // FF_CHUNK
    w_gate = w_ffn_in[..., :D_FF].reshape(DEPTH, 2, D_MODEL, nf, FF_CHUNK)
    w_up = w_ffn_in[..., D_FF:].reshape(DEPTH, 2, D_MODEL, nf, FF_CHUNK)

    eye = jnp.eye(LRU_BLOCKS, dtype=F32)
    dense = jnp.einsum("ldghij,hk->ldghikj", lru_gate_w, eye).reshape(DEPTH, 2, 2, LRU_W, LRU_W)
    lru_wg = jnp.concatenate([dense[:, :, 0], dense[:, :, 1]], axis=-1)

    return dict(
        wp=wp, wmix=sp["mix"].astype(BF16),
        ffn_g=jnp.moveaxis(w_gate, 3, 2).astype(BF16), ffn_u=jnp.moveaxis(w_up, 3, 2).astype(BF16),
        ffn_d=w_ffn_out.reshape(DEPTH, 2, nf, FF_CHUNK, D_MODEL).astype(BF16),
        ln_g=ln_g[:, :, None, :], ln_b=ln_b[:, :, None, :],
        wb=w_branch.astype(BF16), wo=w_out.astype(BF16),
        lru_cw=lru_conv_w, lru_cb=lru_conv_b[:, None, :], lru_wg=lru_wg.astype(BF16),
        lru_gb=lru_gate_b.reshape(DEPTH, 2, 1, 2 * LRU_W), lru_lam=lru_lambda[:, :, None, :],
        ssd_cw=ssd_conv_w, ssd_cb=ssd_conv_b[:, None, :],
        ssd_dtb=_pad_cols(ssd_dt_bias.reshape(DEPTH, 1, 16), CHUNK),
        ssd_alog=_pad_cols(ssd_a_log.reshape(DEPTH, 1, 16), CHUNK),
        ssd_dexp=jnp.repeat(ssd_d, SSD_HEAD_DIM, axis=-1)[:, None, :], ssd_nw=ssd_norm_w[:, None, :],
        ret_nw=ret_norm_w[:, None, :],
        ml_cw=mlstm_conv_w, ml_cb=mlstm_conv_b[:, None, :], ml_gbi=gbi, ml_gbf=gbf, ml_nwt=ml_nwt,
    )


def _ret_tables(seq):
    d = RET_QK
    inv = ROPE_BASE ** (-jnp.arange(0, d, 2, dtype=F32) / d)
    ang = jnp.arange(seq, dtype=F32)[:, None] * inv[None, :]
    cos = jnp.tile(jnp.cos(ang), (1, RET_HEADS))
    sin = jnp.tile(jnp.sin(ang), (1, RET_HEADS))
    lg = jnp.log1p(-jnp.exp2(-5.0 - jnp.arange(RET_HEADS, dtype=F32)))[:, None, None]
    pos = jnp.arange(CHUNK, dtype=F32)
    dmat = jnp.exp(lg * jnp.abs(pos[:, None] - pos[None, :])[None])
    ones = jnp.ones((1, 1, RET_V), F32)
    gin = jnp.stack([jnp.exp(lg * (pos + 1.0)[None, :, None]) * ones,
                     jnp.exp(lg * (CHUNK - pos)[None, :, None]) * ones])
    ones = jnp.ones((1, RET_QK, 1), F32)
    gst = jnp.stack([jnp.exp(lg * (CHUNK - 1.0 - pos)[None, None, :]) * ones,
                     jnp.exp(lg * pos[None, None, :]) * ones])
    g128 = jnp.exp(lg * float(CHUNK)) * jnp.ones((1, RET_QK, RET_V), F32)
    return cos, sin, dmat, gin, gst, g128


def _trunk(x3, pw):
    nb, seq, _ = x3.shape
    x = x3.reshape(nb * seq, D_MODEL)
    tabs = _ret_tables(seq)
    for l in range(DEPTH):
        x = _ffn_call(x, pw["ffn_g"][l, 0], pw["ffn_u"][l, 0], pw["ffn_d"][l, 0], pw["ln_g"][l, 0], pw["ln_b"][l, 0])
        pr = dict(zip([n for n, _ in PROJ_OUTS], _proj_call(x, pw["wp"][l])))
        ya = _lru_call(nb, seq, pr["xa"], pr["ga"], pw["lru_cw"][l], pw["lru_cb"][l], pw["lru_wg"][l],
                       pw["lru_gb"][l], pw["lru_lam"][l])
        yb = _ssd_call(nb, seq, pr["xbc"], pr["dt"], pr["z"], pw["ssd_cw"][l], pw["ssd_cb"][l], pw["ssd_dtb"][l],
                       pw["ssd_alog"][l], pw["ssd_dexp"][l], pw["ssd_nw"][l])
        yc = _ret_call(nb, seq, pr["rq"], pr["rk"], pr["rv"], pr["rg"], *tabs, pw["ret_nw"][l])
        yd = _mlstm_call(nb, seq, pr["mqk"], pr["mv"], pr["mgi"], pr["mgf"], pr["mo"], pw["ml_cw"][l], pw["ml_cb"][l],
                         pw["ml_gbi"][l], pw["ml_gbf"][l], pw["ml_nwt"][l])
        x = _merge_call(x, (ya, yb, yc, yd), pw["wmix"][l], pw["wb"][l], pw["wo"][l], pw["ln_g"][l, 1],
                        pw["ln_b"][l, 1])
        x = _ffn_call(x, pw["ffn_g"][l, 1], pw["ffn_u"][l, 1], pw["ffn_d"][l, 1], pw["ln_g"][l, 2], pw["ln_b"][l, 2])
    return x.reshape(nb, seq, D_MODEL)


def kernel(x_prompt, x_sample, w_in, lru_conv_w, lru_conv_b, lru_gate_w, lru_gate_b, lru_lambda, ssd_conv_w,
           ssd_conv_b, ssd_dt_bias, ssd_a_log, ssd_d, ssd_norm_w, ret_norm_w, mlstm_conv_w, mlstm_conv_b,
           mlstm_gate_b, mlstm_norm_w, w_branch, w_out, w_ffn_in, w_ffn_out, ln_g, ln_b):
    pw = _prepare(w_in, lru_conv_w, lru_conv_b, lru_gate_w, lru_gate_b, lru_lambda, ssd_conv_w, ssd_conv_b,
                  ssd_dt_bias, ssd_a_log, ssd_d, ssd_norm_w, ret_norm_w, mlstm_conv_w, mlstm_conv_b, mlstm_gate_b,
                  mlstm_norm_w, w_branch, w_out, w_ffn_in, w_ffn_out, ln_g, ln_b)
    return _trunk(x_prompt, pw), _trunk(x_sample, pw)
```

```python
import functools

import jax
import jax.numpy as jnp
from jax import lax
from jax.experimental import pallas as pl
from jax.experimental.pallas import tpu as pltpu

F32 = jnp.float32
BF16 = jnp.bfloat16
HIGHEST = lax.Precision.HIGHEST

D_MODEL = 1024
DEPTH = 4
D_FF = 2816
CHUNK = 128
LRU_W = 512
LRU_BLOCKS = 8
LRU_C = 8.0
SSD_HEADS = 8
SSD_HEAD_DIM = 64
SSD_W = 512
SSD_STATE = 64
SSD_GROUPS = 2
SSD_CONV_W = SSD_W + 2 * SSD_GROUPS * SSD_STATE
RET_HEADS = 4
RET_QK = 64
RET_V = 128
ROPE_BASE = 10000.0
ML_HEADS = 4
ML_QK = 64
ML_V = 128
DN_ALPHA = (2.0 * DEPTH) ** 0.25
NORM_EPS = 1e-5
LOG2E = 1.4426950408889634
TINY = 1.1754944e-38

VMEM_LIMIT_BYTES = 56 * 1024 * 1024
FF_CHUNK = 256
TOKEN_TILE = 512
SEQ_TILE = 512
NCH = SEQ_TILE // CHUNK
SUBLANES = 8
LANES = 128
LRU_UNROLL = 2
SSD_UNROLL = NCH
RET_UNROLL = NCH
ML_UNROLL = NCH
ML_NROWS = 16

PROJ_OUTS = (("xa", 512), ("ga", 512), ("z", 512), ("xbc", 768), ("dt", 128), ("rq", 256), ("rk", 256),
             ("rv", 512), ("rg", 512), ("mqk", 512), ("mv", 512), ("mo", 512), ("mgi", 128), ("mgf", 128))
SLAB_OUTS = ("xa", "ga", "xbc", "mqk")


def _cparams(sem):
    return pltpu.CompilerParams(dimension_semantics=sem, vmem_limit_bytes=VMEM_LIMIT_BYTES)


def _const_spec(shape):
    nd = len(shape)
    return pl.BlockSpec(shape, lambda *_: (0,) * nd, pipeline_mode=pl.Buffered(1))


def _bdot(a, b):
    return jnp.dot(a, b, preferred_element_type=F32)


def _xdot(a, b):
    return jnp.dot(a, b, preferred_element_type=F32, precision=HIGHEST)


def _sigmoid(x):
    return 0.5 * jnp.tanh(0.5 * x) + 0.5


def _silu(x):
    return x * _sigmoid(x)


def _softplus(x):
    return jnp.maximum(x, 0.0) + jnp.log1p(jnp.exp(-jnp.abs(x)))


def _layer_norm(y, g, b):
    mu = jnp.mean(y, axis=-1, keepdims=True)
    yc = y - mu
    var = jnp.mean(yc * yc, axis=-1, keepdims=True)
    return yc * lax.rsqrt(var + NORM_EPS) * g + b


def _ffn_kernel(x_ref, wi_ref, wd_ref, g_ref, b_ref, o_ref, acc_ref):
    x = x_ref[...]
    xb = x.astype(BF16)
    nf = D_FF // FF_CHUNK
    for f in range(nf):
        gate = _bdot(xb, wi_ref[:, f * FF_CHUNK:(f + 1) * FF_CHUNK])
        up = _bdot(xb, wi_ref[:, D_FF + f * FF_CHUNK:D_FF + (f + 1) * FF_CHUNK])
        h = (_silu(gate) * up).astype(BF16)
        part = _bdot(h, wd_ref[f * FF_CHUNK:(f + 1) * FF_CHUNK, :])
        if f == 0:
            acc_ref[...] = part
        else:
            acc_ref[...] += part
    y = DN_ALPHA * x + 0.5 * acc_ref[...]
    o_ref[...] = _layer_norm(y, g_ref[...], b_ref[...])


def _ffn_call(x, wi, wd, g, b):
    m = x.shape[0]
    tm = TOKEN_TILE
    return pl.pallas_call(
        _ffn_kernel,
        out_shape=jax.ShapeDtypeStruct((m, D_MODEL), F32),
        grid=(m // tm,),
        in_specs=[pl.BlockSpec((tm, D_MODEL), lambda i: (i, 0)),
                  _const_spec(wi.shape), _const_spec(wd.shape), _const_spec(g.shape), _const_spec(b.shape)],
        out_specs=pl.BlockSpec((tm, D_MODEL), lambda i: (i, 0)),
        scratch_shapes=[pltpu.VMEM((tm, D_MODEL), F32)],
        compiler_params=_cparams(("parallel",)),
        name="ffn",
    )(x, wi, wd, g, b)


def _proj_kernel(x_ref, w_ref, *o_refs):
    xb = x_ref[...].astype(BF16)
    off = 0
    for o_ref in o_refs:
        if len(o_ref.shape) == 3:
            w = o_ref.shape[0] * LANES
            res = _bdot(xb, w_ref[:, off:off + w])
            for k in range(o_ref.shape[0]):
                o_ref[k] = res[:, k * LANES:(k + 1) * LANES]
            off += w
        else:
            w = o_ref.shape[1]
            o_ref[...] = _bdot(xb, w_ref[:, off:off + w])
            off += w


def _proj_call(x, wp):
    m = x.shape[0]
    tm = TOKEN_TILE
    shapes, specs = [], []
    for name, w in PROJ_OUTS:
        if name in SLAB_OUTS:
            shapes.append(jax.ShapeDtypeStruct((w // LANES, m, LANES), F32))
            specs.append(pl.BlockSpec((w // LANES, tm, LANES), lambda i: (0, i, 0)))
        else:
            shapes.append(jax.ShapeDtypeStruct((m, w), F32))
            specs.append(pl.BlockSpec((tm, w), lambda i: (i, 0)))
    return pl.pallas_call(
        _proj_kernel,
        out_shape=shapes,
        grid=(m // tm,),
        in_specs=[pl.BlockSpec((tm, D_MODEL), lambda i: (i, 0)), _const_spec(wp.shape)],
        out_specs=specs,
        compiler_params=_cparams(("parallel",)),
        name="proj",
    )(x, wp)


def _merge_kernel(x_ref, ya_ref, yb_ref, yc_ref, yd_ref, wg_ref, wb_ref, wo_ref, g_ref, b_ref, o_ref):
    x = x_ref[...]
    xb = x.astype(BF16)
    merged = None
    for i, y_ref in enumerate((ya_ref, yb_ref, yc_ref, yd_ref)):
        gate = _sigmoid(_bdot(xb, wg_ref[:, i * D_MODEL:(i + 1) * D_MODEL]))
        term = gate * _bdot(y_ref[...], wb_ref[i])
        merged = term if merged is None else merged + term
    out = _bdot(merged.astype(BF16), wo_ref[...])
    o_ref[...] = _layer_norm(DN_ALPHA * x + out, g_ref[...], b_ref[...])


def _merge_call(x, ys, wg, wb, wo, g, b):
    m = x.shape[0]
    tm = TOKEN_TILE
    return pl.pallas_call(
        _merge_kernel,
        out_shape=jax.ShapeDtypeStruct((m, D_MODEL), F32),
        grid=(m // tm,),
        in_specs=[pl.BlockSpec((tm, D_MODEL), lambda i: (i, 0))]
                 + [pl.BlockSpec((tm, 512), lambda i: (i, 0)) for _ in range(4)]
                 + [_const_spec(wg.shape), _const_spec(wb.shape), _const_spec(wo.shape),
                    _const_spec(g.shape), _const_spec(b.shape)],
        out_specs=pl.BlockSpec((tm, D_MODEL), lambda i: (i, 0)),
        compiler_params=_cparams(("parallel",)),
        name="merge",
    )(x, *ys, wg, wb, wo, g, b)


def _tile_pos(p, t, nt):
    return t + (1 - p) * (nt - 1 - 2 * t)


def _p0_tile(p, t, nt):
    return (1 - p) * (nt - 1 - t)


def _both_spec(w, nt):
    return pl.BlockSpec((SEQ_TILE, w), lambda b, p, t: (b * nt + _tile_pos(p, t, nt), 0))


def _fwd_spec(w, nt):
    return pl.BlockSpec((SEQ_TILE, w), lambda b, p, t: (b * nt + p * t, 0))


def _slab_spec_p0(ns, nt):
    return pl.BlockSpec((ns, SEQ_TILE, LANES), lambda b, p, t: (0, b * nt + _p0_tile(p, t, nt), 0))


def _slab_spec_p1(ns, nt):
    return pl.BlockSpec((ns, SEQ_TILE, LANES), lambda b, p, t: (0, b * nt + p * t, 0))


def _slab_halo_specs_p0(ns, nt, n_rows):
    per = SEQ_TILE // SUBLANES
    last = n_rows // SUBLANES - 1
    prev = pl.BlockSpec((ns, SUBLANES, LANES),
                        lambda b, p, t: (0, jnp.maximum((b * nt + _p0_tile(p, t, nt)) * per - 1, 0), 0))
    nxt = pl.BlockSpec((ns, SUBLANES, LANES),
                       lambda b, p, t: (0, jnp.minimum((b * nt + _p0_tile(p, t, nt) + 1) * per, last), 0))
    return prev, nxt


def _conv_to_cache(x_ref, prev_ref, next_ref, xpad_ref, cache_ref, w_ref, b_ref, tt, nt, act):
    ns = x_ref.shape[0]
    for k in range(ns):
        xpad_ref[k, 0:SUBLANES, :] = jnp.where(tt == 0, 0.0, prev_ref[k])
        xpad_ref[k, SUBLANES:SUBLANES + SEQ_TILE, :] = x_ref[k]
        xpad_ref[k, SUBLANES + SEQ_TILE:2 * SUBLANES + SEQ_TILE, :] = jnp.where(tt == nt - 1, 0.0, next_ref[k])
    base = tt * SEQ_TILE
    for k in range(ns):
        w = w_ref[:, k * LANES:(k + 1) * LANES]
        bias = b_ref[:, k * LANES:(k + 1) * LANES]
        for r in range(0, SEQ_TILE, CHUNK):
            taps = [xpad_ref[k, SUBLANES - 2 + j + r:SUBLANES - 2 + j + r + CHUNK, :] for j in range(4)]
            y = w[0:1, :] * taps[0] + w[1:2, :] * taps[1] + w[2:3, :] * taps[2] + w[3:4, :] * taps[3] + bias
            cache_ref[k, pl.ds(pl.multiple_of(base + r, CHUNK), CHUNK), :] = act(y)


def _mixer_call(kernel, name, nb, seq, operands, in_specs, scratch):
    nt = seq // SEQ_TILE
    return pl.pallas_call(
        functools.partial(kernel, nt=nt),
        out_shape=jax.ShapeDtypeStruct((nb * seq, 512), BF16),
        grid=(nb, 2, nt),
        in_specs=in_specs,
        out_specs=pl.BlockSpec((SEQ_TILE, 512), lambda b, p, t: (b * nt + p * t, 0)),
        scratch_shapes=scratch,
        compiler_params=_cparams(("arbitrary", "arbitrary", "arbitrary")),
        name=name,
    )(*operands)


def _tri_masks():
    ii = lax.broadcasted_iota(jnp.int32, (CHUNK, CHUNK), 0)
    jj = lax.broadcasted_iota(jnp.int32, (CHUNK, CHUNK), 1)
    return jj <= ii, jj >= ii


def _rows_to_cols(x):
    r = x.shape[0]
    return jnp.concatenate([x, jnp.zeros((CHUNK - r, x.shape[1]), x.dtype)], axis=0).T


def _chunk_rows(ci):
    return pl.ds(pl.multiple_of(ci * CHUNK, CHUNK), CHUNK)


def _seq_rows(tt, ci):
    return pl.ds(pl.multiple_of(tt * SEQ_TILE + ci * CHUNK, CHUNK), CHUNK)


def _head_norm(x):
    mu = jnp.mean(x, axis=-1, keepdims=True)
    xc = x - mu
    var = jnp.mean(xc * xc, axis=-1, keepdims=True)
    return xc * lax.rsqrt(var + NORM_EPS)


def _scan_rows(a, b, reverse):
    n = a.shape[0]
    row = lax.broadcasted_iota(jnp.int32, a.shape, 0)
    s = 1
    while s < n:
        if reverse:
            a_sh = pltpu.roll(a, n - s, 0)
            b_sh = pltpu.roll(b, n - s, 0)
            valid = row < n - s
        else:
            a_sh = pltpu.roll(a, s, 0)
            b_sh = pltpu.roll(b, s, 0)
            valid = row >= s
        b = a * jnp.where(valid, b_sh, 0.0) + b
        a = a * jnp.where(valid, a_sh, 1.0)
        s *= 2
    return a, b


LRU_GROUPS = CHUNK // SUBLANES


def _load_interleaved(ref, row0):
    ns = ref.shape[0]
    return jnp.concatenate(
        [jnp.concatenate([ref[k, pl.ds(row0 + s, LRU_GROUPS, stride=SUBLANES), :] for k in range(ns)], axis=1)
         for s in range(SUBLANES)], axis=0)


def _lru_kernel(xa_ref, xap_ref, xan_ref, ga_ref, cw_ref, cb_ref, wg_ref, gb_ref, lam_ref, o_ref,
                xpad_ref, xc_ref, carry_ref, hb_ref, y_ref, *, nt):
    p = pl.program_id(1)
    t = pl.program_id(2)
    tt = _tile_pos(p, t, nt)
    ng = LRU_GROUPS

    @pl.when(t == 0)
    def _():
        carry_ref[...] = jnp.zeros_like(carry_ref)

    def direction(ci, d):
        xc = _load_interleaved(xc_ref, pl.multiple_of(tt * SEQ_TILE + ci * CHUNK, CHUNK))
        th = jnp.tanh(_bdot(xc.astype(BF16), wg_ref[d]) + gb_ref[d])
        c_half = (-0.5 * LRU_C) * _softplus(-lam_ref[d])
        log_a = c_half * th[:, :LRU_W] + c_half
        a = jnp.exp(log_a)
        v = -jnp.tanh(log_a) * (a * a + 1.0)
        ix = (th[:, LRU_W:] + 1.0) * (0.5 * xc)
        b = (v * lax.rsqrt(jnp.maximum(v, TINY))) * ix
        sa = [a[s * ng:(s + 1) * ng, :] for s in range(SUBLANES)]
        sb = [b[s * ng:(s + 1) * ng, :] for s in range(SUBLANES)]
        order = range(SUBLANES - 2, -1, -1) if d == 1 else range(1, SUBLANES)
        for s in order:
            q = s + 1 if d == 1 else s - 1
            sb[s] = sa[s] * sb[q] + sb[s]
            sa[s] = sa[s] * sa[q]
        edge = 0 if d == 1 else SUBLANES - 1
        ga, gb = _scan_rows(sa[edge], sb[edge], reverse=(d == 1))
        row = lax.broadcasted_iota(jnp.int32, ga.shape, 0)
        if d == 1:
            ga = jnp.where(row == ng - 1, 1.0, pltpu.roll(ga, ng - 1, 0))
            gb = jnp.where(row == ng - 1, 0.0, pltpu.roll(gb, ng - 1, 0))
        else:
            ga = jnp.where(row == 0, 1.0, pltpu.roll(ga, 1, 0))
            gb = jnp.where(row == 0, 0.0, pltpu.roll(gb, 1, 0))
        h_in = ga * carry_ref[...] + gb
        hs = [sa[s] * h_in + sb[s] for s in range(SUBLANES)]
        carry_ref[...] = hs[0][0:1, :] if d == 1 else hs[SUBLANES - 1][ng - 1:ng, :]
        return jnp.concatenate(hs, axis=0)

    @pl.when(p == 0)
    def _():
        _conv_to_cache(xa_ref, xap_ref, xan_ref, xpad_ref, xc_ref, cw_ref, cb_ref, tt, nt, lambda v: v)

        def body(i, c):
            ci = NCH - 1 - i
            hb_ref[tt * NCH + ci] = direction(ci, 1)
            return c
        lax.fori_loop(0, NCH, body, 0, unroll=LRU_UNROLL)

    @pl.when(p == 1)
    def _():
        def body(ci, c):
            h = direction(ci, 0) + hb_ref[tt * NCH + ci]
            y = jax.nn.gelu(_load_interleaved(ga_ref, pl.multiple_of(ci * CHUNK, CHUNK))) * h
            for s in range(SUBLANES):
                for k in range(LRU_W // LANES):
                    y_ref[k, pl.ds(s, ng, stride=SUBLANES), :] = y[s * ng:(s + 1) * ng, k * LANES:(k + 1) * LANES]
            o_ref[_chunk_rows(ci), :] = jnp.concatenate(
                [y_ref[k] for k in range(LRU_W // LANES)], axis=1).astype(o_ref.dtype)
            return c
        lax.fori_loop(0, NCH, body, 0, unroll=LRU_UNROLL)


def _lru_call(nb, seq, xa, ga, cw, cb, wg, gb, lam):
    nt = seq // SEQ_TILE
    ns = LRU_W // LANES
    prev, nxt = _slab_halo_specs_p0(ns, nt, nb * seq)
    in_specs = [_slab_spec_p0(ns, nt), prev, nxt, _slab_spec_p1(ns, nt),
                _const_spec(cw.shape), _const_spec(cb.shape), _const_spec(wg.shape),
                _const_spec(gb.shape), _const_spec(lam.shape)]
    scratch = [pltpu.VMEM((ns, SEQ_TILE + 2 * SUBLANES, LANES), F32), pltpu.VMEM((ns, seq, LANES), F32),
               pltpu.VMEM((1, LRU_W), F32), pltpu.VMEM((seq // CHUNK, CHUNK, LRU_W), F32),
               pltpu.VMEM((ns, CHUNK, LANES), F32)]
    return _mixer_call(_lru_kernel, "lru", nb, seq, (xa, xa, xa, ga, cw, cb, wg, gb, lam), in_specs, scratch)


def _ssd_kernel(xbc_ref, xbcp_ref, xbcn_ref, dt_ref, z_ref, cw_ref, cb_ref, dtb_ref, alog_ref, dexp_ref, nw_ref,
                o_ref, xpad_ref, xc_ref, hf_ref, hbk_ref, stash_ref, *, nt):
    p = pl.program_id(1)
    t = pl.program_id(2)
    tt = _tile_pos(p, t, nt)
    nh = SSD_HEADS
    nhd = 2 * nh
    npair = nh // 2
    hpg = nh // SSD_GROUPS

    @pl.when(t == 0)
    def _():
        hf_ref[...] = jnp.zeros_like(hf_ref)
        hbk_ref[...] = jnp.zeros_like(hbk_ref)

    lower, upper = _tri_masks()
    le = lower.astype(F32)
    ge = upper.astype(F32)
    lane = lax.broadcasted_iota(jnp.int32, (1, CHUNK), 1)
    neg_a = jnp.where(lane < nhd, -LOG2E * jnp.exp(alog_ref[...]), 0.0)
    rowid = lax.broadcasted_iota(jnp.int32, (nhd, CHUNK), 0)
    colid = lax.broadcasted_iota(jnp.int32, (CHUNK, CHUNK), 1)
    left = colid < SSD_HEAD_DIM

    def row_gates(ci):
        dt_col = _softplus(dt_ref[_chunk_rows(ci), :] + dtb_ref[...])
        la_col = dt_col * neg_a
        dt_row = dt_col.T[0:nhd, :]
        la_row = la_col.T[0:nhd, :]
        cum_row = jnp.where(rowid < nh, _xdot(la_row, ge), _xdot(la_row, le))
        tot_row = jnp.sum(la_row, axis=1, keepdims=True)
        return la_col, dt_row, cum_row, tot_row

    def update_state(rows, d, h_ref, dt_row, cum_row, tot_row):
        bm_t = xc_ref[4, rows, :].T
        w_row = jnp.exp2(tot_row - cum_row) * dt_row
        dec = jnp.broadcast_to(jnp.exp2(tot_row), (nhd, CHUNK))
        shp = (SSD_STATE, LANES)
        keep = lax.broadcasted_iota(jnp.int32, shp, 1) < SSD_HEAD_DIM
        for pr in range(npair):
            g = (2 * pr) // hpg
            xs_p = xc_ref[pr, rows, :].astype(BF16)
            bg = bm_t[g * SSD_STATE:(g + 1) * SSD_STATE, :]
            hd0 = d * nh + 2 * pr
            s0 = _bdot((bg * w_row[hd0:hd0 + 1, :]).astype(BF16), xs_p)
            s1 = _bdot((bg * w_row[hd0 + 1:hd0 + 2, :]).astype(BF16), xs_p)
            dec_p = jnp.where(keep, jnp.broadcast_to(dec[hd0:hd0 + 1, :], shp),
                              jnp.broadcast_to(dec[hd0 + 1:hd0 + 2, :], shp))
            h_ref[pr] = dec_p * h_ref[pr] + jnp.where(keep, s0, s1)

    @pl.when(p == 0)
    def _():
        _conv_to_cache(xbc_ref, xbcp_ref, xbcn_ref, xpad_ref, xc_ref, cw_ref, cb_ref, tt, nt, _silu)

        def body(i, c):
            ci = NCH - 1 - i
            stash_ref[tt * NCH + ci] = hbk_ref[...]
            _, dt_row, cum_row, tot_row = row_gates(ci)
            update_state(_seq_rows(tt, ci), 1, hbk_ref, dt_row, cum_row, tot_row)
            return c
        lax.fori_loop(0, NCH, body, 0, unroll=SSD_UNROLL)

    @pl.when(p == 1)
    def _():
        def body(ci, c):
            gc = tt * NCH + ci
            rows = _seq_rows(tt, ci)
            orow = _chunk_rows(ci)
            la_col, dt_row, cum_row, tot_row = row_gates(ci)
            cum_col = jnp.where(colid < nh, _xdot(le, la_col), _xdot(ge, la_col))
            bm_t = xc_ref[4, rows, :].T
            cm = xc_ref[5, rows, :]
            gmats, y_inter = [], []
            for g in range(SSD_GROUPS):
                cm_g = cm[:, g * SSD_STATE:(g + 1) * SSD_STATE].astype(BF16)
                gmats.append(_bdot(cm_g, bm_t[g * SSD_STATE:(g + 1) * SSD_STATE, :].astype(BF16)))
                prs = range(g * hpg // 2, (g + 1) * hpg // 2)
                h_cat = jnp.concatenate([hf_ref[pr] for pr in prs] + [stash_ref[gc, pr] for pr in prs], axis=1)
                y_inter.append(_bdot(cm_g, h_cat.astype(BF16)))
            y_parts = []
            for pr in range(npair):
                g = (2 * pr) // hpg
                xs_f = xc_ref[pr, rows, :]
                xs_p = xs_f.astype(BF16)
                ys, ef, eb = [], [], []
                for hl in range(2):
                    hf = 2 * pr + hl
                    hb = nh + hf
                    ci_f = jnp.broadcast_to(cum_col[:, hf:hf + 1], (CHUNK, CHUNK))
                    ci_b = jnp.broadcast_to(cum_col[:, hb:hb + 1], (CHUNK, CHUNK))
                    dec_f = jnp.where(lower, jnp.exp2(ci_f - cum_row[hf:hf + 1, :]), 0.0)
                    dec_b = jnp.where(upper, jnp.exp2(ci_b - cum_row[hb:hb + 1, :]), 0.0)
                    mmat = gmats[g] * (dec_f * dt_row[hf:hf + 1, :] + dec_b * dt_row[hb:hb + 1, :])
                    ys.append(_bdot(mmat.astype(BF16), xs_p))
                    ef.append(jnp.exp2(ci_f))
                    eb.append(jnp.exp2(ci_b))
                k = pr - g * hpg // 2
                yi = y_inter[g]
                y = (jnp.where(left, ys[0], ys[1])
                     + yi[:, k * LANES:(k + 1) * LANES] * jnp.where(left, ef[0], ef[1])
                     + yi[:, (hpg // 2 + k) * LANES:(hpg // 2 + k + 1) * LANES] * jnp.where(left, eb[0], eb[1])
                     + dexp_ref[:, pr * LANES:(pr + 1) * LANES] * xs_f)
                y_parts.append(y)
            y = jnp.concatenate(y_parts, axis=1)
            y = y * _silu(z_ref[orow, :])
            y = y * lax.rsqrt(jnp.mean(y * y, axis=-1, keepdims=True) + NORM_EPS) * nw_ref[...]
            o_ref[orow, :] = y.astype(o_ref.dtype)
            update_state(rows, 0, hf_ref, dt_row, cum_row, tot_row)
            return c
        lax.fori_loop(0, NCH, body, 0, unroll=SSD_UNROLL)


def _ssd_call(nb, seq, xbc, dt, z, cw, cb, dtb, alog, dexp, nw):
    nt = seq // SEQ_TILE
    ns = SSD_CONV_W // LANES
    prev, nxt = _slab_halo_specs_p0(ns, nt, nb * seq)
    in_specs = [_slab_spec_p0(ns, nt), prev, nxt, _both_spec(CHUNK, nt), _fwd_spec(SSD_W, nt)] + \
               [_const_spec(a.shape) for a in (cw, cb, dtb, alog, dexp, nw)]
    state = (SSD_HEADS // 2, SSD_STATE, LANES)
    scratch = [pltpu.VMEM((ns, SEQ_TILE + 2 * SUBLANES, LANES), F32), pltpu.VMEM((ns, seq, LANES), F32),
               pltpu.VMEM(state, F32), pltpu.VMEM(state, F32), pltpu.VMEM((seq // CHUNK,) + state, F32)]
    return _mixer_call(_ssd_kernel, "ssd", nb, seq, (xbc, xbc, xbc, dt, z, cw, cb, dtb, alog, dexp, nw),
                       in_specs, scratch)


def _ret_kernel(rq_ref, rk_ref, rv_ref, rg_ref, cos_ref, sin_ref, dmat_ref, gin_ref, gst_ref, g128_ref, nw_ref,
                o_ref, q_ref, kt_ref, h_ref, hb_ref, *, nt):
    p = pl.program_id(1)
    t = pl.program_id(2)
    tt = _tile_pos(p, t, nt)
    nh = RET_HEADS
    half = RET_QK // 2
    hw = nh * half

    @pl.when(t == 0)
    def _():
        h_ref[...] = jnp.zeros_like(h_ref)

    def rope(x):
        a, b = x[:, :hw], x[:, hw:]
        c, s = cos_ref[...], sin_ref[...]
        return jnp.concatenate([a * c - b * s, b * c + a * s], axis=1)

    def head_rows(kt, h):
        return jnp.concatenate([kt[h * half:(h + 1) * half, :], kt[hw + h * half:hw + (h + 1) * half, :]], axis=0)

    def widen(hc, h):
        n = hc.shape[1]
        sizes = [h * half, hw - half, hw - (h + 1) * half]
        z0, z1, z2 = [jnp.zeros((r, n), hc.dtype) for r in sizes]
        parts = [z0, hc[:half, :], z1, hc[half:, :], z2]
        return jnp.concatenate([q for q in parts if q.shape[0] > 0], axis=0)

    def update_state(gc, vrow, d):
        v = rv_ref[vrow, :]
        kt = kt_ref[gc]
        for h in range(nh):
            ks = (head_rows(kt, h) * gst_ref[d, h]).astype(BF16)
            s = _bdot(ks, v[:, h * RET_V:(h + 1) * RET_V].astype(BF16))
            h_ref[d, h] = g128_ref[h] * h_ref[d, h] + s

    @pl.when(p == 0)
    def _():
        k = rope(rk_ref[...])
        for c in range(NCH):
            kt_ref[tt * NCH + c] = k[c * CHUNK:(c + 1) * CHUNK, :].T

        def body(i, c):
            ci = NCH - 1 - i
            gc = tt * NCH + ci
            hb_ref[gc] = h_ref[1]
            update_state(gc, _chunk_rows(ci), 1)
            return c
        lax.fori_loop(0, NCH, body, 0, unroll=RET_UNROLL)

    @pl.when(p == 1)
    def _():
        q_ref[...] = rope(rq_ref[...]) * (RET_QK ** -0.5)
        lane = lax.broadcasted_iota(jnp.int32, (CHUNK, 2 * hw), 1) % hw

        def body(ci, c):
            gc = tt * NCH + ci
            rows = _chunk_rows(ci)
            q = q_ref[rows, :]
            v = rv_ref[rows, :]
            kt = kt_ref[gc].astype(BF16)
            outs = []
            for h in range(nh):
                qh = jnp.where((lane >= h * half) & (lane < (h + 1) * half), q, 0.0).astype(BF16)
                s = _bdot(qh, kt)
                o = _bdot((s * dmat_ref[h]).astype(BF16), v[:, h * RET_V:(h + 1) * RET_V].astype(BF16))
                h_cat = jnp.concatenate([widen(h_ref[0, h], h), widen(hb_ref[gc, h], h)], axis=1).astype(BF16)
                oi = _bdot(qh, h_cat)
                o = o + oi[:, :RET_V] * gin_ref[0, h] + oi[:, RET_V:] * gin_ref[1, h]
                outs.append(_head_norm(o))
            y = jnp.concatenate(outs, axis=1) * nw_ref[...]
            o_ref[rows, :] = (_silu(rg_ref[rows, :]) * y).astype(o_ref.dtype)
            update_state(gc, rows, 0)
            return c
        lax.fori_loop(0, NCH, body, 0, unroll=RET_UNROLL)


def _ret_call(nb, seq, rq, rk, rv, rg, cos, sin, dmat, gin, gst, g128, nw):
    nt = seq // SEQ_TILE
    w = RET_HEADS * RET_QK
    tab_both = pl.BlockSpec((SEQ_TILE, w // 2), lambda b, p, t: (_tile_pos(p, t, nt), 0))
    p0_spec = pl.BlockSpec((SEQ_TILE, w), lambda b, p, t: (b * nt + _p0_tile(p, t, nt), 0))
    in_specs = [_fwd_spec(w, nt), p0_spec, _both_spec(512, nt), _fwd_spec(512, nt), tab_both, tab_both] + \
               [_const_spec(a.shape) for a in (dmat, gin, gst, g128, nw)]
    scratch = [pltpu.VMEM((SEQ_TILE, w), F32), pltpu.VMEM((seq // CHUNK, w, CHUNK), F32),
               pltpu.VMEM((2, RET_HEADS, RET_QK, RET_V), F32),
               pltpu.VMEM((seq // CHUNK, RET_HEADS, RET_QK, RET_V), F32)]
    return _mixer_call(_ret_kernel, "ret", nb, seq, (rq, rk, rv, rg, cos, sin, dmat, gin, gst, g128, nw),
                       in_specs, scratch)


def _mlstm_kernel(qk_ref, qkp_ref, qkn_ref, v_ref, gi_ref, gf_ref, mo_ref, cw_ref, cb_ref, gbi_ref, gbf_ref, nwt_ref,
                  o_ref, xpad_ref, qk_s, c_ref, m_ref, c_stash, m_stash, *, nt):
    p = pl.program_id(1)
    t = pl.program_id(2)
    tt = _tile_pos(p, t, nt)
    nh = ML_HEADS
    nhd = 2 * nh

    @pl.when(t == 0)
    def _():
        c_ref[...] = jnp.zeros_like(c_ref)
        m_ref[...] = jnp.zeros_like(m_ref)

    valid_b, valid_f = _tri_masks()
    le = valid_b.astype(F32)
    ge = valid_f.astype(F32)
    rowid8 = lax.broadcasted_iota(jnp.int32, (nhd, CHUNK), 0)
    colid = lax.broadcasted_iota(jnp.int32, (CHUNK, CHUNK), 1)
    rowid = lax.broadcasted_iota(jnp.int32, (CHUNK, CHUNK), 0)
    top = rowid < ML_QK
    ones_blk = jnp.ones((ML_NROWS, CHUNK), F32)
    srows = ML_V + ML_NROWS

    def row_gates(ci):
        rows = _chunk_rows(ci)
        gi_col = gi_ref[rows, :] + gbi_ref[...]
        li_row = gi_col.T[0:nhd, :]
        lf_row = -_softplus(-(gf_ref[rows, :] + gbf_ref[...]).T[0:nhd, :])
        b_row = jnp.where(rowid8 < nh, _xdot(lf_row, ge), _xdot(lf_row, le))
        tot = jnp.sum(lf_row, axis=1, keepdims=True)
        return gi_col, li_row, b_row, tot

    def update_state(srow, vrow, d, li_row, b_row, tot):
        w_row = tot - b_row + li_row
        m_loc = jnp.max(w_row, axis=1, keepdims=True)
        e_row = jnp.exp(w_row - m_loc)
        m_prev = m_ref[...]
        m_new = jnp.maximum(tot + m_prev, m_loc)
        dec = jnp.exp(tot + m_prev - m_new)
        inj = jnp.exp(m_loc - m_new)
        v = v_ref[vrow, :]
        for pr in range(nh // 2):
            k_pair = (qk_s[nh // 2 + pr, srow, :] * (ML_QK ** -0.5)).astype(BF16)
            for hl in range(2):
                h = 2 * pr + hl
                hd = d * nh + h
                e_b = jnp.broadcast_to(e_row[hd:hd + 1, :], (srows, CHUNK))
                vt = v[:, h * ML_V:(h + 1) * ML_V].T
                lhs = (jnp.concatenate([vt, ones_blk], axis=0) * e_b).astype(BF16)
                s = _bdot(lhs, k_pair)
                c_ref[hd] = (jnp.broadcast_to(dec[hd:hd + 1, :], (srows, CHUNK)) * c_ref[hd]
                             + jnp.broadcast_to(inj[hd:hd + 1, :], (srows, CHUNK)) * s)
        keep = (rowid8 >= nh) if d == 0 else (rowid8 < nh)
        m_ref[...] = jnp.where(keep, m_prev, m_new)

    def direction_out(d, qts, k_pairs, vts, r_col, b_row, c_aug, m_prev):
        mask = valid_f if d == 0 else valid_b
        outs = []
        for h in range(nh):
            hd = d * nh + h
            rj = jnp.where(mask, r_col[:, hd:hd + 1], -jnp.inf)
            u = jnp.maximum(m_prev[hd:hd + 1, :], jnp.max(rj, axis=0, keepdims=True))
            st = _bdot(k_pairs[h // 2], qts[h])
            pt = (st * jnp.exp(rj - u)).astype(BF16)
            lhs = jnp.concatenate([vts[h], ones_blk], axis=0).astype(BF16)
            nd = _bdot(lhs, pt) + jnp.exp(m_prev[hd:hd + 1, :] - u) * _bdot(c_aug[h].astype(BF16), qts[h])
            floor = jnp.exp(-(b_row[hd:hd + 1, :] + u))
            den = jnp.maximum(jnp.abs(nd[ML_V:ML_V + 1, :]), floor)
            outs.append(nd[0:ML_V, :] / den)
        return outs

    @pl.when(p == 0)
    def _():
        _conv_to_cache(qk_ref, qkp_ref, qkn_ref, xpad_ref, qk_s, cw_ref, cb_ref, tt, nt, _silu)

        def body(i, c):
            ci = NCH - 1 - i
            gc = tt * NCH + ci
            c_stash[gc] = c_ref[nh:nhd]
            m_stash[gc] = m_ref[...]
            _, li_row, b_row, tot = row_gates(ci)
            update_state(_seq_rows(tt, ci), _chunk_rows(ci), 1, li_row, b_row, tot)
            return c
        lax.fori_loop(0, NCH, body, 0, unroll=ML_UNROLL)

    @pl.when(p == 1)
    def _():
        def body(ci, c):
            gc = tt * NCH + ci
            srow = _seq_rows(tt, ci)
            rows = _chunk_rows(ci)
            gi_col, li_row, b_row, tot = row_gates(ci)
            r_col = gi_col - _rows_to_cols(b_row)
            qts, k_pairs, vts = [], [], []
            v = v_ref[rows, :]
            for pr in range(nh // 2):
                qt = qk_s[pr, srow, :].T
                qts.append(jnp.where(top, qt, 0.0).astype(BF16))
                qts.append(jnp.where(top, 0.0, qt).astype(BF16))
                k_pairs.append((qk_s[nh // 2 + pr, srow, :] * (ML_QK ** -0.5)).astype(BF16))
            for h in range(nh):
                vts.append(v[:, h * ML_V:(h + 1) * ML_V].T)
            hf = direction_out(0, qts, k_pairs, vts, r_col, b_row, c_ref[0:nh], m_ref[...])
            hb = direction_out(1, qts, k_pairs, vts, r_col, b_row, c_stash[gc], m_stash[gc])
            ys = []
            for h in range(nh):
                x = hf[h] + hb[h]
                mu = jnp.mean(x, axis=0, keepdims=True)
                xc = x - mu
                var = jnp.mean(xc * xc, axis=0, keepdims=True)
                ys.append((xc * lax.rsqrt(var + NORM_EPS) * nwt_ref[h]).T)
            y = jnp.concatenate(ys, axis=1)
            o_ref[rows, :] = (_sigmoid(mo_ref[rows, :]) * y).astype(o_ref.dtype)
            update_state(srow, rows, 0, li_row, b_row, tot)
            return c
        lax.fori_loop(0, NCH, body, 0, unroll=ML_UNROLL)


def _mlstm_call(nb, seq, mqk, mv, mgi, mgf, mo, cw, cb, gbi, gbf, nwt):
    nt = seq // SEQ_TILE
    ns = 2 * ML_HEADS * ML_QK // LANES
    prev, nxt = _slab_halo_specs_p0(ns, nt, nb * seq)
    in_specs = [_slab_spec_p0(ns, nt), prev, nxt, _both_spec(512, nt), _both_spec(CHUNK, nt), _both_spec(CHUNK, nt),
                _fwd_spec(512, nt)] + [_const_spec(a.shape) for a in (cw, cb, gbi, gbf, nwt)]
    scratch = [pltpu.VMEM((ns, SEQ_TILE + 2 * SUBLANES, LANES), F32), pltpu.VMEM((ns, seq, LANES), F32),
               pltpu.VMEM((2 * ML_HEADS, ML_V + ML_NROWS, CHUNK), F32),
               pltpu.VMEM((2 * ML_HEADS, CHUNK), F32),
               pltpu.VMEM((seq // CHUNK, ML_HEADS, ML_V + ML_NROWS, CHUNK), F32),
               pltpu.VMEM((seq // CHUNK, 2 * ML_HEADS, CHUNK), F32)]
    return _mixer_call(_mlstm_kernel, "mlstm", nb, seq, (mqk, mqk, mqk, mv, mgi, mgf, mo, cw, cb, gbi, gbf, nwt),
                       in_specs, scratch)


def _pad_cols(a, width):
    return jnp.pad(a, [(0, 0)] * (a.ndim - 1) + [(0, width - a.shape[-1])])


def _prepare(w_in, lru_conv_w, lru_conv_b, lru_gate_w, lru_gate_b, lru_lambda, ssd_conv_w, ssd_conv_b, ssd_dt_bias,
             ssd_a_log, ssd_d, ssd_norm_w, ret_norm_w, mlstm_conv_w, mlstm_conv_b, mlstm_gate_b, mlstm_norm_w,
             w_branch, w_out, w_ffn_in, w_ffn_out, ln_g, ln_b):
    sp = {}
    off = 0
    for name, w in (("xa", 512), ("ga", 512), ("z", 512), ("xbc", 768), ("dt", 16), ("rq", 256), ("rk", 256),
                    ("rv", 512), ("rg", 512), ("mqk", 512), ("mv", 512), ("mo", 512), ("mg", 16), ("mix", 4096)):
        sp[name] = w_in[:, :, off:off + w]
        off += w
    mg = sp["mg"].reshape(DEPTH, D_MODEL, 2, 2, ML_HEADS)

    def halves_major(w):
        w = w.reshape(DEPTH, D_MODEL, RET_HEADS, 2, RET_QK // 2)
        return jnp.moveaxis(w, 3, 2).reshape(DEPTH, D_MODEL, RET_HEADS * RET_QK)

    cols = {"dt": _pad_cols(sp["dt"], CHUNK), "rq": halves_major(sp["rq"]), "rk": halves_major(sp["rk"]),
            "mgi": _pad_cols(mg[:, :, :, 0, :].reshape(DEPTH, D_MODEL, 8), CHUNK),
            "mgf": _pad_cols(mg[:, :, :, 1, :].reshape(DEPTH, D_MODEL, 8), CHUNK)}
    wp = jnp.concatenate([cols[n] if n in cols else sp[n] for n, _ in PROJ_OUTS], axis=-1).astype(BF16)
    gbi = _pad_cols(mlstm_gate_b[:, :, 0, :].reshape(DEPTH, 1, 8), CHUNK)
    gbf = _pad_cols(mlstm_gate_b[:, :, 1, :].reshape(DEPTH, 1, 8), CHUNK)
    ml_nwt = jnp.broadcast_to(mlstm_norm_w.reshape(DEPTH, ML_HEADS, ML_V, 1), (DEPTH, ML_HEADS, ML_V, CHUNK))

    eye = jnp.eye(LRU_BLOCKS, dtype=F32)
    dense = jnp.einsum("ldghij,hk->ldghikj", lru_gate_w, eye).reshape(DEPTH, 2, 2, LRU_W, LRU_W)
    lru_wg = jnp.concatenate([dense[:, :, 0], dense[:, :, 1]], axis=-1)

    return dict(
        wp=wp, wmix=sp["mix"].astype(BF16),
        ffn_i=w_ffn_in.astype(BF16), ffn_d=w_ffn_out.astype(BF16),
        ln_g=ln_g[:, :, None, :], ln_b=ln_b[:, :, None, :],
        wb=w_branch.astype(BF16), wo=w_out.astype(BF16),
        lru_cw=lru_conv_w, lru_cb=lru_conv_b[:, None, :], lru_wg=(0.5 * lru_wg).astype(BF16),
        lru_gb=0.5 * lru_gate_b.reshape(DEPTH, 2, 1, 2 * LRU_W), lru_lam=lru_lambda[:, :, None, :],
        ssd_cw=ssd_conv_w, ssd_cb=ssd_conv_b[:, None, :],
        ssd_dtb=_pad_cols(ssd_dt_bias.reshape(DEPTH, 1, 16), CHUNK),
        ssd_alog=_pad_cols(ssd_a_log.reshape(DEPTH, 1, 16), CHUNK),
        ssd_dexp=jnp.repeat(ssd_d, SSD_HEAD_DIM, axis=-1)[:, None, :], ssd_nw=ssd_norm_w[:, None, :],
        ret_nw=ret_norm_w[:, None, :],
        ml_cw=mlstm_conv_w, ml_cb=mlstm_conv_b[:, None, :], ml_gbi=gbi, ml_gbf=gbf, ml_nwt=ml_nwt,
    )


def _ret_tables(seq):
    d = RET_QK
    inv = ROPE_BASE ** (-jnp.arange(0, d, 2, dtype=F32) / d)
    ang = jnp.arange(seq, dtype=F32)[:, None] * inv[None, :]
    cos = jnp.tile(jnp.cos(ang), (1, RET_HEADS))
    sin = jnp.tile(jnp.sin(ang), (1, RET_HEADS))
    lg = jnp.log1p(-jnp.exp2(-5.0 - jnp.arange(RET_HEADS, dtype=F32)))[:, None, None]
    pos = jnp.arange(CHUNK, dtype=F32)
    dmat = jnp.exp(lg * jnp.abs(pos[:, None] - pos[None, :])[None])
    ones = jnp.ones((1, 1, RET_V), F32)
    gin = jnp.stack([jnp.exp(lg * (pos + 1.0)[None, :, None]) * ones,
                     jnp.exp(lg * (CHUNK - pos)[None, :, None]) * ones])
    ones = jnp.ones((1, RET_QK, 1), F32)
    gst = jnp.stack([jnp.exp(lg * (CHUNK - 1.0 - pos)[None, None, :]) * ones,
                     jnp.exp(lg * pos[None, None, :]) * ones])
    g128 = jnp.exp(lg * float(CHUNK)) * jnp.ones((1, RET_QK, RET_V), F32)
    return cos, sin, dmat, gin, gst, g128


def _trunk(x3, pw):
    nb, seq, _ = x3.shape
    x = x3.reshape(nb * seq, D_MODEL)
    tabs = _ret_tables(seq)
    for l in range(DEPTH):
        x = _ffn_call(x, pw["ffn_i"][l, 0], pw["ffn_d"][l, 0], pw["ln_g"][l, 0], pw["ln_b"][l, 0])
        pr = dict(zip([n for n, _ in PROJ_OUTS], _proj_call(x, pw["wp"][l])))
        ya = _lru_call(nb, seq, pr["xa"], pr["ga"], pw["lru_cw"][l], pw["lru_cb"][l], pw["lru_wg"][l],
                       pw["lru_gb"][l], pw["lru_lam"][l])
        yb = _ssd_call(nb, seq, pr["xbc"], pr["dt"], pr["z"], pw["ssd_cw"][l], pw["ssd_cb"][l], pw["ssd_dtb"][l],
                       pw["ssd_alog"][l], pw["ssd_dexp"][l], pw["ssd_nw"][l])
        yc = _ret_call(nb, seq, pr["rq"], pr["rk"], pr["rv"], pr["rg"], *tabs, pw["ret_nw"][l])
        yd = _mlstm_call(nb, seq, pr["mqk"], pr["mv"], pr["mgi"], pr["mgf"], pr["mo"], pw["ml_cw"][l], pw["ml_cb"][l],
                         pw["ml_gbi"][l], pw["ml_gbf"][l], pw["ml_nwt"][l])
        x = _merge_call(x, (ya, yb, yc, yd), pw["wmix"][l], pw["wb"][l], pw["wo"][l], pw["ln_g"][l, 1],
                        pw["ln_b"][l, 1])
        x = _ffn_call(x, pw["ffn_i"][l, 1], pw["ffn_d"][l, 1], pw["ln_g"][l, 2], pw["ln_b"][l, 2])
    return x.reshape(nb, seq, D_MODEL)


def kernel(x_prompt, x_sample, w_in, lru_conv_w, lru_conv_b, lru_gate_w, lru_gate_b, lru_lambda, ssd_conv_w,
           ssd_conv_b, ssd_dt_bias, ssd_a_log, ssd_d, ssd_norm_w, ret_norm_w, mlstm_conv_w, mlstm_conv_b,
           mlstm_gate_b, mlstm_norm_w, w_branch, w_out, w_ffn_in, w_ffn_out, ln_g, ln_b):
    pw = _prepare(w_in, lru_conv_w, lru_conv_b, lru_gate_w, lru_gate_b, lru_lambda, ssd_conv_w, ssd_conv_b,
                  ssd_dt_bias, ssd_a_log, ssd_d, ssd_norm_w, ret_norm_w, mlstm_conv_w, mlstm_conv_b, mlstm_gate_b,
                  mlstm_norm_w, w_branch, w_out, w_ffn_in, w_ffn_out, ln_g, ln_b)
    return _trunk(x_prompt, pw), _trunk(x_sample, pw)
```

```python
import functools

import jax
import jax.numpy as jnp
from jax import lax
from jax.experimental import pallas as pl
from jax.experimental.pallas import tpu as pltpu

F32 = jnp.float32
BF16 = jnp.bfloat16
HIGHEST = lax.Precision.HIGHEST

D_MODEL = 1024
DEPTH = 4
D_FF = 2816
CHUNK = 128
LRU_W = 512
LRU_BLOCKS = 8
LRU_C = 8.0
SSD_HEADS = 8
SSD_HEAD_DIM = 64
SSD_W = 512
SSD_STATE = 64
SSD_GROUPS = 2
SSD_CONV_W = SSD_W + 2 * SSD_GROUPS * SSD_STATE
RET_HEADS = 4
RET_QK = 64
RET_V = 128
ROPE_BASE = 10000.0
ML_HEADS = 4
ML_QK = 64
ML_V = 128
DN_ALPHA = (2.0 * DEPTH) ** 0.25
NORM_EPS = 1e-5
LOG2E = 1.4426950408889634
TINY = 1.1754944e-38

VMEM_LIMIT_BYTES = 56 * 1024 * 1024
FF_CHUNK = 256
TOKEN_TILE = 512
SEQ_TILE = 512
NCH = SEQ_TILE // CHUNK
SUBLANES = 8
LANES = 128
LRU_UNROLL = NCH
SSD_UNROLL = NCH
RET_UNROLL = NCH
ML_UNROLL = NCH
ML_NROWS = 16
MIXER_GROUPS = (("lru", "ret"), ("ssd", "mlstm"))

PROJ_OUTS = (("xa", 512), ("ga", 512), ("z", 512), ("xbc", 768), ("dt", 128), ("rq", 256), ("rk", 256),
             ("rv", 512), ("rg", 512), ("mqk", 512), ("mv", 512), ("mo", 512), ("mgi", 128), ("mgf", 128))
SLAB_OUTS = ("xa", "ga", "xbc", "mqk")


def _cparams(sem):
    return pltpu.CompilerParams(dimension_semantics=sem, vmem_limit_bytes=VMEM_LIMIT_BYTES)


def _const_spec(shape):
    nd = len(shape)
    return pl.BlockSpec(shape, lambda *_: (0,) * nd, pipeline_mode=pl.Buffered(1))


def _bdot(a, b):
    return jnp.dot(a, b, preferred_element_type=F32)


def _xdot(a, b):
    return jnp.dot(a, b, preferred_element_type=F32, precision=HIGHEST)


def _sigmoid(x):
    return 0.5 * jnp.tanh(0.5 * x) + 0.5


def _silu(x):
    return x * _sigmoid(x)


def _softplus(x):
    return jnp.maximum(x, 0.0) + jnp.log1p(jnp.exp(-jnp.abs(x)))


def _layer_norm(y, g, b):
    mu = jnp.mean(y, axis=-1, keepdims=True)
    yc = y - mu
    var = jnp.mean(yc * yc, axis=-1, keepdims=True)
    return yc * lax.rsqrt(var + NORM_EPS) * g + b


def _ffn_kernel(x_ref, wi_ref, wd_ref, g_ref, b_ref, o_ref, acc_ref):
    x = x_ref[...]
    xb = x.astype(BF16)
    nf = D_FF // FF_CHUNK
    for f in range(nf):
        gate = _bdot(xb, wi_ref[:, f * FF_CHUNK:(f + 1) * FF_CHUNK])
        up = _bdot(xb, wi_ref[:, D_FF + f * FF_CHUNK:D_FF + (f + 1) * FF_CHUNK])
        h = (_silu(gate) * up).astype(BF16)
        part = _bdot(h, wd_ref[f * FF_CHUNK:(f + 1) * FF_CHUNK, :])
        if f == 0:
            acc_ref[...] = part
        else:
            acc_ref[...] += part
    y = DN_ALPHA * x + 0.5 * acc_ref[...]
    o_ref[...] = _layer_norm(y, g_ref[...], b_ref[...])


def _ffn_call(x, wi, wd, g, b):
    m = x.shape[0]
    tm = TOKEN_TILE
    return pl.pallas_call(
        _ffn_kernel,
        out_shape=jax.ShapeDtypeStruct((m, D_MODEL), F32),
        grid=(m // tm,),
        in_specs=[pl.BlockSpec((tm, D_MODEL), lambda i: (i, 0)),
                  _const_spec(wi.shape), _const_spec(wd.shape), _const_spec(g.shape), _const_spec(b.shape)],
        out_specs=pl.BlockSpec((tm, D_MODEL), lambda i: (i, 0)),
        scratch_shapes=[pltpu.VMEM((tm, D_MODEL), F32)],
        compiler_params=_cparams(("parallel",)),
        name="ffn",
    )(x, wi, wd, g, b)


def _proj_kernel(x_ref, w_ref, *o_refs):
    xb = x_ref[...].astype(BF16)
    off = 0
    for o_ref in o_refs:
        if len(o_ref.shape) == 3:
            w = o_ref.shape[0] * LANES
            res = _bdot(xb, w_ref[:, off:off + w])
            for k in range(o_ref.shape[0]):
                o_ref[k] = res[:, k * LANES:(k + 1) * LANES]
            off += w
        else:
            w = o_ref.shape[1]
            o_ref[...] = _bdot(xb, w_ref[:, off:off + w])
            off += w


def _proj_call(x, wp):
    m = x.shape[0]
    tm = TOKEN_TILE
    shapes, specs = [], []
    for name, w in PROJ_OUTS:
        if name in SLAB_OUTS:
            shapes.append(jax.ShapeDtypeStruct((w // LANES, m, LANES), F32))
            specs.append(pl.BlockSpec((w // LANES, tm, LANES), lambda i: (0, i, 0)))
        else:
            shapes.append(jax.ShapeDtypeStruct((m, w), F32))
            specs.append(pl.BlockSpec((tm, w), lambda i: (i, 0)))
    return pl.pallas_call(
        _proj_kernel,
        out_shape=shapes,
        grid=(m // tm,),
        in_specs=[pl.BlockSpec((tm, D_MODEL), lambda i: (i, 0)), _const_spec(wp.shape)],
        out_specs=specs,
        compiler_params=_cparams(("parallel",)),
        name="proj",
    )(x, wp)


def _merge_kernel(x_ref, ya_ref, yb_ref, yc_ref, yd_ref, wg_ref, wb_ref, wo_ref, g_ref, b_ref, o_ref):
    x = x_ref[...]
    xb = x.astype(BF16)
    merged = None
    for i, y_ref in enumerate((ya_ref, yb_ref, yc_ref, yd_ref)):
        gate = _sigmoid(_bdot(xb, wg_ref[:, i * D_MODEL:(i + 1) * D_MODEL]))
        term = gate * _bdot(y_ref[...], wb_ref[i])
        merged = term if merged is None else merged + term
    out = _bdot(merged.astype(BF16), wo_ref[...])
    o_ref[...] = _layer_norm(DN_ALPHA * x + out, g_ref[...], b_ref[...])


def _merge_call(x, ys, wg, wb, wo, g, b):
    m = x.shape[0]
    tm = TOKEN_TILE
    return pl.pallas_call(
        _merge_kernel,
        out_shape=jax.ShapeDtypeStruct((m, D_MODEL), F32),
        grid=(m // tm,),
        in_specs=[pl.BlockSpec((tm, D_MODEL), lambda i: (i, 0))]
                 + [pl.BlockSpec((tm, 512), lambda i: (i, 0)) for _ in range(4)]
                 + [_const_spec(wg.shape), _const_spec(wb.shape), _const_spec(wo.shape),
                    _const_spec(g.shape), _const_spec(b.shape)],
        out_specs=pl.BlockSpec((tm, D_MODEL), lambda i: (i, 0)),
        compiler_params=_cparams(("parallel",)),
        name="merge",
    )(x, *ys, wg, wb, wo, g, b)


def _tile_pos(p, t, nt):
    return t + (1 - p) * (nt - 1 - 2 * t)


def _p0_tile(p, t, nt):
    return (1 - p) * (nt - 1 - t)


def _both_spec(w, nt):
    return pl.BlockSpec((SEQ_TILE, w), lambda b, p, t: (b * nt + _tile_pos(p, t, nt), 0))


def _fwd_spec(w, nt):
    return pl.BlockSpec((SEQ_TILE, w), lambda b, p, t: (b * nt + p * t, 0))


def _slab_spec_p0(ns, nt):
    return pl.BlockSpec((ns, SEQ_TILE, LANES), lambda b, p, t: (0, b * nt + _p0_tile(p, t, nt), 0))


def _slab_spec_p1(ns, nt):
    return pl.BlockSpec((ns, SEQ_TILE, LANES), lambda b, p, t: (0, b * nt + p * t, 0))


def _slab_halo_specs_p0(ns, nt, n_rows):
    per = SEQ_TILE // SUBLANES
    last = n_rows // SUBLANES - 1
    prev = pl.BlockSpec((ns, SUBLANES, LANES),
                        lambda b, p, t: (0, jnp.maximum((b * nt + _p0_tile(p, t, nt)) * per - 1, 0), 0))
    nxt = pl.BlockSpec((ns, SUBLANES, LANES),
                       lambda b, p, t: (0, jnp.minimum((b * nt + _p0_tile(p, t, nt) + 1) * per, last), 0))
    return prev, nxt


def _conv_to_cache(x_ref, prev_ref, next_ref, xpad_ref, cache_ref, w_ref, b_ref, tt, nt, act):
    ns = x_ref.shape[0]
    for k in range(ns):
        xpad_ref[k, 0:SUBLANES, :] = jnp.where(tt == 0, 0.0, prev_ref[k])
        xpad_ref[k, SUBLANES:SUBLANES + SEQ_TILE, :] = x_ref[k]
        xpad_ref[k, SUBLANES + SEQ_TILE:2 * SUBLANES + SEQ_TILE, :] = jnp.where(tt == nt - 1, 0.0, next_ref[k])
    base = tt * SEQ_TILE
    for k in range(ns):
        w = w_ref[:, k * LANES:(k + 1) * LANES]
        bias = b_ref[:, k * LANES:(k + 1) * LANES]
        for r in range(0, SEQ_TILE, CHUNK):
            taps = [xpad_ref[k, SUBLANES - 2 + j + r:SUBLANES - 2 + j + r + CHUNK, :] for j in range(4)]
            y = w[0:1, :] * taps[0] + w[1:2, :] * taps[1] + w[2:3, :] * taps[2] + w[3:4, :] * taps[3] + bias
            cache_ref[k, pl.ds(pl.multiple_of(base + r, CHUNK), CHUNK), :] = act(y)


def _mixers_kernel(*refs, nt, layout):
    p = pl.program_id(1)
    t = pl.program_id(2)
    tt = _tile_pos(p, t, nt)
    n_in = sum(ni for _, ni, _ in layout)
    n_out = len(layout)
    parts, i_in, i_sc = [], 0, n_in + n_out
    for k, (body, ni, ns) in enumerate(layout):
        parts.append(body(*refs[i_in:i_in + ni], refs[n_in + k], *refs[i_sc:i_sc + ns], p=p, t=t, tt=tt, nt=nt))
        i_in += ni
        i_sc += ns

    @pl.when(t == 0)
    def _():
        for init, _, _ in parts:
            init()

    @pl.when(p == 0)
    def _():
        for _, phase0, _ in parts:
            phase0()

    @pl.when(p == 1)
    def _():
        for _, _, phase1 in parts:
            phase1()


def _mixers_call(nb, seq, parts):
    nt = seq // SEQ_TILE
    layout = tuple((body, len(ops), len(scr)) for _, body, ops, _, scr in parts)
    out_spec = pl.BlockSpec((SEQ_TILE, 512), lambda b, p, t: (b * nt + p * t, 0))
    return pl.pallas_call(
        functools.partial(_mixers_kernel, nt=nt, layout=layout),
        out_shape=[jax.ShapeDtypeStruct((nb * seq, 512), BF16) for _ in parts],
        grid=(nb, 2, nt),
        in_specs=[sp for _, _, _, specs, _ in parts for sp in specs],
        out_specs=[out_spec for _ in parts],
        scratch_shapes=[sc for _, _, _, _, scr in parts for sc in scr],
        compiler_params=_cparams(("arbitrary", "arbitrary", "arbitrary")),
        name="_".join(name for name, _, _, _, _ in parts),
    )(*[op for _, _, ops, _, _ in parts for op in ops])


def _tri_masks():
    ii = lax.broadcasted_iota(jnp.int32, (CHUNK, CHUNK), 0)
    jj = lax.broadcasted_iota(jnp.int32, (CHUNK, CHUNK), 1)
    return jj <= ii, jj >= ii


def _rows_to_cols(x):
    r = x.shape[0]
    return jnp.concatenate([x, jnp.zeros((CHUNK - r, x.shape[1]), x.dtype)], axis=0).T


def _chunk_rows(ci):
    return pl.ds(pl.multiple_of(ci * CHUNK, CHUNK), CHUNK)


def _seq_rows(tt, ci):
    return pl.ds(pl.multiple_of(tt * SEQ_TILE + ci * CHUNK, CHUNK), CHUNK)


def _head_norm(x):
    mu = jnp.mean(x, axis=-1, keepdims=True)
    xc = x - mu
    var = jnp.mean(xc * xc, axis=-1, keepdims=True)
    return xc * lax.rsqrt(var + NORM_EPS)


def _scan_rows(a, b, reverse):
    n = a.shape[0]
    row = lax.broadcasted_iota(jnp.int32, a.shape, 0)
    s = 1
    while s < n:
        if reverse:
            a_sh = pltpu.roll(a, n - s, 0)
            b_sh = pltpu.roll(b, n - s, 0)
            valid = row < n - s
        else:
            a_sh = pltpu.roll(a, s, 0)
            b_sh = pltpu.roll(b, s, 0)
            valid = row >= s
        b = a * jnp.where(valid, b_sh, 0.0) + b
        a = a * jnp.where(valid, a_sh, 1.0)
        s *= 2
    return a, b


LRU_GROUPS = CHUNK // SUBLANES


def _load_interleaved(ref, row0):
    ns = ref.shape[0]
    return jnp.concatenate(
        [jnp.concatenate([ref[k, pl.ds(row0 + s, LRU_GROUPS, stride=SUBLANES), :] for k in range(ns)], axis=1)
         for s in range(SUBLANES)], axis=0)


def _lru_body(xa_ref, xap_ref, xan_ref, ga_ref, cw_ref, cb_ref, wg_ref, gb_ref, lam_ref, o_ref,
              xpad_ref, xc_ref, carry_ref, hb_ref, y_ref, *, p, t, tt, nt):
    ng = LRU_GROUPS

    def init():
        carry_ref[...] = jnp.zeros_like(carry_ref)

    def direction(ci, d):
        xc = _load_interleaved(xc_ref, pl.multiple_of(tt * SEQ_TILE + ci * CHUNK, CHUNK))
        th = jnp.tanh(_bdot(xc.astype(BF16), wg_ref[d]) + gb_ref[d])
        c_half = (-0.5 * LRU_C) * _softplus(-lam_ref[d])
        log_a = c_half * th[:, :LRU_W] + c_half
        a = jnp.exp(log_a)
        v = -jnp.tanh(log_a) * (a * a + 1.0)
        ix = (th[:, LRU_W:] + 1.0) * (0.5 * xc)
        b = (v * lax.rsqrt(jnp.maximum(v, TINY))) * ix
        sa = [a[s * ng:(s + 1) * ng, :] for s in range(SUBLANES)]
        sb = [b[s * ng:(s + 1) * ng, :] for s in range(SUBLANES)]
        order = range(SUBLANES - 2, -1, -1) if d == 1 else range(1, SUBLANES)
        for s in order:
            q = s + 1 if d == 1 else s - 1
            sb[s] = sa[s] * sb[q] + sb[s]
            sa[s] = sa[s] * sa[q]
        edge = 0 if d == 1 else SUBLANES - 1
        ga, gb = _scan_rows(sa[edge], sb[edge], reverse=(d == 1))
        row = lax.broadcasted_iota(jnp.int32, ga.shape, 0)
        if d == 1:
            ga = jnp.where(row == ng - 1, 1.0, pltpu.roll(ga, ng - 1, 0))
            gb = jnp.where(row == ng - 1, 0.0, pltpu.roll(gb, ng - 1, 0))
        else:
            ga = jnp.where(row == 0, 1.0, pltpu.roll(ga, 1, 0))
            gb = jnp.where(row == 0, 0.0, pltpu.roll(gb, 1, 0))
        h_in = ga * carry_ref[...] + gb
        hs = [sa[s] * h_in + sb[s] for s in range(SUBLANES)]
        carry_ref[...] = hs[0][0:1, :] if d == 1 else hs[SUBLANES - 1][ng - 1:ng, :]
        return jnp.concatenate(hs, axis=0)

    def phase0():
        _conv_to_cache(xa_ref, xap_ref, xan_ref, xpad_ref, xc_ref, cw_ref, cb_ref, tt, nt, lambda v: v)

        def body(i, c):
            ci = NCH - 1 - i
            hb_ref[tt * NCH + ci] = direction(ci, 1)
            return c
        lax.fori_loop(0, NCH, body, 0, unroll=LRU_UNROLL)

    def phase1():
        def body(ci, c):
            h = direction(ci, 0) + hb_ref[tt * NCH + ci]
            y = jax.nn.gelu(_load_interleaved(ga_ref, pl.multiple_of(ci * CHUNK, CHUNK))) * h
            for s in range(SUBLANES):
                for k in range(LRU_W // LANES):
                    y_ref[k, pl.ds(s, ng, stride=SUBLANES), :] = y[s * ng:(s + 1) * ng, k * LANES:(k + 1) * LANES]
            o_ref[_chunk_rows(ci), :] = jnp.concatenate(
                [y_ref[k] for k in range(LRU_W // LANES)], axis=1).astype(o_ref.dtype)
            return c
        lax.fori_loop(0, NCH, body, 0, unroll=LRU_UNROLL)

    return init, phase0, phase1


def _lru_part(nb, seq, xa, ga, cw, cb, wg, gb, lam):
    nt = seq // SEQ_TILE
    ns = LRU_W // LANES
    prev, nxt = _slab_halo_specs_p0(ns, nt, nb * seq)
    in_specs = [_slab_spec_p0(ns, nt), prev, nxt, _slab_spec_p1(ns, nt),
                _const_spec(cw.shape), _const_spec(cb.shape), _const_spec(wg.shape),
                _const_spec(gb.shape), _const_spec(lam.shape)]
    scratch = [pltpu.VMEM((ns, SEQ_TILE + 2 * SUBLANES, LANES), F32), pltpu.VMEM((ns, seq, LANES), F32),
               pltpu.VMEM((1, LRU_W), F32), pltpu.VMEM((seq // CHUNK, CHUNK, LRU_W), F32),
               pltpu.VMEM((ns, CHUNK, LANES), F32)]
    return "lru", _lru_body, (xa, xa, xa, ga, cw, cb, wg, gb, lam), in_specs, scratch


def _ssd_body(xbc_ref, xbcp_ref, xbcn_ref, dt_ref, z_ref, cw_ref, cb_ref, dtb_ref, alog_ref, dexp_ref, nw_ref,
              o_ref, xpad_ref, xc_ref, hf_ref, hbk_ref, stash_ref, *, p, t, tt, nt):
    nh = SSD_HEADS
    nhd = 2 * nh
    npair = nh // 2
    hpg = nh // SSD_GROUPS

    def init():
        hf_ref[...] = jnp.zeros_like(hf_ref)
        hbk_ref[...] = jnp.zeros_like(hbk_ref)

    lower, upper = _tri_masks()
    le = lower.astype(F32)
    ge = upper.astype(F32)
    lane = lax.broadcasted_iota(jnp.int32, (1, CHUNK), 1)
    neg_a = jnp.where(lane < nhd, -LOG2E * jnp.exp(alog_ref[...]), 0.0)
    rowid = lax.broadcasted_iota(jnp.int32, (nhd, CHUNK), 0)
    colid = lax.broadcasted_iota(jnp.int32, (CHUNK, CHUNK), 1)
    left = colid < SSD_HEAD_DIM

    def row_gates(ci):
        dt_col = _softplus(dt_ref[_chunk_rows(ci), :] + dtb_ref[...])
        la_col = dt_col * neg_a
        dt_row = dt_col.T[0:nhd, :]
        la_row = la_col.T[0:nhd, :]
        cum_row = jnp.where(rowid < nh, _xdot(la_row, ge), _xdot(la_row, le))
        tot_row = jnp.sum(la_row, axis=1, keepdims=True)
        return la_col, dt_row, cum_row, tot_row

    def update_state(rows, d, h_ref, dt_row, cum_row, tot_row):
        bm_t = xc_ref[4, rows, :].T
        w_row = jnp.exp2(tot_row - cum_row) * dt_row
        dec = jnp.broadcast_to(jnp.exp2(tot_row), (nhd, CHUNK))
        shp = (SSD_STATE, LANES)
        keep = lax.broadcasted_iota(jnp.int32, shp, 1) < SSD_HEAD_DIM
        for pr in range(npair):
            g = (2 * pr) // hpg
            xs_p = xc_ref[pr, rows, :].astype(BF16)
            bg = bm_t[g * SSD_STATE:(g + 1) * SSD_STATE, :]
            hd0 = d * nh + 2 * pr
            s0 = _bdot((bg * w_row[hd0:hd0 + 1, :]).astype(BF16), xs_p)
            s1 = _bdot((bg * w_row[hd0 + 1:hd0 + 2, :]).astype(BF16), xs_p)
            dec_p = jnp.where(keep, jnp.broadcast_to(dec[hd0:hd0 + 1, :], shp),
                              jnp.broadcast_to(dec[hd0 + 1:hd0 + 2, :], shp))
            h_ref[pr] = dec_p * h_ref[pr] + jnp.where(keep, s0, s1)

    def phase0():
        _conv_to_cache(xbc_ref, xbcp_ref, xbcn_ref, xpad_ref, xc_ref, cw_ref, cb_ref, tt, nt, _silu)

        def body(i, c):
            ci = NCH - 1 - i
            stash_ref[tt * NCH + ci] = hbk_ref[...]
            _, dt_row, cum_row, tot_row = row_gates(ci)
            update_state(_seq_rows(tt, ci), 1, hbk_ref, dt_row, cum_row, tot_row)
            return c
        lax.fori_loop(0, NCH, body, 0, unroll=SSD_UNROLL)

    def phase1():
        def body(ci, c):
            gc = tt * NCH + ci
            rows = _seq_rows(tt, ci)
            orow = _chunk_rows(ci)
            la_col, dt_row, cum_row, tot_row = row_gates(ci)
            cum_col = jnp.where(colid < nh, _xdot(le, la_col), _xdot(ge, la_col))
            bm_t = xc_ref[4, rows, :].T
            cm = xc_ref[5, rows, :]
            gmats, y_inter = [], []
            for g in range(SSD_GROUPS):
                cm_g = cm[:, g * SSD_STATE:(g + 1) * SSD_STATE].astype(BF16)
                gmats.append(_bdot(cm_g, bm_t[g * SSD_STATE:(g + 1) * SSD_STATE, :].astype(BF16)))
                prs = range(g * hpg // 2, (g + 1) * hpg // 2)
                h_cat = jnp.concatenate([hf_ref[pr] for pr in prs] + [stash_ref[gc, pr] for pr in prs], axis=1)
                y_inter.append(_bdot(cm_g, h_cat.astype(BF16)))
            y_parts = []
            for pr in range(npair):
                g = (2 * pr) // hpg
                xs_f = xc_ref[pr, rows, :]
                xs_p = xs_f.astype(BF16)
                ys, ef, eb = [], [], []
                for hl in range(2):
                    hf = 2 * pr + hl
                    hb = nh + hf
                    ci_f = jnp.broadcast_to(cum_col[:, hf:hf + 1], (CHUNK, CHUNK))
                    ci_b = jnp.broadcast_to(cum_col[:, hb:hb + 1], (CHUNK, CHUNK))
                    dec_f = jnp.where(lower, jnp.exp2(ci_f - cum_row[hf:hf + 1, :]), 0.0)
                    dec_b = jnp.where(upper, jnp.exp2(ci_b - cum_row[hb:hb + 1, :]), 0.0)
                    mmat = gmats[g] * (dec_f * dt_row[hf:hf + 1, :] + dec_b * dt_row[hb:hb + 1, :])
                    ys.append(_bdot(mmat.astype(BF16), xs_p))
                    ef.append(jnp.exp2(ci_f))
                    eb.append(jnp.exp2(ci_b))
                k = pr - g * hpg // 2
                yi = y_inter[g]
                y = (jnp.where(left, ys[0], ys[1])
                     + yi[:, k * LANES:(k + 1) * LANES] * jnp.where(left, ef[0], ef[1])
                     + yi[:, (hpg // 2 + k) * LANES:(hpg // 2 + k + 1) * LANES] * jnp.where(left, eb[0], eb[1])
                     + dexp_ref[:, pr * LANES:(pr + 1) * LANES] * xs_f)
                y_parts.append(y)
            y = jnp.concatenate(y_parts, axis=1)
            y = y * _silu(z_ref[orow, :])
            y = y * lax.rsqrt(jnp.mean(y * y, axis=-1, keepdims=True) + NORM_EPS) * nw_ref[...]
            o_ref[orow, :] = y.astype(o_ref.dtype)
            update_state(rows, 0, hf_ref, dt_row, cum_row, tot_row)
            return c
        lax.fori_loop(0, NCH, body, 0, unroll=SSD_UNROLL)

    return init, phase0, phase1


def _ssd_part(nb, seq, xbc, dt, z, cw, cb, dtb, alog, dexp, nw):
    nt = seq // SEQ_TILE
    ns = SSD_CONV_W // LANES
    prev, nxt = _slab_halo_specs_p0(ns, nt, nb * seq)
    in_specs = [_slab_spec_p0(ns, nt), prev, nxt, _both_spec(CHUNK, nt), _fwd_spec(SSD_W, nt)] + \
               [_const_spec(a.shape) for a in (cw, cb, dtb, alog, dexp, nw)]
    state = (SSD_HEADS // 2, SSD_STATE, LANES)
    scratch = [pltpu.VMEM((ns, SEQ_TILE + 2 * SUBLANES, LANES), F32), pltpu.VMEM((ns, seq, LANES), F32),
               pltpu.VMEM(state, F32), pltpu.VMEM(state, F32), pltpu.VMEM((seq // CHUNK,) + state, F32)]
    return "ssd", _ssd_body, (xbc, xbc, xbc, dt, z, cw, cb, dtb, alog, dexp, nw), in_specs, scratch


def _ret_body(rq_ref, rk_ref, rv_ref, rg_ref, cos_ref, sin_ref, dmat_ref, gin_ref, gst_ref, g128_ref, nw_ref,
              o_ref, q_ref, kt_ref, h_ref, hb_ref, *, p, t, tt, nt):
    nh = RET_HEADS
    half = RET_QK // 2
    hw = nh * half

    def init():
        h_ref[...] = jnp.zeros_like(h_ref)

    def rope(x):
        a, b = x[:, :hw], x[:, hw:]
        c, s = cos_ref[...], sin_ref[...]
        return jnp.concatenate([a * c - b * s, b * c + a * s], axis=1)

    def head_rows(kt, h):
        return jnp.concatenate([kt[h * half:(h + 1) * half, :], kt[hw + h * half:hw + (h + 1) * half, :]], axis=0)

    def widen(hc, h):
        n = hc.shape[1]
        sizes = [h * half, hw - half, hw - (h + 1) * half]
        z0, z1, z2 = [jnp.zeros((r, n), hc.dtype) for r in sizes]
        parts = [z0, hc[:half, :], z1, hc[half:, :], z2]
        return jnp.concatenate([q for q in parts if q.shape[0] > 0], axis=0)

    def update_state(gc, vrow, d):
        v = rv_ref[vrow, :]
        kt = kt_ref[gc]
        for h in range(nh):
            ks = (head_rows(kt, h) * gst_ref[d, h]).astype(BF16)
            s = _bdot(ks, v[:, h * RET_V:(h + 1) * RET_V].astype(BF16))
            h_ref[d, h] = g128_ref[h] * h_ref[d, h] + s

    def phase0():
        k = rope(rk_ref[...])
        for c in range(NCH):
            kt_ref[tt * NCH + c] = k[c * CHUNK:(c + 1) * CHUNK, :].T

        def body(i, c):
            ci = NCH - 1 - i
            gc = tt * NCH + ci
            hb_ref[gc] = h_ref[1]
            update_state(gc, _chunk_rows(ci), 1)
            return c
        lax.fori_loop(0, NCH, body, 0, unroll=RET_UNROLL)

    def phase1():
        q_ref[...] = rope(rq_ref[...]) * (RET_QK ** -0.5)
        lane = lax.broadcasted_iota(jnp.int32, (CHUNK, 2 * hw), 1) % hw

        def body(ci, c):
            gc = tt * NCH + ci
            rows = _chunk_rows(ci)
            q = q_ref[rows, :]
            v = rv_ref[rows, :]
            kt = kt_ref[gc].astype(BF16)
            outs = []
            for h in range(nh):
                qh = jnp.where((lane >= h * half) & (lane < (h + 1) * half), q, 0.0).astype(BF16)
                s = _bdot(qh, kt)
                o = _bdot((s * dmat_ref[h]).astype(BF16), v[:, h * RET_V:(h + 1) * RET_V].astype(BF16))
                h_cat = jnp.concatenate([widen(h_ref[0, h], h), widen(hb_ref[gc, h], h)], axis=1).astype(BF16)
                oi = _bdot(qh, h_cat)
                o = o + oi[:, :RET_V] * gin_ref[0, h] + oi[:, RET_V:] * gin_ref[1, h]
                outs.append(_head_norm(o))
            y = jnp.concatenate(outs, axis=1) * nw_ref[...]
            o_ref[rows, :] = (_silu(rg_ref[rows, :]) * y).astype(o_ref.dtype)
            update_state(gc, rows, 0)
            return c
        lax.fori_loop(0, NCH, body, 0, unroll=RET_UNROLL)

    return init, phase0, phase1


def _ret_part(nb, seq, rq, rk, rv, rg, cos, sin, dmat, gin, gst, g128, nw):
    nt = seq // SEQ_TILE
    w = RET_HEADS * RET_QK
    tab_both = pl.BlockSpec((SEQ_TILE, w // 2), lambda b, p, t: (_tile_pos(p, t, nt), 0))
    p0_spec = pl.BlockSpec((SEQ_TILE, w), lambda b, p, t: (b * nt + _p0_tile(p, t, nt), 0))
    in_specs = [_fwd_spec(w, nt), p0_spec, _both_spec(512, nt), _fwd_spec(512, nt), tab_both, tab_both] + \
               [_const_spec(a.shape) for a in (dmat, gin, gst, g128, nw)]
    scratch = [pltpu.VMEM((SEQ_TILE, w), F32), pltpu.VMEM((seq // CHUNK, w, CHUNK), F32),
               pltpu.VMEM((2, RET_HEADS, RET_QK, RET_V), F32),
               pltpu.VMEM((seq // CHUNK, RET_HEADS, RET_QK, RET_V), F32)]
    return "ret", _ret_body, (rq, rk, rv, rg, cos, sin, dmat, gin, gst, g128, nw), in_specs, scratch


def _mlstm_body(qk_ref, qkp_ref, qkn_ref, v_ref, gi_ref, gf_ref, mo_ref, cw_ref, cb_ref, gbi_ref, gbf_ref, nwt_ref,
                o_ref, xpad_ref, qk_s, c_ref, m_ref, c_stash, m_stash, *, p, t, tt, nt):
    nh = ML_HEADS
    nhd = 2 * nh

    def init():
        c_ref[...] = jnp.zeros_like(c_ref)
        m_ref[...] = jnp.zeros_like(m_ref)

    valid_b, valid_f = _tri_masks()
    le = valid_b.astype(F32)
    ge = valid_f.astype(F32)
    rowid8 = lax.broadcasted_iota(jnp.int32, (nhd, CHUNK), 0)
    colid = lax.broadcasted_iota(jnp.int32, (CHUNK, CHUNK), 1)
    rowid = lax.broadcasted_iota(jnp.int32, (CHUNK, CHUNK), 0)
    top = rowid < ML_QK
    ones_blk = jnp.ones((ML_NROWS, CHUNK), F32)
    srows = ML_V + ML_NROWS

    def row_gates(ci):
        rows = _chunk_rows(ci)
        gi_col = gi_ref[rows, :] + gbi_ref[...]
        li_row = gi_col.T[0:nhd, :]
        lf_row = -_softplus(-(gf_ref[rows, :] + gbf_ref[...]).T[0:nhd, :])
        b_row = jnp.where(rowid8 < nh, _xdot(lf_row, ge), _xdot(lf_row, le))
        tot = jnp.sum(lf_row, axis=1, keepdims=True)
        return gi_col, li_row, b_row, tot

    def update_state(srow, vrow, d, li_row, b_row, tot):
        w_row = tot - b_row + li_row
        m_loc = jnp.max(w_row, axis=1, keepdims=True)
        e_row = jnp.exp(w_row - m_loc)
        m_prev = m_ref[...]
        m_new = jnp.maximum(tot + m_prev, m_loc)
        dec = jnp.exp(tot + m_prev - m_new)
        inj = jnp.exp(m_loc - m_new)
        v = v_ref[vrow, :]
        for pr in range(nh // 2):
            k_pair = (qk_s[nh // 2 + pr, srow, :] * (ML_QK ** -0.5)).astype(BF16)
            for hl in range(2):
                h = 2 * pr + hl
                hd = d * nh + h
                e_b = jnp.broadcast_to(e_row[hd:hd + 1, :], (srows, CHUNK))
                vt = v[:, h * ML_V:(h + 1) * ML_V].T
                lhs = (jnp.concatenate([vt, ones_blk], axis=0) * e_b).astype(BF16)
                s = _bdot(lhs, k_pair)
                c_ref[hd] = (jnp.broadcast_to(dec[hd:hd + 1, :], (srows, CHUNK)) * c_ref[hd]
                             + jnp.broadcast_to(inj[hd:hd + 1, :], (srows, CHUNK)) * s)
        keep = (rowid8 >= nh) if d == 0 else (rowid8 < nh)
        m_ref[...] = jnp.where(keep, m_prev, m_new)

    def direction_out(d, qts, k_pairs, vts, r_col, b_row, c_aug, m_prev):
        mask = valid_f if d == 0 else valid_b
        outs = []
        for h in range(nh):
            hd = d * nh + h
            rj = jnp.where(mask, r_col[:, hd:hd + 1], -jnp.inf)
            u = jnp.maximum(m_prev[hd:hd + 1, :], jnp.max(rj, axis=0, keepdims=True))
            st = _bdot(k_pairs[h // 2], qts[h])
            pt = (st * jnp.exp(rj - u)).astype(BF16)
            lhs = jnp.concatenate([vts[h], ones_blk], axis=0).astype(BF16)
            nd = _bdot(lhs, pt) + jnp.exp(m_prev[hd:hd + 1, :] - u) * _bdot(c_aug[h].astype(BF16), qts[h])
            floor = jnp.exp(-(b_row[hd:hd + 1, :] + u))
            den = jnp.maximum(jnp.abs(nd[ML_V:ML_V + 1, :]), floor)
            outs.append(nd[0:ML_V, :] / den)
        return outs

    def phase0():
        _conv_to_cache(qk_ref, qkp_ref, qkn_ref, xpad_ref, qk_s, cw_ref, cb_ref, tt, nt, _silu)

        def body(i, c):
            ci = NCH - 1 - i
            gc = tt * NCH + ci
            c_stash[gc] = c_ref[nh:nhd]
            m_stash[gc] = m_ref[...]
            _, li_row, b_row, tot = row_gates(ci)
            update_state(_seq_rows(tt, ci), _chunk_rows(ci), 1, li_row, b_row, tot)
            return c
        lax.fori_loop(0, NCH, body, 0, unroll=ML_UNROLL)

    def phase1():
        def body(ci, c):
            gc = tt * NCH + ci
            srow = _seq_rows(tt, ci)
            rows = _chunk_rows(ci)
            gi_col, li_row, b_row, tot = row_gates(ci)
            r_col = gi_col - _rows_to_cols(b_row)
            qts, k_pairs, vts = [], [], []
            v = v_ref[rows, :]
            for pr in range(nh // 2):
                qt = qk_s[pr, srow, :].T
                qts.append(jnp.where(top, qt, 0.0).astype(BF16))
                qts.append(jnp.where(top, 0.0, qt).astype(BF16))
                k_pairs.append((qk_s[nh // 2 + pr, srow, :] * (ML_QK ** -0.5)).astype(BF16))
            for h in range(nh):
                vts.append(v[:, h * ML_V:(h + 1) * ML_V].T)
            hf = direction_out(0, qts, k_pairs, vts, r_col, b_row, c_ref[0:nh], m_ref[...])
            hb = direction_out(1, qts, k_pairs, vts, r_col, b_row, c_stash[gc], m_stash[gc])
            ys = []
            for h in range(nh):
                x = hf[h] + hb[h]
                mu = jnp.mean(x, axis=0, keepdims=True)
                xc = x - mu
                var = jnp.mean(xc * xc, axis=0, keepdims=True)
                ys.append((xc * lax.rsqrt(var + NORM_EPS) * nwt_ref[h]).T)
            y = jnp.concatenate(ys, axis=1)
            o_ref[rows, :] = (_sigmoid(mo_ref[rows, :]) * y).astype(o_ref.dtype)
            update_state(srow, rows, 0, li_row, b_row, tot)
            return c
        lax.fori_loop(0, NCH, body, 0, unroll=ML_UNROLL)

    return init, phase0, phase1


def _mlstm_part(nb, seq, mqk, mv, mgi, mgf, mo, cw, cb, gbi, gbf, nwt):
    nt = seq // SEQ_TILE
    ns = 2 * ML_HEADS * ML_QK // LANES
    prev, nxt = _slab_halo_specs_p0(ns, nt, nb * seq)
    in_specs = [_slab_spec_p0(ns, nt), prev, nxt, _both_spec(512, nt), _both_spec(CHUNK, nt), _both_spec(CHUNK, nt),
                _fwd_spec(512, nt)] + [_const_spec(a.shape) for a in (cw, cb, gbi, gbf, nwt)]
    scratch = [pltpu.VMEM((ns, SEQ_TILE + 2 * SUBLANES, LANES), F32), pltpu.VMEM((ns, seq, LANES), F32),
               pltpu.VMEM((2 * ML_HEADS, ML_V + ML_NROWS, CHUNK), F32),
               pltpu.VMEM((2 * ML_HEADS, CHUNK), F32),
               pltpu.VMEM((seq // CHUNK, ML_HEADS, ML_V + ML_NROWS, CHUNK), F32),
               pltpu.VMEM((seq // CHUNK, 2 * ML_HEADS, CHUNK), F32)]
    return "mlstm", _mlstm_body, (mqk, mqk, mqk, mv, mgi, mgf, mo, cw, cb, gbi, gbf, nwt), in_specs, scratch


def _pad_cols(a, width):
    return jnp.pad(a, [(0, 0)] * (a.ndim - 1) + [(0, width - a.shape[-1])])


def _prepare(w_in, lru_conv_w, lru_conv_b, lru_gate_w, lru_gate_b, lru_lambda, ssd_conv_w, ssd_conv_b, ssd_dt_bias,
             ssd_a_log, ssd_d, ssd_norm_w, ret_norm_w, mlstm_conv_w, mlstm_conv_b, mlstm_gate_b, mlstm_norm_w,
             w_branch, w_out, w_ffn_in, w_ffn_out, ln_g, ln_b):
    sp = {}
    off = 0
    for name, w in (("xa", 512), ("ga", 512), ("z", 512), ("xbc", 768), ("dt", 16), ("rq", 256), ("rk", 256),
                    ("rv", 512), ("rg", 512), ("mqk", 512), ("mv", 512), ("mo", 512), ("mg", 16), ("mix", 4096)):
        sp[name] = w_in[:, :, off:off + w]
        off += w
    mg = sp["mg"].reshape(DEPTH, D_MODEL, 2, 2, ML_HEADS)

    def halves_major(w):
        w = w.reshape(DEPTH, D_MODEL, RET_HEADS, 2, RET_QK // 2)
        return jnp.moveaxis(w, 3, 2).reshape(DEPTH, D_MODEL, RET_HEADS * RET_QK)

    cols = {"dt": _pad_cols(sp["dt"], CHUNK), "rq": halves_major(sp["rq"]), "rk": halves_major(sp["rk"]),
            "mgi": _pad_cols(mg[:, :, :, 0, :].reshape(DEPTH, D_MODEL, 8), CHUNK),
            "mgf": _pad_cols(mg[:, :, :, 1, :].reshape(DEPTH, D_MODEL, 8), CHUNK)}
    wp = jnp.concatenate([cols[n] if n in cols else sp[n] for n, _ in PROJ_OUTS], axis=-1).astype(BF16)
    gbi = _pad_cols(mlstm_gate_b[:, :, 0, :].reshape(DEPTH, 1, 8), CHUNK)
    gbf = _pad_cols(mlstm_gate_b[:, :, 1, :].reshape(DEPTH, 1, 8), CHUNK)
    ml_nwt = jnp.broadcast_to(mlstm_norm_w.reshape(DEPTH, ML_HEADS, ML_V, 1), (DEPTH, ML_HEADS, ML_V, CHUNK))

    eye = jnp.eye(LRU_BLOCKS, dtype=F32)
    dense = jnp.einsum("ldghij,hk->ldghikj", lru_gate_w, eye).reshape(DEPTH, 2, 2, LRU_W, LRU_W)
    lru_wg = jnp.concatenate([dense[:, :, 0], dense[:, :, 1]], axis=-1)

    return dict(
        wp=wp, wmix=sp["mix"].astype(BF16),
        ffn_i=w_ffn_in.astype(BF16), ffn_d=w_ffn_out.astype(BF16),
        ln_g=ln_g[:, :, None, :], ln_b=ln_b[:, :, None, :],
        wb=w_branch.astype(BF16), wo=w_out.astype(BF16),
        lru_cw=lru_conv_w, lru_cb=lru_conv_b[:, None, :], lru_wg=(0.5 * lru_wg).astype(BF16),
        lru_gb=0.5 * lru_gate_b.reshape(DEPTH, 2, 1, 2 * LRU_W), lru_lam=lru_lambda[:, :, None, :],
        ssd_cw=ssd_conv_w, ssd_cb=ssd_conv_b[:, None, :],
        ssd_dtb=_pad_cols(ssd_dt_bias.reshape(DEPTH, 1, 16), CHUNK),
        ssd_alog=_pad_cols(ssd_a_log.reshape(DEPTH, 1, 16), CHUNK),
        ssd_dexp=jnp.repeat(ssd_d, SSD_HEAD_DIM, axis=-1)[:, None, :], ssd_nw=ssd_norm_w[:, None, :],
        ret_nw=ret_norm_w[:, None, :],
        ml_cw=mlstm_conv_w, ml_cb=mlstm_conv_b[:, None, :], ml_gbi=gbi, ml_gbf=gbf, ml_nwt=ml_nwt,
    )


def _ret_tables(seq):
    d = RET_QK
    inv = ROPE_BASE ** (-jnp.arange(0, d, 2, dtype=F32) / d)
    ang = jnp.arange(seq, dtype=F32)[:, None] * inv[None, :]
    cos = jnp.tile(jnp.cos(ang), (1, RET_HEADS))
    sin = jnp.tile(jnp.sin(ang), (1, RET_HEADS))
    lg = jnp.log1p(-jnp.exp2(-5.0 - jnp.arange(RET_HEADS, dtype=F32)))[:, None, None]
    pos = jnp.arange(CHUNK, dtype=F32)
    dmat = jnp.exp(lg * jnp.abs(pos[:, None] - pos[None, :])[None])
    ones = jnp.ones((1, 1, RET_V), F32)
    gin = jnp.stack([jnp.exp(lg * (pos + 1.0)[None, :, None]) * ones,
                     jnp.exp(lg * (CHUNK - pos)[None, :, None]) * ones])
    ones = jnp.ones((1, RET_QK, 1), F32)
    gst = jnp.stack([jnp.exp(lg * (CHUNK - 1.0 - pos)[None, None, :]) * ones,
                     jnp.exp(lg * pos[None, None, :]) * ones])
    g128 = jnp.exp(lg * float(CHUNK)) * jnp.ones((1, RET_QK, RET_V), F32)
    return cos, sin, dmat, gin, gst, g128


def _trunk(x3, pw):
    nb, seq, _ = x3.shape
    x = x3.reshape(nb * seq, D_MODEL)
    tabs = _ret_tables(seq)
    for l in range(DEPTH):
        x = _ffn_call(x, pw["ffn_i"][l, 0], pw["ffn_d"][l, 0], pw["ln_g"][l, 0], pw["ln_b"][l, 0])
        pr = dict(zip([n for n, _ in PROJ_OUTS], _proj_call(x, pw["wp"][l])))
        parts = dict(
            lru=_lru_part(nb, seq, pr["xa"], pr["ga"], pw["lru_cw"][l], pw["lru_cb"][l], pw["lru_wg"][l],
                          pw["lru_gb"][l], pw["lru_lam"][l]),
            ssd=_ssd_part(nb, seq, pr["xbc"], pr["dt"], pr["z"], pw["ssd_cw"][l], pw["ssd_cb"][l], pw["ssd_dtb"][l],
                          pw["ssd_alog"][l], pw["ssd_dexp"][l], pw["ssd_nw"][l]),
            ret=_ret_part(nb, seq, pr["rq"], pr["rk"], pr["rv"], pr["rg"], *tabs, pw["ret_nw"][l]),
            mlstm=_mlstm_part(nb, seq, pr["mqk"], pr["mv"], pr["mgi"], pr["mgf"], pr["mo"], pw["ml_cw"][l],
                              pw["ml_cb"][l], pw["ml_gbi"][l], pw["ml_gbf"][l], pw["ml_nwt"][l]))
        ys = {}
        for group in MIXER_GROUPS:
            ys.update(zip(group, _mixers_call(nb, seq, [parts[n] for n in group])))
        x = _merge_call(x, (ys["lru"], ys["ssd"], ys["ret"], ys["mlstm"]), pw["wmix"][l], pw["wb"][l], pw["wo"][l],
                        pw["ln_g"][l, 1], pw["ln_b"][l, 1])
        x = _ffn_call(x, pw["ffn_i"][l, 1], pw["ffn_d"][l, 1], pw["ln_g"][l, 2], pw["ln_b"][l, 2])
    return x.reshape(nb, seq, D_MODEL)


def kernel(x_prompt, x_sample, w_in, lru_conv_w, lru_conv_b, lru_gate_w, lru_gate_b, lru_lambda, ssd_conv_w,
           ssd_conv_b, ssd_dt_bias, ssd_a_log, ssd_d, ssd_norm_w, ret_norm_w, mlstm_conv_w, mlstm_conv_b,
           mlstm_gate_b, mlstm_norm_w, w_branch, w_out, w_ffn_in, w_ffn_out, ln_g, ln_b):
    pw = _prepare(w_in, lru_conv_w, lru_conv_b, lru_gate_w, lru_gate_b, lru_lambda, ssd_conv_w, ssd_conv_b,
                  ssd_dt_bias, ssd_a_log, ssd_d, ssd_norm_w, ret_norm_w, mlstm_conv_w, mlstm_conv_b, mlstm_gate_b,
                  mlstm_norm_w, w_branch, w_out, w_ffn_in, w_ffn_out, ln_g, ln_b)
    return _trunk(x_prompt, pw), _trunk(x_sample, pw)
```

```python
import functools

import jax
import jax.numpy as jnp
from jax import lax
from jax.experimental import pallas as pl
from jax.experimental.pallas import tpu as pltpu

F32 = jnp.float32
BF16 = jnp.bfloat16
HIGHEST = lax.Precision.HIGHEST

D_MODEL = 1024
DEPTH = 4
D_FF = 2816
CHUNK = 128
LRU_W = 512
LRU_BLOCKS = 8
LRU_C = 8.0
SSD_HEADS = 8
SSD_HEAD_DIM = 64
SSD_W = 512
SSD_STATE = 64
SSD_GROUPS = 2
SSD_CONV_W = SSD_W + 2 * SSD_GROUPS * SSD_STATE
RET_HEADS = 4
RET_QK = 64
RET_V = 128
ROPE_BASE = 10000.0
ML_HEADS = 4
ML_QK = 64
ML_V = 128
DN_ALPHA = (2.0 * DEPTH) ** 0.25
NORM_EPS = 1e-5
LOG2E = 1.4426950408889634
TINY = 1.1754944e-38

VMEM_LIMIT_BYTES = 56 * 1024 * 1024
FF_CHUNK = 256
TOKEN_TILE = 512
SEQ_TILE = 512
NCH = SEQ_TILE // CHUNK
SUBLANES = 8
LANES = 128
LRU_UNROLL = NCH
SSD_UNROLL = NCH
RET_UNROLL = NCH
ML_UNROLL = NCH
ML_NROWS = 16
MIXER_GROUPS = (("ret", "lru"), ("mlstm", "ssd"))

PROJ_OUTS = (("xa", 512), ("ga", 512), ("z", 512), ("xbc", 768), ("dt", 128), ("rq", 256), ("rk", 256),
             ("rv", 512), ("rg", 512), ("mqk", 512), ("mv", 512), ("mo", 512), ("mgi", 128), ("mgf", 128))
SLAB_OUTS = ("xa", "ga", "xbc", "mqk")


def _cparams(sem):
    return pltpu.CompilerParams(dimension_semantics=sem, vmem_limit_bytes=VMEM_LIMIT_BYTES)


def _const_spec(shape):
    nd = len(shape)
    return pl.BlockSpec(shape, lambda *_: (0,) * nd, pipeline_mode=pl.Buffered(1))


def _bdot(a, b):
    return jnp.dot(a, b, preferred_element_type=F32)


def _xdot(a, b):
    return jnp.dot(a, b, preferred_element_type=F32, precision=HIGHEST)


def _sigmoid(x):
    return 0.5 * jnp.tanh(0.5 * x) + 0.5


def _silu(x):
    return x * _sigmoid(x)


def _softplus(x):
    return jnp.maximum(x, 0.0) + jnp.log1p(jnp.exp(-jnp.abs(x)))


def _layer_norm(y, g, b):
    mu = jnp.mean(y, axis=-1, keepdims=True)
    yc = y - mu
    var = jnp.mean(yc * yc, axis=-1, keepdims=True)
    return yc * lax.rsqrt(var + NORM_EPS) * g + b


def _ffn_kernel(x_ref, wi_ref, wd_ref, g_ref, b_ref, o_ref, acc_ref):
    x = x_ref[...]
    xb = x.astype(BF16)
    nf = D_FF // FF_CHUNK
    for f in range(nf):
        gate = _bdot(xb, wi_ref[:, f * FF_CHUNK:(f + 1) * FF_CHUNK])
        up = _bdot(xb, wi_ref[:, D_FF + f * FF_CHUNK:D_FF + (f + 1) * FF_CHUNK])
        h = (_silu(gate) * up).astype(BF16)
        part = _bdot(h, wd_ref[f * FF_CHUNK:(f + 1) * FF_CHUNK, :])
        if f == 0:
            acc_ref[...] = part
        else:
            acc_ref[...] += part
    y = DN_ALPHA * x + 0.5 * acc_ref[...]
    o_ref[...] = _layer_norm(y, g_ref[...], b_ref[...])


def _ffn_call(x, wi, wd, g, b):
    m = x.shape[0]
    tm = TOKEN_TILE
    return pl.pallas_call(
        _ffn_kernel,
        out_shape=jax.ShapeDtypeStruct((m, D_MODEL), F32),
        grid=(m // tm,),
        in_specs=[pl.BlockSpec((tm, D_MODEL), lambda i: (i, 0)),
                  _const_spec(wi.shape), _const_spec(wd.shape), _const_spec(g.shape), _const_spec(b.shape)],
        out_specs=pl.BlockSpec((tm, D_MODEL), lambda i: (i, 0)),
        scratch_shapes=[pltpu.VMEM((tm, D_MODEL), F32)],
        compiler_params=_cparams(("parallel",)),
        name="ffn",
    )(x, wi, wd, g, b)


def _proj_kernel(x_ref, w_ref, *o_refs):
    xb = x_ref[...].astype(BF16)
    off = 0
    for o_ref in o_refs:
        if len(o_ref.shape) == 3:
            w = o_ref.shape[0] * LANES
            res = _bdot(xb, w_ref[:, off:off + w])
            for k in range(o_ref.shape[0]):
                o_ref[k] = res[:, k * LANES:(k + 1) * LANES]
            off += w
        else:
            w = o_ref.shape[1]
            o_ref[...] = _bdot(xb, w_ref[:, off:off + w])
            off += w


def _proj_call(x, wp):
    m = x.shape[0]
    tm = TOKEN_TILE
    shapes, specs = [], []
    for name, w in PROJ_OUTS:
        if name in SLAB_OUTS:
            shapes.append(jax.ShapeDtypeStruct((w // LANES, m, LANES), F32))
            specs.append(pl.BlockSpec((w // LANES, tm, LANES), lambda i: (0, i, 0)))
        else:
            shapes.append(jax.ShapeDtypeStruct((m, w), F32))
            specs.append(pl.BlockSpec((tm, w), lambda i: (i, 0)))
    return pl.pallas_call(
        _proj_kernel,
        out_shape=shapes,
        grid=(m // tm,),
        in_specs=[pl.BlockSpec((tm, D_MODEL), lambda i: (i, 0)), _const_spec(wp.shape)],
        out_specs=specs,
        compiler_params=_cparams(("parallel",)),
        name="proj",
    )(x, wp)


def _merge_kernel(x_ref, ya_ref, yb_ref, yc_ref, yd_ref, wg_ref, wb_ref, wo_ref, g_ref, b_ref, o_ref):
    x = x_ref[...]
    xb = x.astype(BF16)
    merged = None
    for i, y_ref in enumerate((ya_ref, yb_ref, yc_ref, yd_ref)):
        gate = _sigmoid(_bdot(xb, wg_ref[:, i * D_MODEL:(i + 1) * D_MODEL]))
        term = gate * _bdot(y_ref[...], wb_ref[i])
        merged = term if merged is None else merged + term
    out = _bdot(merged.astype(BF16), wo_ref[...])
    o_ref[...] = _layer_norm(DN_ALPHA * x + out, g_ref[...], b_ref[...])


def _merge_call(x, ys, wg, wb, wo, g, b):
    m = x.shape[0]
    tm = TOKEN_TILE
    return pl.pallas_call(
        _merge_kernel,
        out_shape=jax.ShapeDtypeStruct((m, D_MODEL), F32),
        grid=(m // tm,),
        in_specs=[pl.BlockSpec((tm, D_MODEL), lambda i: (i, 0))]
                 + [pl.BlockSpec((tm, 512), lambda i: (i, 0)) for _ in range(4)]
                 + [_const_spec(wg.shape), _const_spec(wb.shape), _const_spec(wo.shape),
                    _const_spec(g.shape), _const_spec(b.shape)],
        out_specs=pl.BlockSpec((tm, D_MODEL), lambda i: (i, 0)),
        compiler_params=_cparams(("parallel",)),
        name="merge",
    )(x, *ys, wg, wb, wo, g, b)


def _tile_pos(p, t, nt):
    return t + (1 - p) * (nt - 1 - 2 * t)


def _p0_tile(p, t, nt):
    return (1 - p) * (nt - 1 - t)


def _both_spec(w, nt):
    return pl.BlockSpec((SEQ_TILE, w), lambda b, p, t: (b * nt + _tile_pos(p, t, nt), 0))


def _fwd_spec(w, nt):
    return pl.BlockSpec((SEQ_TILE, w), lambda b, p, t: (b * nt + p * t, 0))


def _slab_spec_p0(ns, nt):
    return pl.BlockSpec((ns, SEQ_TILE, LANES), lambda b, p, t: (0, b * nt + _p0_tile(p, t, nt), 0))


def _slab_spec_p1(ns, nt):
    return pl.BlockSpec((ns, SEQ_TILE, LANES), lambda b, p, t: (0, b * nt + p * t, 0))


def _slab_halo_specs_p0(ns, nt, n_rows):
    per = SEQ_TILE // SUBLANES
    last = n_rows // SUBLANES - 1
    prev = pl.BlockSpec((ns, SUBLANES, LANES),
                        lambda b, p, t: (0, jnp.maximum((b * nt + _p0_tile(p, t, nt)) * per - 1, 0), 0))
    nxt = pl.BlockSpec((ns, SUBLANES, LANES),
                       lambda b, p, t: (0, jnp.minimum((b * nt + _p0_tile(p, t, nt) + 1) * per, last), 0))
    return prev, nxt


def _conv_to_cache(x_ref, prev_ref, next_ref, xpad_ref, cache_ref, w_ref, b_ref, tt, nt, act):
    ns = x_ref.shape[0]
    for k in range(ns):
        xpad_ref[k, 0:SUBLANES, :] = jnp.where(tt == 0, 0.0, prev_ref[k])
        xpad_ref[k, SUBLANES:SUBLANES + SEQ_TILE, :] = x_ref[k]
        xpad_ref[k, SUBLANES + SEQ_TILE:2 * SUBLANES + SEQ_TILE, :] = jnp.where(tt == nt - 1, 0.0, next_ref[k])
    base = tt * SEQ_TILE
    for k in range(ns):
        w = w_ref[:, k * LANES:(k + 1) * LANES]
        bias = b_ref[:, k * LANES:(k + 1) * LANES]
        for r in range(0, SEQ_TILE, CHUNK):
            taps = [xpad_ref[k, SUBLANES - 2 + j + r:SUBLANES - 2 + j + r + CHUNK, :] for j in range(4)]
            y = w[0:1, :] * taps[0] + w[1:2, :] * taps[1] + w[2:3, :] * taps[2] + w[3:4, :] * taps[3] + bias
            cache_ref[k, pl.ds(pl.multiple_of(base + r, CHUNK), CHUNK), :] = act(y)


def _mixers_kernel(*refs, nt, layout):
    p = pl.program_id(1)
    t = pl.program_id(2)
    tt = _tile_pos(p, t, nt)
    n_in = sum(ni for _, ni, _ in layout)
    n_out = len(layout)
    parts, i_in, i_sc = [], 0, n_in + n_out
    for k, (body, ni, ns) in enumerate(layout):
        parts.append(body(*refs[i_in:i_in + ni], refs[n_in + k], *refs[i_sc:i_sc + ns], p=p, t=t, tt=tt, nt=nt))
        i_in += ni
        i_sc += ns

    @pl.when(t == 0)
    def _():
        for init, _, _ in parts:
            init()

    @pl.when(p == 0)
    def _():
        for _, phase0, _ in parts:
            phase0()

    @pl.when(p == 1)
    def _():
        for _, _, phase1 in parts:
            phase1()


def _mixers_call(nb, seq, parts):
    nt = seq // SEQ_TILE
    layout = tuple((body, len(ops), len(scr)) for _, body, ops, _, scr in parts)
    out_spec = pl.BlockSpec((SEQ_TILE, 512), lambda b, p, t: (b * nt + p * t, 0))
    return pl.pallas_call(
        functools.partial(_mixers_kernel, nt=nt, layout=layout),
        out_shape=[jax.ShapeDtypeStruct((nb * seq, 512), BF16) for _ in parts],
        grid=(nb, 2, nt),
        in_specs=[sp for _, _, _, specs, _ in parts for sp in specs],
        out_specs=[out_spec for _ in parts],
        scratch_shapes=[sc for _, _, _, _, scr in parts for sc in scr],
        compiler_params=_cparams(("arbitrary", "arbitrary", "arbitrary")),
        name="_".join(name for name, _, _, _, _ in parts),
    )(*[op for _, _, ops, _, _ in parts for op in ops])


def _tri_masks():
    ii = lax.broadcasted_iota(jnp.int32, (CHUNK, CHUNK), 0)
    jj = lax.broadcasted_iota(jnp.int32, (CHUNK, CHUNK), 1)
    return jj <= ii, jj >= ii


def _rows_to_cols(x):
    r = x.shape[0]
    return jnp.concatenate([x, jnp.zeros((CHUNK - r, x.shape[1]), x.dtype)], axis=0).T


def _chunk_rows(ci):
    return pl.ds(pl.multiple_of(ci * CHUNK, CHUNK), CHUNK)


def _seq_rows(tt, ci):
    return pl.ds(pl.multiple_of(tt * SEQ_TILE + ci * CHUNK, CHUNK), CHUNK)


def _head_norm(x):
    mu = jnp.mean(x, axis=-1, keepdims=True)
    xc = x - mu
    var = jnp.mean(xc * xc, axis=-1, keepdims=True)
    return xc * lax.rsqrt(var + NORM_EPS)


def _scan_rows(a, b, reverse):
    n = a.shape[0]
    row = lax.broadcasted_iota(jnp.int32, a.shape, 0)
    s = 1
    while s < n:
        if reverse:
            a_sh = pltpu.roll(a, n - s, 0)
            b_sh = pltpu.roll(b, n - s, 0)
            valid = row < n - s
        else:
            a_sh = pltpu.roll(a, s, 0)
            b_sh = pltpu.roll(b, s, 0)
            valid = row >= s
        b = a * jnp.where(valid, b_sh, 0.0) + b
        a = a * jnp.where(valid, a_sh, 1.0)
        s *= 2
    return a, b


LRU_GROUPS = CHUNK // SUBLANES


def _load_interleaved(ref, row0):
    ns = ref.shape[0]
    return jnp.concatenate(
        [jnp.concatenate([ref[k, pl.ds(row0 + s, LRU_GROUPS, stride=SUBLANES), :] for k in range(ns)], axis=1)
         for s in range(SUBLANES)], axis=0)


def _lru_body(xa_ref, xap_ref, xan_ref, ga_ref, cw_ref, cb_ref, wg_ref, gb_ref, lam_ref, o_ref,
              xpad_ref, xc_ref, carry_ref, hb_ref, y_ref, *, p, t, tt, nt):
    ng = LRU_GROUPS

    def init():
        carry_ref[...] = jnp.zeros_like(carry_ref)

    def direction(ci, d):
        xc = _load_interleaved(xc_ref, pl.multiple_of(tt * SEQ_TILE + ci * CHUNK, CHUNK))
        th = jnp.tanh(_bdot(xc.astype(BF16), wg_ref[d]) + gb_ref[d])
        c_half = (-0.5 * LRU_C) * _softplus(-lam_ref[d])
        log_a = c_half * th[:, :LRU_W] + c_half
        a = jnp.exp(log_a)
        v = -jnp.tanh(log_a) * (a * a + 1.0)
        ix = (th[:, LRU_W:] + 1.0) * (0.5 * xc)
        b = (v * lax.rsqrt(jnp.maximum(v, TINY))) * ix
        sa = [a[s * ng:(s + 1) * ng, :] for s in range(SUBLANES)]
        sb = [b[s * ng:(s + 1) * ng, :] for s in range(SUBLANES)]
        order = range(SUBLANES - 2, -1, -1) if d == 1 else range(1, SUBLANES)
        for s in order:
            q = s + 1 if d == 1 else s - 1
            sb[s] = sa[s] * sb[q] + sb[s]
            sa[s] = sa[s] * sa[q]
        edge = 0 if d == 1 else SUBLANES - 1
        ga, gb = _scan_rows(sa[edge], sb[edge], reverse=(d == 1))
        row = lax.broadcasted_iota(jnp.int32, ga.shape, 0)
        if d == 1:
            ga = jnp.where(row == ng - 1, 1.0, pltpu.roll(ga, ng - 1, 0))
            gb = jnp.where(row == ng - 1, 0.0, pltpu.roll(gb, ng - 1, 0))
        else:
            ga = jnp.where(row == 0, 1.0, pltpu.roll(ga, 1, 0))
            gb = jnp.where(row == 0, 0.0, pltpu.roll(gb, 1, 0))
        h_in = ga * carry_ref[...] + gb
        hs = [sa[s] * h_in + sb[s] for s in range(SUBLANES)]
        carry_ref[...] = hs[0][0:1, :] if d == 1 else hs[SUBLANES - 1][ng - 1:ng, :]
        return jnp.concatenate(hs, axis=0)

    def phase0():
        _conv_to_cache(xa_ref, xap_ref, xan_ref, xpad_ref, xc_ref, cw_ref, cb_ref, tt, nt, lambda v: v)

        def body(i, c):
            ci = NCH - 1 - i
            hb_ref[tt * NCH + ci] = direction(ci, 1)
            return c
        lax.fori_loop(0, NCH, body, 0, unroll=LRU_UNROLL)

    def phase1():
        def body(ci, c):
            h = direction(ci, 0) + hb_ref[tt * NCH + ci]
            y = jax.nn.gelu(_load_interleaved(ga_ref, pl.multiple_of(ci * CHUNK, CHUNK))) * h
            for s in range(SUBLANES):
                for k in range(LRU_W // LANES):
                    y_ref[k, pl.ds(s, ng, stride=SUBLANES), :] = y[s * ng:(s + 1) * ng, k * LANES:(k + 1) * LANES]
            o_ref[_chunk_rows(ci), :] = jnp.concatenate(
                [y_ref[k] for k in range(LRU_W // LANES)], axis=1).astype(o_ref.dtype)
            return c
        lax.fori_loop(0, NCH, body, 0, unroll=LRU_UNROLL)

    return init, phase0, phase1


def _lru_part(nb, seq, xa, ga, cw, cb, wg, gb, lam):
    nt = seq // SEQ_TILE
    ns = LRU_W // LANES
    prev, nxt = _slab_halo_specs_p0(ns, nt, nb * seq)
    in_specs = [_slab_spec_p0(ns, nt), prev, nxt, _slab_spec_p1(ns, nt),
                _const_spec(cw.shape), _const_spec(cb.shape), _const_spec(wg.shape),
                _const_spec(gb.shape), _const_spec(lam.shape)]
    scratch = [pltpu.VMEM((ns, SEQ_TILE + 2 * SUBLANES, LANES), F32), pltpu.VMEM((ns, seq, LANES), F32),
               pltpu.VMEM((1, LRU_W), F32), pltpu.VMEM((seq // CHUNK, CHUNK, LRU_W), F32),
               pltpu.VMEM((ns, CHUNK, LANES), F32)]
    return "lru", _lru_body, (xa, xa, xa, ga, cw, cb, wg, gb, lam), in_specs, scratch


def _ssd_body(xbc_ref, xbcp_ref, xbcn_ref, dt_ref, z_ref, cw_ref, cb_ref, dtb_ref, alog_ref, dexp_ref, nw_ref,
              o_ref, xpad_ref, xc_ref, hf_ref, hbk_ref, stash_ref, *, p, t, tt, nt):
    nh = SSD_HEADS
    nhd = 2 * nh
    npair = nh // 2
    hpg = nh // SSD_GROUPS

    def init():
        hf_ref[...] = jnp.zeros_like(hf_ref)
        hbk_ref[...] = jnp.zeros_like(hbk_ref)

    lower, upper = _tri_masks()
    le = lower.astype(F32)
    ge = upper.astype(F32)
    lane = lax.broadcasted_iota(jnp.int32, (1, CHUNK), 1)
    neg_a = jnp.where(lane < nhd, -LOG2E * jnp.exp(alog_ref[...]), 0.0)
    rowid = lax.broadcasted_iota(jnp.int32, (nhd, CHUNK), 0)
    colid = lax.broadcasted_iota(jnp.int32, (CHUNK, CHUNK), 1)
    left = colid < SSD_HEAD_DIM

    def row_gates(ci):
        dt_col = _softplus(dt_ref[_chunk_rows(ci), :] + dtb_ref[...])
        la_col = dt_col * neg_a
        dt_row = dt_col.T[0:nhd, :]
        la_row = la_col.T[0:nhd, :]
        cum_row = jnp.where(rowid < nh, _xdot(la_row, ge), _xdot(la_row, le))
        tot_row = jnp.sum(la_row, axis=1, keepdims=True)
        return la_col, dt_row, cum_row, tot_row

    def update_state(rows, d, h_ref, dt_row, cum_row, tot_row):
        bm_t = xc_ref[4, rows, :].T
        w_row = jnp.exp2(tot_row - cum_row) * dt_row
        dec = jnp.broadcast_to(jnp.exp2(tot_row), (nhd, CHUNK))
        shp = (SSD_STATE, LANES)
        keep = lax.broadcasted_iota(jnp.int32, shp, 1) < SSD_HEAD_DIM
        for pr in range(npair):
            g = (2 * pr) // hpg
            xs_p = xc_ref[pr, rows, :].astype(BF16)
            bg = bm_t[g * SSD_STATE:(g + 1) * SSD_STATE, :]
            hd0 = d * nh + 2 * pr
            s0 = _bdot((bg * w_row[hd0:hd0 + 1, :]).astype(BF16), xs_p)
            s1 = _bdot((bg * w_row[hd0 + 1:hd0 + 2, :]).astype(BF16), xs_p)
            dec_p = jnp.where(keep, jnp.broadcast_to(dec[hd0:hd0 + 1, :], shp),
                              jnp.broadcast_to(dec[hd0 + 1:hd0 + 2, :], shp))
            h_ref[pr] = dec_p * h_ref[pr] + jnp.where(keep, s0, s1)

    def phase0():
        _conv_to_cache(xbc_ref, xbcp_ref, xbcn_ref, xpad_ref, xc_ref, cw_ref, cb_ref, tt, nt, _silu)

        def body(i, c):
            ci = NCH - 1 - i
            stash_ref[tt * NCH + ci] = hbk_ref[...]
            _, dt_row, cum_row, tot_row = row_gates(ci)
            update_state(_seq_rows(tt, ci), 1, hbk_ref, dt_row, cum_row, tot_row)
            return c
        lax.fori_loop(0, NCH, body, 0, unroll=SSD_UNROLL)

    def phase1():
        def body(ci, c):
            gc = tt * NCH + ci
            rows = _seq_rows(tt, ci)
            orow = _chunk_rows(ci)
            la_col, dt_row, cum_row, tot_row = row_gates(ci)
            cum_col = jnp.where(colid < nh, _xdot(le, la_col), _xdot(ge, la_col))
            bm_t = xc_ref[4, rows, :].T
            cm = xc_ref[5, rows, :]
            gmats, y_inter = [], []
            for g in range(SSD_GROUPS):
                cm_g = cm[:, g * SSD_STATE:(g + 1) * SSD_STATE].astype(BF16)
                gmats.append(_bdot(cm_g, bm_t[g * SSD_STATE:(g + 1) * SSD_STATE, :].astype(BF16)))
                prs = range(g * hpg // 2, (g + 1) * hpg // 2)
                h_cat = jnp.concatenate([hf_ref[pr] for pr in prs] + [stash_ref[gc, pr] for pr in prs], axis=1)
                y_inter.append(_bdot(cm_g, h_cat.astype(BF16)))
            y_parts = []
            for pr in range(npair):
                g = (2 * pr) // hpg
                xs_f = xc_ref[pr, rows, :]
                xs_p = xs_f.astype(BF16)
                ys, ef, eb = [], [], []
                for hl in range(2):
                    hf = 2 * pr + hl
                    hb = nh + hf
                    ci_f = jnp.broadcast_to(cum_col[:, hf:hf + 1], (CHUNK, CHUNK))
                    ci_b = jnp.broadcast_to(cum_col[:, hb:hb + 1], (CHUNK, CHUNK))
                    dec_f = jnp.where(lower, jnp.exp2(ci_f - cum_row[hf:hf + 1, :]), 0.0)
                    dec_b = jnp.where(upper, jnp.exp2(ci_b - cum_row[hb:hb + 1, :]), 0.0)
                    mmat = gmats[g] * (dec_f * dt_row[hf:hf + 1, :] + dec_b * dt_row[hb:hb + 1, :])
                    ys.append(_bdot(mmat.astype(BF16), xs_p))
                    ef.append(jnp.exp2(ci_f))
                    eb.append(jnp.exp2(ci_b))
                k = pr - g * hpg // 2
                yi = y_inter[g]
                y = (jnp.where(left, ys[0], ys[1])
                     + yi[:, k * LANES:(k + 1) * LANES] * jnp.where(left, ef[0], ef[1])
                     + yi[:, (hpg // 2 + k) * LANES:(hpg // 2 + k + 1) * LANES] * jnp.where(left, eb[0], eb[1])
                     + dexp_ref[:, pr * LANES:(pr + 1) * LANES] * xs_f)
                y_parts.append(y)
            y = jnp.concatenate(y_parts, axis=1)
            y = y * _silu(z_ref[orow, :])
            y = y * lax.rsqrt(jnp.mean(y * y, axis=-1, keepdims=True) + NORM_EPS) * nw_ref[...]
            o_ref[orow, :] = y.astype(o_ref.dtype)
            update_state(rows, 0, hf_ref, dt_row, cum_row, tot_row)
            return c
        lax.fori_loop(0, NCH, body, 0, unroll=SSD_UNROLL)

    return init, phase0, phase1


def _ssd_part(nb, seq, xbc, dt, z, cw, cb, dtb, alog, dexp, nw):
    nt = seq // SEQ_TILE
    ns = SSD_CONV_W // LANES
    prev, nxt = _slab_halo_specs_p0(ns, nt, nb * seq)
    in_specs = [_slab_spec_p0(ns, nt), prev, nxt, _both_spec(CHUNK, nt), _fwd_spec(SSD_W, nt)] + \
               [_const_spec(a.shape) for a in (cw, cb, dtb, alog, dexp, nw)]
    state = (SSD_HEADS // 2, SSD_STATE, LANES)
    scratch = [pltpu.VMEM((ns, SEQ_TILE + 2 * SUBLANES, LANES), F32), pltpu.VMEM((ns, seq, LANES), F32),
               pltpu.VMEM(state, F32), pltpu.VMEM(state, F32), pltpu.VMEM((seq // CHUNK,) + state, F32)]
    return "ssd", _ssd_body, (xbc, xbc, xbc, dt, z, cw, cb, dtb, alog, dexp, nw), in_specs, scratch


def _ret_body(rq_ref, rk_ref, rv_ref, rg_ref, cos_ref, sin_ref, dmat_ref, gin_ref, gst_ref, g128_ref, nw_ref,
              o_ref, q_ref, kt_ref, h_ref, hb_ref, *, p, t, tt, nt):
    nh = RET_HEADS
    half = RET_QK // 2
    hw = nh * half

    def init():
        h_ref[...] = jnp.zeros_like(h_ref)

    def rope(x):
        a, b = x[:, :hw], x[:, hw:]
        c, s = cos_ref[...], sin_ref[...]
        return jnp.concatenate([a * c - b * s, b * c + a * s], axis=1)

    def head_rows(kt, h):
        return jnp.concatenate([kt[h * half:(h + 1) * half, :], kt[hw + h * half:hw + (h + 1) * half, :]], axis=0)

    def widen(hc, h):
        n = hc.shape[1]
        sizes = [h * half, hw - half, hw - (h + 1) * half]
        z0, z1, z2 = [jnp.zeros((r, n), hc.dtype) for r in sizes]
        parts = [z0, hc[:half, :], z1, hc[half:, :], z2]
        return jnp.concatenate([q for q in parts if q.shape[0] > 0], axis=0)

    def update_state(gc, vrow, d):
        v = rv_ref[vrow, :]
        kt = kt_ref[gc]
        for h in range(nh):
            ks = (head_rows(kt, h) * gst_ref[d, h]).astype(BF16)
            s = _bdot(ks, v[:, h * RET_V:(h + 1) * RET_V].astype(BF16))
            h_ref[d, h] = g128_ref[h] * h_ref[d, h] + s

    def phase0():
        k = rope(rk_ref[...])
        for c in range(NCH):
            kt_ref[tt * NCH + c] = k[c * CHUNK:(c + 1) * CHUNK, :].T

        def body(i, c):
            ci = NCH - 1 - i
            gc = tt * NCH + ci
            hb_ref[gc] = h_ref[1]
            update_state(gc, _chunk_rows(ci), 1)
            return c
        lax.fori_loop(0, NCH, body, 0, unroll=RET_UNROLL)

    def phase1():
        q_ref[...] = rope(rq_ref[...]) * (RET_QK ** -0.5)
        lane = lax.broadcasted_iota(jnp.int32, (CHUNK, 2 * hw), 1) % hw

        def body(ci, c):
            gc = tt * NCH + ci
            rows = _chunk_rows(ci)
            q = q_ref[rows, :]
            v = rv_ref[rows, :]
            kt = kt_ref[gc].astype(BF16)
            outs = []
            for h in range(nh):
                qh = jnp.where((lane >= h * half) & (lane < (h + 1) * half), q, 0.0).astype(BF16)
                s = _bdot(qh, kt)
                o = _bdot((s * dmat_ref[h]).astype(BF16), v[:, h * RET_V:(h + 1) * RET_V].astype(BF16))
                h_cat = jnp.concatenate([widen(h_ref[0, h], h), widen(hb_ref[gc, h], h)], axis=1).astype(BF16)
                oi = _bdot(qh, h_cat)
                o = o + oi[:, :RET_V] * gin_ref[0, h] + oi[:, RET_V:] * gin_ref[1, h]
                outs.append(_head_norm(o))
            y = jnp.concatenate(outs, axis=1) * nw_ref[...]
            o_ref[rows, :] = (_silu(rg_ref[rows, :]) * y).astype(o_ref.dtype)
            update_state(gc, rows, 0)
            return c
        lax.fori_loop(0, NCH, body, 0, unroll=RET_UNROLL)

    return init, phase0, phase1


def _ret_part(nb, seq, rq, rk, rv, rg, cos, sin, dmat, gin, gst, g128, nw):
    nt = seq // SEQ_TILE
    w = RET_HEADS * RET_QK
    tab_both = pl.BlockSpec((SEQ_TILE, w // 2), lambda b, p, t: (_tile_pos(p, t, nt), 0))
    p0_spec = pl.BlockSpec((SEQ_TILE, w), lambda b, p, t: (b * nt + _p0_tile(p, t, nt), 0))
    in_specs = [_fwd_spec(w, nt), p0_spec, _both_spec(512, nt), _fwd_spec(512, nt), tab_both, tab_both] + \
               [_const_spec(a.shape) for a in (dmat, gin, gst, g128, nw)]
    scratch = [pltpu.VMEM((SEQ_TILE, w), F32), pltpu.VMEM((seq // CHUNK, w, CHUNK), F32),
               pltpu.VMEM((2, RET_HEADS, RET_QK, RET_V), F32),
               pltpu.VMEM((seq // CHUNK, RET_HEADS, RET_QK, RET_V), F32)]
    return "ret", _ret_body, (rq, rk, rv, rg, cos, sin, dmat, gin, gst, g128, nw), in_specs, scratch


def _mlstm_body(qk_ref, qkp_ref, qkn_ref, v_ref, gi_ref, gf_ref, mo_ref, cw_ref, cb_ref, gbi_ref, gbf_ref, nwt_ref,
                o_ref, xpad_ref, qk_s, c_ref, m_ref, c_stash, m_stash, *, p, t, tt, nt):
    nh = ML_HEADS
    nhd = 2 * nh

    def init():
        c_ref[...] = jnp.zeros_like(c_ref)
        m_ref[...] = jnp.zeros_like(m_ref)

    valid_b, valid_f = _tri_masks()
    le = valid_b.astype(F32)
    ge = valid_f.astype(F32)
    rowid8 = lax.broadcasted_iota(jnp.int32, (nhd, CHUNK), 0)
    rowid = lax.broadcasted_iota(jnp.int32, (CHUNK, CHUNK), 0)
    top = rowid < ML_QK
    ones_blk = jnp.ones((ML_NROWS, CHUNK), F32)
    srows = ML_V + ML_NROWS

    def row_gates(ci):
        rows = _chunk_rows(ci)
        gi_col = gi_ref[rows, :] + gbi_ref[...]
        li_row = gi_col.T[0:nhd, :]
        lf_row = -_softplus(-(gf_ref[rows, :] + gbf_ref[...]).T[0:nhd, :])
        b_row = jnp.where(rowid8 < nh, _xdot(lf_row, ge), _xdot(lf_row, le))
        tot = jnp.sum(lf_row, axis=1, keepdims=True)
        return gi_col, li_row, b_row, tot

    def update_state(srow, vrow, d, li_row, b_row, tot):
        w_row = tot - b_row + li_row
        m_loc = jnp.max(w_row, axis=1, keepdims=True)
        e_row = jnp.exp(w_row - m_loc)
        m_prev = m_ref[...]
        m_new = jnp.maximum(tot + m_prev, m_loc)
        dec = jnp.exp(tot + m_prev - m_new)
        inj = jnp.exp(m_loc - m_new)
        v = v_ref[vrow, :]
        for pr in range(nh // 2):
            k_pair = (qk_s[nh // 2 + pr, srow, :] * (ML_QK ** -0.5)).astype(BF16)
            for hl in range(2):
                h = 2 * pr + hl
                hd = d * nh + h
                e_b = jnp.broadcast_to(e_row[hd:hd + 1, :], (srows, CHUNK))
                vt = v[:, h * ML_V:(h + 1) * ML_V].T
                lhs = (jnp.concatenate([vt, ones_blk], axis=0) * e_b).astype(BF16)
                s = _bdot(lhs, k_pair)
                c_ref[hd] = (jnp.broadcast_to(dec[hd:hd + 1, :], (srows, CHUNK)) * c_ref[hd]
                             + jnp.broadcast_to(inj[hd:hd + 1, :], (srows, CHUNK)) * s)
        keep = (rowid8 >= nh) if d == 0 else (rowid8 < nh)
        m_ref[...] = jnp.where(keep, m_prev, m_new)

    def direction_out(d, qts, k_pairs, vts, r_col, b_row, c_aug, m_prev):
        mask = valid_f if d == 0 else valid_b
        outs = []
        for h in range(nh):
            hd = d * nh + h
            rj = jnp.where(mask, r_col[:, hd:hd + 1], -jnp.inf)
            u = jnp.maximum(m_prev[hd:hd + 1, :], jnp.max(rj, axis=0, keepdims=True))
            st = _bdot(k_pairs[h // 2], qts[h])
            pt = (st * jnp.exp(rj - u)).astype(BF16)
            lhs = jnp.concatenate([vts[h], ones_blk], axis=0).astype(BF16)
            nd = _bdot(lhs, pt) + jnp.exp(m_prev[hd:hd + 1, :] - u) * _bdot(c_aug[h].astype(BF16), qts[h])
            floor = jnp.exp(-(b_row[hd:hd + 1, :] + u))
            den = jnp.maximum(jnp.abs(nd[ML_V:ML_V + 1, :]), floor)
            outs.append(nd[0:ML_V, :] / den)
        return outs

    def phase0():
        _conv_to_cache(qk_ref, qkp_ref, qkn_ref, xpad_ref, qk_s, cw_ref, cb_ref, tt, nt, _silu)

        def body(i, c):
            ci = NCH - 1 - i
            gc = tt * NCH + ci
            c_stash[gc] = c_ref[nh:nhd]
            m_stash[gc] = m_ref[...]
            _, li_row, b_row, tot = row_gates(ci)
            update_state(_seq_rows(tt, ci), _chunk_rows(ci), 1, li_row, b_row, tot)
            return c
        lax.fori_loop(0, NCH, body, 0, unroll=ML_UNROLL)

    def phase1():
        def body(ci, c):
            gc = tt * NCH + ci
            srow = _seq_rows(tt, ci)
            rows = _chunk_rows(ci)
            gi_col, li_row, b_row, tot = row_gates(ci)
            r_col = gi_col - _rows_to_cols(b_row)
            qts, k_pairs, vts = [], [], []
            v = v_ref[rows, :]
            for pr in range(nh // 2):
                qt = qk_s[pr, srow, :].T
                qts.append(jnp.where(top, qt, 0.0).astype(BF16))
                qts.append(jnp.where(top, 0.0, qt).astype(BF16))
                k_pairs.append((qk_s[nh // 2 + pr, srow, :] * (ML_QK ** -0.5)).astype(BF16))
            for h in range(nh):
                vts.append(v[:, h * ML_V:(h + 1) * ML_V].T)
            hf = direction_out(0, qts, k_pairs, vts, r_col, b_row, c_ref[0:nh], m_ref[...])
            hb = direction_out(1, qts, k_pairs, vts, r_col, b_row, c_stash[gc], m_stash[gc])
            ys = []
            for h in range(nh):
                x = hf[h] + hb[h]
                mu = jnp.mean(x, axis=0, keepdims=True)
                xc = x - mu
                var = jnp.mean(xc * xc, axis=0, keepdims=True)
                ys.append((xc * lax.rsqrt(var + NORM_EPS) * nwt_ref[h]).T)
            y = jnp.concatenate(ys, axis=1)
            o_ref[rows, :] = (_sigmoid(mo_ref[rows, :]) * y).astype(o_ref.dtype)
            update_state(srow, rows, 0, li_row, b_row, tot)
            return c
        lax.fori_loop(0, NCH, body, 0, unroll=ML_UNROLL)

    return init, phase0, phase1


def _mlstm_part(nb, seq, mqk, mv, mgi, mgf, mo, cw, cb, gbi, gbf, nwt):
    nt = seq // SEQ_TILE
    ns = 2 * ML_HEADS * ML_QK // LANES
    prev, nxt = _slab_halo_specs_p0(ns, nt, nb * seq)
    in_specs = [_slab_spec_p0(ns, nt), prev, nxt, _both_spec(512, nt), _both_spec(CHUNK, nt), _both_spec(CHUNK, nt),
                _fwd_spec(512, nt)] + [_const_spec(a.shape) for a in (cw, cb, gbi, gbf, nwt)]
    scratch = [pltpu.VMEM((ns, SEQ_TILE + 2 * SUBLANES, LANES), F32), pltpu.VMEM((ns, seq, LANES), F32),
               pltpu.VMEM((2 * ML_HEADS, ML_V + ML_NROWS, CHUNK), F32),
               pltpu.VMEM((2 * ML_HEADS, CHUNK), F32),
               pltpu.VMEM((seq // CHUNK, ML_HEADS, ML_V + ML_NROWS, CHUNK), F32),
               pltpu.VMEM((seq // CHUNK, 2 * ML_HEADS, CHUNK), F32)]
    return "mlstm", _mlstm_body, (mqk, mqk, mqk, mv, mgi, mgf, mo, cw, cb, gbi, gbf, nwt), in_specs, scratch


def _pad_cols(a, width):
    return jnp.pad(a, [(0, 0)] * (a.ndim - 1) + [(0, width - a.shape[-1])])


def _prepare(w_in, lru_conv_w, lru_conv_b, lru_gate_w, lru_gate_b, lru_lambda, ssd_conv_w, ssd_conv_b, ssd_dt_bias,
             ssd_a_log, ssd_d, ssd_norm_w, ret_norm_w, mlstm_conv_w, mlstm_conv_b, mlstm_gate_b, mlstm_norm_w,
             w_branch, w_out, w_ffn_in, w_ffn_out, ln_g, ln_b):
    sp = {}
    off = 0
    for name, w in (("xa", 512), ("ga", 512), ("z", 512), ("xbc", 768), ("dt", 16), ("rq", 256), ("rk", 256),
                    ("rv", 512), ("rg", 512), ("mqk", 512), ("mv", 512), ("mo", 512), ("mg", 16), ("mix", 4096)):
        sp[name] = w_in[:, :, off:off + w]
        off += w
    mg = sp["mg"].reshape(DEPTH, D_MODEL, 2, 2, ML_HEADS)

    def halves_major(w):
        w = w.reshape(DEPTH, D_MODEL, RET_HEADS, 2, RET_QK // 2)
        return jnp.moveaxis(w, 3, 2).reshape(DEPTH, D_MODEL, RET_HEADS * RET_QK)

    cols = {"dt": _pad_cols(sp["dt"], CHUNK), "rq": halves_major(sp["rq"]), "rk": halves_major(sp["rk"]),
            "mgi": _pad_cols(mg[:, :, :, 0, :].reshape(DEPTH, D_MODEL, 8), CHUNK),
            "mgf": _pad_cols(mg[:, :, :, 1, :].reshape(DEPTH, D_MODEL, 8), CHUNK)}
    wp = jnp.concatenate([cols[n] if n in cols else sp[n] for n, _ in PROJ_OUTS], axis=-1).astype(BF16)
    gbi = _pad_cols(mlstm_gate_b[:, :, 0, :].reshape(DEPTH, 1, 8), CHUNK)
    gbf = _pad_cols(mlstm_gate_b[:, :, 1, :].reshape(DEPTH, 1, 8), CHUNK)
    ml_nwt = jnp.broadcast_to(mlstm_norm_w.reshape(DEPTH, ML_HEADS, ML_V, 1), (DEPTH, ML_HEADS, ML_V, CHUNK))

    eye = jnp.eye(LRU_BLOCKS, dtype=F32)
    dense = jnp.einsum("ldghij,hk->ldghikj", lru_gate_w, eye).reshape(DEPTH, 2, 2, LRU_W, LRU_W)
    lru_wg = jnp.concatenate([dense[:, :, 0], dense[:, :, 1]], axis=-1)

    return dict(
        wp=wp, wmix=sp["mix"].astype(BF16),
        ffn_i=w_ffn_in.astype(BF16), ffn_d=w_ffn_out.astype(BF16),
        ln_g=ln_g[:, :, None, :], ln_b=ln_b[:, :, None, :],
        wb=w_branch.astype(BF16), wo=w_out.astype(BF16),
        lru_cw=lru_conv_w, lru_cb=lru_conv_b[:, None, :], lru_wg=(0.5 * lru_wg).astype(BF16),
        lru_gb=0.5 * lru_gate_b.reshape(DEPTH, 2, 1, 2 * LRU_W), lru_lam=lru_lambda[:, :, None, :],
        ssd_cw=ssd_conv_w, ssd_cb=ssd_conv_b[:, None, :],
        ssd_dtb=_pad_cols(ssd_dt_bias.reshape(DEPTH, 1, 16), CHUNK),
        ssd_alog=_pad_cols(ssd_a_log.reshape(DEPTH, 1, 16), CHUNK),
        ssd_dexp=jnp.repeat(ssd_d, SSD_HEAD_DIM, axis=-1)[:, None, :], ssd_nw=ssd_norm_w[:, None, :],
        ret_nw=ret_norm_w[:, None, :],
        ml_cw=mlstm_conv_w, ml_cb=mlstm_conv_b[:, None, :], ml_gbi=gbi, ml_gbf=gbf, ml_nwt=ml_nwt,
    )


def _ret_tables(seq):
    d = RET_QK
    inv = ROPE_BASE ** (-jnp.arange(0, d, 2, dtype=F32) / d)
    ang = jnp.arange(seq, dtype=F32)[:, None] * inv[None, :]
    cos = jnp.tile(jnp.cos(ang), (1, RET_HEADS))
    sin = jnp.tile(jnp.sin(ang), (1, RET_HEADS))
    lg = jnp.log1p(-jnp.exp2(-5.0 - jnp.arange(RET_HEADS, dtype=F32)))[:, None, None]
    pos = jnp.arange(CHUNK, dtype=F32)
    dmat = jnp.exp(lg * jnp.abs(pos[:, None] - pos[None, :])[None])
    ones = jnp.ones((1, 1, RET_V), F32)
    gin = jnp.stack([jnp.exp(lg * (pos + 1.0)[None, :, None]) * ones,
                     jnp.exp(lg * (CHUNK - pos)[None, :, None]) * ones])
    ones = jnp.ones((1, RET_QK, 1), F32)
    gst = jnp.stack([jnp.exp(lg * (CHUNK - 1.0 - pos)[None, None, :]) * ones,
                     jnp.exp(lg * pos[None, None, :]) * ones])
    g128 = jnp.exp(lg * float(CHUNK)) * jnp.ones((1, RET_QK, RET_V), F32)
    return cos, sin, dmat, gin, gst, g128


def _trunk(x3, pw):
    nb, seq, _ = x3.shape
    x = x3.reshape(nb * seq, D_MODEL)
    tabs = _ret_tables(seq)
    for l in range(DEPTH):
        x = _ffn_call(x, pw["ffn_i"][l, 0], pw["ffn_d"][l, 0], pw["ln_g"][l, 0], pw["ln_b"][l, 0])
        pr = dict(zip([n for n, _ in PROJ_OUTS], _proj_call(x, pw["wp"][l])))
        parts = dict(
            lru=_lru_part(nb, seq, pr["xa"], pr["ga"], pw["lru_cw"][l], pw["lru_cb"][l], pw["lru_wg"][l],
                          pw["lru_gb"][l], pw["lru_lam"][l]),
            ssd=_ssd_part(nb, seq, pr["xbc"], pr["dt"], pr["z"], pw["ssd_cw"][l], pw["ssd_cb"][l], pw["ssd_dtb"][l],
                          pw["ssd_alog"][l], pw["ssd_dexp"][l], pw["ssd_nw"][l]),
            ret=_ret_part(nb, seq, pr["rq"], pr["rk"], pr["rv"], pr["rg"], *tabs, pw["ret_nw"][l]),
            mlstm=_mlstm_part(nb, seq, pr["mqk"], pr["mv"], pr["mgi"], pr["mgf"], pr["mo"], pw["ml_cw"][l],
                              pw["ml_cb"][l], pw["ml_gbi"][l], pw["ml_gbf"][l], pw["ml_nwt"][l]))
        ys = {}
        for group in MIXER_GROUPS:
            ys.update(zip(group, _mixers_call(nb, seq, [parts[n] for n in group])))
        x = _merge_call(x, (ys["lru"], ys["ssd"], ys["ret"], ys["mlstm"]), pw["wmix"][l], pw["wb"][l], pw["wo"][l],
                        pw["ln_g"][l, 1], pw["ln_b"][l, 1])
        x = _ffn_call(x, pw["ffn_i"][l, 1], pw["ffn_d"][l, 1], pw["ln_g"][l, 2], pw["ln_b"][l, 2])
    return x.reshape(nb, seq, D_MODEL)


def kernel(x_prompt, x_sample, w_in, lru_conv_w, lru_conv_b, lru_gate_w, lru_gate_b, lru_lambda, ssd_conv_w,
           ssd_conv_b, ssd_dt_bias, ssd_a_log, ssd_d, ssd_norm_w, ret_norm_w, mlstm_conv_w, mlstm_conv_b,
           mlstm_gate_b, mlstm_norm_w, w_branch, w_out, w_ffn_in, w_ffn_out, ln_g, ln_b):
    pw = _prepare(w_in, lru_conv_w, lru_conv_b, lru_gate_w, lru_gate_b, lru_lambda, ssd_conv_w, ssd_conv_b,
                  ssd_dt_bias, ssd_a_log, ssd_d, ssd_norm_w, ret_norm_w, mlstm_conv_w, mlstm_conv_b, mlstm_gate_b,
                  mlstm_norm_w, w_branch, w_out, w_ffn_in, w_ffn_out, ln_g, ln_b)
    return _trunk(x_prompt, pw), _trunk(x_sample, pw)
```

```python
import functools

import jax
import jax.numpy as jnp
from jax import lax
from jax.experimental import pallas as pl
from jax.experimental.pallas import tpu as pltpu

F32 = jnp.float32
BF16 = jnp.bfloat16
HIGHEST = lax.Precision.HIGHEST

D_MODEL = 1024
DEPTH = 4
D_FF = 2816
CHUNK = 128
LRU_W = 512
LRU_BLOCKS = 8
LRU_C = 8.0
SSD_HEADS = 8
SSD_HEAD_DIM = 64
SSD_W = 512
SSD_STATE = 64
SSD_GROUPS = 2
SSD_CONV_W = SSD_W + 2 * SSD_GROUPS * SSD_STATE
RET_HEADS = 4
RET_QK = 64
RET_V = 128
ROPE_BASE = 10000.0
ML_HEADS = 4
ML_QK = 64
ML_V = 128
DN_ALPHA = (2.0 * DEPTH) ** 0.25
NORM_EPS = 1e-5
LOG2E = 1.4426950408889634
TINY = 1.1754944e-38

VMEM_LIMIT_BYTES = 56 * 1024 * 1024
FF_CHUNK = 256
TOKEN_TILE = 512
SEQ_TILE = 512
NCH = SEQ_TILE // CHUNK
SUBLANES = 8
LANES = 128
LRU_UNROLL = NCH
SSD_UNROLL = NCH
RET_UNROLL = NCH
ML_UNROLL = NCH
ML_NROWS = 16
MIXER_GROUPS = (("ret", "lru"), ("mlstm", "ssd"))

PROJ_OUTS = (("xa", 512), ("ga", 512), ("z", 512), ("xbc", 768), ("dt", 128), ("rq", 256), ("rk", 256),
             ("rv", 512), ("rg", 512), ("mqk", 512), ("mv", 512), ("mo", 512), ("mgi", 128), ("mgf", 128))
SLAB_OUTS = ("xa", "ga", "xbc", "mqk")


def _cparams(sem):
    return pltpu.CompilerParams(dimension_semantics=sem, vmem_limit_bytes=VMEM_LIMIT_BYTES)


def _const_spec(shape):
    nd = len(shape)
    return pl.BlockSpec(shape, lambda *_: (0,) * nd, pipeline_mode=pl.Buffered(1))


def _bdot(a, b):
    return jnp.dot(a, b, preferred_element_type=F32)


def _xdot(a, b):
    return jnp.dot(a, b, preferred_element_type=F32, precision=HIGHEST)


def _sigmoid(x):
    return 0.5 * jnp.tanh(0.5 * x) + 0.5


def _silu(x):
    return x * _sigmoid(x)


def _softplus(x):
    return jnp.maximum(x, 0.0) + jnp.log1p(jnp.exp(-jnp.abs(x)))


def _layer_norm(y, g, b):
    mu = jnp.mean(y, axis=-1, keepdims=True)
    yc = y - mu
    var = jnp.mean(yc * yc, axis=-1, keepdims=True)
    return yc * lax.rsqrt(var + NORM_EPS) * g + b


def _ffn_kernel(x_ref, wi_ref, wd_ref, g_ref, b_ref, o_ref, acc_ref):
    x = x_ref[...]
    xb = x.astype(BF16)
    nf = D_FF // FF_CHUNK
    for f in range(nf):
        gate = _bdot(xb, wi_ref[:, f * FF_CHUNK:(f + 1) * FF_CHUNK])
        up = _bdot(xb, wi_ref[:, D_FF + f * FF_CHUNK:D_FF + (f + 1) * FF_CHUNK])
        h = (_silu(gate) * up).astype(BF16)
        part = _bdot(h, wd_ref[f * FF_CHUNK:(f + 1) * FF_CHUNK, :])
        if f == 0:
            acc_ref[...] = part
        else:
            acc_ref[...] += part
    y = DN_ALPHA * x + 0.5 * acc_ref[...]
    o_ref[...] = _layer_norm(y, g_ref[...], b_ref[...])


def _ffn_call(x, wi, wd, g, b):
    m = x.shape[0]
    tm = TOKEN_TILE
    return pl.pallas_call(
        _ffn_kernel,
        out_shape=jax.ShapeDtypeStruct((m, D_MODEL), F32),
        grid=(m // tm,),
        in_specs=[pl.BlockSpec((tm, D_MODEL), lambda i: (i, 0)),
                  _const_spec(wi.shape), _const_spec(wd.shape), _const_spec(g.shape), _const_spec(b.shape)],
        out_specs=pl.BlockSpec((tm, D_MODEL), lambda i: (i, 0)),
        scratch_shapes=[pltpu.VMEM((tm, D_MODEL), F32)],
        compiler_params=_cparams(("parallel",)),
        name="ffn",
    )(x, wi, wd, g, b)


def _proj_kernel(x_ref, w_ref, *o_refs):
    xb = x_ref[...].astype(BF16)
    off = 0
    for o_ref in o_refs:
        if len(o_ref.shape) == 3:
            w = o_ref.shape[0] * LANES
            res = _bdot(xb, w_ref[:, off:off + w])
            for k in range(o_ref.shape[0]):
                o_ref[k] = res[:, k * LANES:(k + 1) * LANES]
            off += w
        else:
            w = o_ref.shape[1]
            o_ref[...] = _bdot(xb, w_ref[:, off:off + w])
            off += w


def _proj_call(x, wp):
    m = x.shape[0]
    tm = TOKEN_TILE
    shapes, specs = [], []
    for name, w in PROJ_OUTS:
        if name in SLAB_OUTS:
            shapes.append(jax.ShapeDtypeStruct((w // LANES, m, LANES), F32))
            specs.append(pl.BlockSpec((w // LANES, tm, LANES), lambda i: (0, i, 0)))
        else:
            shapes.append(jax.ShapeDtypeStruct((m, w), F32))
            specs.append(pl.BlockSpec((tm, w), lambda i: (i, 0)))
    return pl.pallas_call(
        _proj_kernel,
        out_shape=shapes,
        grid=(m // tm,),
        in_specs=[pl.BlockSpec((tm, D_MODEL), lambda i: (i, 0)), _const_spec(wp.shape)],
        out_specs=specs,
        compiler_params=_cparams(("parallel",)),
        name="proj",
    )(x, wp)


def _merge_ffn_kernel(x_ref, ya_ref, yb_ref, yc_ref, yd_ref, wg_ref, wb_ref, wo_ref, g1_ref, b1_ref,
                      wi_ref, wd_ref, g2_ref, b2_ref, o_ref, acc_ref):
    x = x_ref[...]
    xb = x.astype(BF16)
    merged = None
    for i, y_ref in enumerate((ya_ref, yb_ref, yc_ref, yd_ref)):
        gate = _sigmoid(_bdot(xb, wg_ref[:, i * D_MODEL:(i + 1) * D_MODEL]))
        term = gate * _bdot(y_ref[...], wb_ref[i])
        merged = term if merged is None else merged + term
    out = _bdot(merged.astype(BF16), wo_ref[...])
    x2 = _layer_norm(DN_ALPHA * x + out, g1_ref[...], b1_ref[...])
    x2b = x2.astype(BF16)
    for f in range(D_FF // FF_CHUNK):
        gate = _bdot(x2b, wi_ref[:, f * FF_CHUNK:(f + 1) * FF_CHUNK])
        up = _bdot(x2b, wi_ref[:, D_FF + f * FF_CHUNK:D_FF + (f + 1) * FF_CHUNK])
        h = (_silu(gate) * up).astype(BF16)
        part = _bdot(h, wd_ref[f * FF_CHUNK:(f + 1) * FF_CHUNK, :])
        if f == 0:
            acc_ref[...] = part
        else:
            acc_ref[...] += part
    o_ref[...] = _layer_norm(DN_ALPHA * x2 + 0.5 * acc_ref[...], g2_ref[...], b2_ref[...])


def _merge_ffn_call(x, ys, wg, wb, wo, g1, b1, wi, wd, g2, b2):
    m = x.shape[0]
    tm = TOKEN_TILE
    consts = (wg, wb, wo, g1, b1, wi, wd, g2, b2)
    return pl.pallas_call(
        _merge_ffn_kernel,
        out_shape=jax.ShapeDtypeStruct((m, D_MODEL), F32),
        grid=(m // tm,),
        in_specs=[pl.BlockSpec((tm, D_MODEL), lambda i: (i, 0))]
                 + [pl.BlockSpec((tm, 512), lambda i: (i, 0)) for _ in range(4)]
                 + [_const_spec(c.shape) for c in consts],
        out_specs=pl.BlockSpec((tm, D_MODEL), lambda i: (i, 0)),
        scratch_shapes=[pltpu.VMEM((tm, D_MODEL), F32)],
        compiler_params=_cparams(("parallel",)),
        name="merge_ffn",
    )(x, *ys, *consts)


def _tile_pos(p, t, nt):
    return t + (1 - p) * (nt - 1 - 2 * t)


def _p0_tile(p, t, nt):
    return (1 - p) * (nt - 1 - t)


def _both_spec(w, nt):
    return pl.BlockSpec((SEQ_TILE, w), lambda b, p, t: (b * nt + _tile_pos(p, t, nt), 0))


def _fwd_spec(w, nt):
    return pl.BlockSpec((SEQ_TILE, w), lambda b, p, t: (b * nt + p * t, 0))


def _slab_spec_p0(ns, nt):
    return pl.BlockSpec((ns, SEQ_TILE, LANES), lambda b, p, t: (0, b * nt + _p0_tile(p, t, nt), 0))


def _slab_spec_p1(ns, nt):
    return pl.BlockSpec((ns, SEQ_TILE, LANES), lambda b, p, t: (0, b * nt + p * t, 0))


def _slab_halo_specs_p0(ns, nt, n_rows):
    per = SEQ_TILE // SUBLANES
    last = n_rows // SUBLANES - 1
    prev = pl.BlockSpec((ns, SUBLANES, LANES),
                        lambda b, p, t: (0, jnp.maximum((b * nt + _p0_tile(p, t, nt)) * per - 1, 0), 0))
    nxt = pl.BlockSpec((ns, SUBLANES, LANES),
                       lambda b, p, t: (0, jnp.minimum((b * nt + _p0_tile(p, t, nt) + 1) * per, last), 0))
    return prev, nxt


def _conv_to_cache(x_ref, prev_ref, next_ref, xpad_ref, cache_ref, w_ref, b_ref, tt, nt, act):
    ns = x_ref.shape[0]
    for k in range(ns):
        xpad_ref[k, 0:SUBLANES, :] = jnp.where(tt == 0, 0.0, prev_ref[k])
        xpad_ref[k, SUBLANES:SUBLANES + SEQ_TILE, :] = x_ref[k]
        xpad_ref[k, SUBLANES + SEQ_TILE:2 * SUBLANES + SEQ_TILE, :] = jnp.where(tt == nt - 1, 0.0, next_ref[k])
    base = tt * SEQ_TILE
    for k in range(ns):
        w = w_ref[:, k * LANES:(k + 1) * LANES]
        bias = b_ref[:, k * LANES:(k + 1) * LANES]
        for r in range(0, SEQ_TILE, CHUNK):
            taps = [xpad_ref[k, SUBLANES - 2 + j + r:SUBLANES - 2 + j + r + CHUNK, :] for j in range(4)]
            y = w[0:1, :] * taps[0] + w[1:2, :] * taps[1] + w[2:3, :] * taps[2] + w[3:4, :] * taps[3] + bias
            cache_ref[k, pl.ds(pl.multiple_of(base + r, CHUNK), CHUNK), :] = act(y)


def _mixers_kernel(*refs, nt, layout):
    p = pl.program_id(1)
    t = pl.program_id(2)
    tt = _tile_pos(p, t, nt)
    n_in = sum(ni for _, ni, _ in layout)
    n_out = len(layout)
    parts, i_in, i_sc = [], 0, n_in + n_out
    for k, (body, ni, ns) in enumerate(layout):
        parts.append(body(*refs[i_in:i_in + ni], refs[n_in + k], *refs[i_sc:i_sc + ns], p=p, t=t, tt=tt, nt=nt))
        i_in += ni
        i_sc += ns

    @pl.when(t == 0)
    def _():
        for init, _, _ in parts:
            init()

    @pl.when(p == 0)
    def _():
        for _, phase0, _ in parts:
            phase0()

    @pl.when(p == 1)
    def _():
        for _, _, phase1 in parts:
            phase1()


def _mixers_call(nb, seq, parts):
    nt = seq // SEQ_TILE
    layout = tuple((body, len(ops), len(scr)) for _, body, ops, _, scr in parts)
    out_spec = pl.BlockSpec((SEQ_TILE, 512), lambda b, p, t: (b * nt + p * t, 0))
    return pl.pallas_call(
        functools.partial(_mixers_kernel, nt=nt, layout=layout),
        out_shape=[jax.ShapeDtypeStruct((nb * seq, 512), BF16) for _ in parts],
        grid=(nb, 2, nt),
        in_specs=[sp for _, _, _, specs, _ in parts for sp in specs],
        out_specs=[out_spec for _ in parts],
        scratch_shapes=[sc for _, _, _, _, scr in parts for sc in scr],
        compiler_params=_cparams(("arbitrary", "arbitrary", "arbitrary")),
        name="_".join(name for name, _, _, _, _ in parts),
    )(*[op for _, _, ops, _, _ in parts for op in ops])


def _tri_masks():
    ii = lax.broadcasted_iota(jnp.int32, (CHUNK, CHUNK), 0)
    jj = lax.broadcasted_iota(jnp.int32, (CHUNK, CHUNK), 1)
    return jj <= ii, jj >= ii


def _rows_to_cols(x):
    r = x.shape[0]
    return jnp.concatenate([x, jnp.zeros((CHUNK - r, x.shape[1]), x.dtype)], axis=0).T


def _chunk_rows(ci):
    return pl.ds(pl.multiple_of(ci * CHUNK, CHUNK), CHUNK)


def _seq_rows(tt, ci):
    return pl.ds(pl.multiple_of(tt * SEQ_TILE + ci * CHUNK, CHUNK), CHUNK)


def _head_norm(x):
    mu = jnp.mean(x, axis=-1, keepdims=True)
    xc = x - mu
    var = jnp.mean(xc * xc, axis=-1, keepdims=True)
    return xc * lax.rsqrt(var + NORM_EPS)


def _scan_rows(a, b, reverse):
    n = a.shape[0]
    row = lax.broadcasted_iota(jnp.int32, a.shape, 0)
    s = 1
    while s < n:
        if reverse:
            a_sh = pltpu.roll(a, n - s, 0)
            b_sh = pltpu.roll(b, n - s, 0)
            valid = row < n - s
        else:
            a_sh = pltpu.roll(a, s, 0)
            b_sh = pltpu.roll(b, s, 0)
            valid = row >= s
        b = a * jnp.where(valid, b_sh, 0.0) + b
        a = a * jnp.where(valid, a_sh, 1.0)
        s *= 2
    return a, b


LRU_GROUPS = CHUNK // SUBLANES


def _load_interleaved(ref, row0):
    ns = ref.shape[0]
    return jnp.concatenate(
        [jnp.concatenate([ref[k, pl.ds(row0 + s, LRU_GROUPS, stride=SUBLANES), :] for k in range(ns)], axis=1)
         for s in range(SUBLANES)], axis=0)


def _lru_body(xa_ref, xap_ref, xan_ref, ga_ref, cw_ref, cb_ref, wg_ref, gb_ref, lam_ref, o_ref,
              xpad_ref, xc_ref, carry_ref, hb_ref, y_ref, *, p, t, tt, nt):
    ng = LRU_GROUPS

    def init():
        carry_ref[...] = jnp.zeros_like(carry_ref)

    def direction(ci, d):
        xc = _load_interleaved(xc_ref, pl.multiple_of(tt * SEQ_TILE + ci * CHUNK, CHUNK))
        th = jnp.tanh(_bdot(xc.astype(BF16), wg_ref[d]) + gb_ref[d])
        c_half = (-0.5 * LRU_C) * _softplus(-lam_ref[d])
        log_a = c_half * th[:, :LRU_W] + c_half
        a = jnp.exp(log_a)
        v = -jnp.tanh(log_a) * (a * a + 1.0)
        ix = (th[:, LRU_W:] + 1.0) * (0.5 * xc)
        b = (v * lax.rsqrt(jnp.maximum(v, TINY))) * ix
        sa = [a[s * ng:(s + 1) * ng, :] for s in range(SUBLANES)]
        sb = [b[s * ng:(s + 1) * ng, :] for s in range(SUBLANES)]
        order = range(SUBLANES - 2, -1, -1) if d == 1 else range(1, SUBLANES)
        for s in order:
            q = s + 1 if d == 1 else s - 1
            sb[s] = sa[s] * sb[q] + sb[s]
            sa[s] = sa[s] * sa[q]
        edge = 0 if d == 1 else SUBLANES - 1
        ga, gb = _scan_rows(sa[edge], sb[edge], reverse=(d == 1))
        row = lax.broadcasted_iota(jnp.int32, ga.shape, 0)
        if d == 1:
            ga = jnp.where(row == ng - 1, 1.0, pltpu.roll(ga, ng - 1, 0))
            gb = jnp.where(row == ng - 1, 0.0, pltpu.roll(gb, ng - 1, 0))
        else:
            ga = jnp.where(row == 0, 1.0, pltpu.roll(ga, 1, 0))
            gb = jnp.where(row == 0, 0.0, pltpu.roll(gb, 1, 0))
        h_in = ga * carry_ref[...] + gb
        hs = [sa[s] * h_in + sb[s] for s in range(SUBLANES)]
        carry_ref[...] = hs[0][0:1, :] if d == 1 else hs[SUBLANES - 1][ng - 1:ng, :]
        return jnp.concatenate(hs, axis=0)

    def phase0():
        _conv_to_cache(xa_ref, xap_ref, xan_ref, xpad_ref, xc_ref, cw_ref, cb_ref, tt, nt, lambda v: v)

        def body(i, c):
            ci = NCH - 1 - i
            hb_ref[tt * NCH + ci] = direction(ci, 1)
            return c
        lax.fori_loop(0, NCH, body, 0, unroll=LRU_UNROLL)

    def phase1():
        def body(ci, c):
            h = direction(ci, 0) + hb_ref[tt * NCH + ci]
            y = jax.nn.gelu(_load_interleaved(ga_ref, pl.multiple_of(ci * CHUNK, CHUNK))) * h
            for s in range(SUBLANES):
                for k in range(LRU_W // LANES):
                    y_ref[k, pl.ds(s, ng, stride=SUBLANES), :] = y[s * ng:(s + 1) * ng, k * LANES:(k + 1) * LANES]
            o_ref[_chunk_rows(ci), :] = jnp.concatenate(
                [y_ref[k] for k in range(LRU_W // LANES)], axis=1).astype(o_ref.dtype)
            return c
        lax.fori_loop(0, NCH, body, 0, unroll=LRU_UNROLL)

    return init, phase0, phase1


def _lru_part(nb, seq, xa, ga, cw, cb, wg, gb, lam):
    nt = seq // SEQ_TILE
    ns = LRU_W // LANES
    prev, nxt = _slab_halo_specs_p0(ns, nt, nb * seq)
    in_specs = [_slab_spec_p0(ns, nt), prev, nxt, _slab_spec_p1(ns, nt),
                _const_spec(cw.shape), _const_spec(cb.shape), _const_spec(wg.shape),
                _const_spec(gb.shape), _const_spec(lam.shape)]
    scratch = [pltpu.VMEM((ns, SEQ_TILE + 2 * SUBLANES, LANES), F32), pltpu.VMEM((ns, seq, LANES), F32),
               pltpu.VMEM((1, LRU_W), F32), pltpu.VMEM((seq // CHUNK, CHUNK, LRU_W), F32),
               pltpu.VMEM((ns, CHUNK, LANES), F32)]
    return "lru", _lru_body, (xa, xa, xa, ga, cw, cb, wg, gb, lam), in_specs, scratch


def _ssd_body(xbc_ref, xbcp_ref, xbcn_ref, dt_ref, z_ref, cw_ref, cb_ref, dtb_ref, alog_ref, dexp_ref, nw_ref,
              o_ref, xpad_ref, xc_ref, hf_ref, hbk_ref, stash_ref, *, p, t, tt, nt):
    nh = SSD_HEADS
    nhd = 2 * nh
    npair = nh // 2
    hpg = nh // SSD_GROUPS

    def init():
        hf_ref[...] = jnp.zeros_like(hf_ref)
        hbk_ref[...] = jnp.zeros_like(hbk_ref)

    lower, upper = _tri_masks()
    le = lower.astype(F32)
    ge = upper.astype(F32)
    lane = lax.broadcasted_iota(jnp.int32, (1, CHUNK), 1)
    neg_a = jnp.where(lane < nhd, -LOG2E * jnp.exp(alog_ref[...]), 0.0)
    rowid = lax.broadcasted_iota(jnp.int32, (nhd, CHUNK), 0)
    colid = lax.broadcasted_iota(jnp.int32, (CHUNK, CHUNK), 1)
    left = colid < SSD_HEAD_DIM

    def row_gates(ci):
        dt_col = _softplus(dt_ref[_chunk_rows(ci), :] + dtb_ref[...])
        la_col = dt_col * neg_a
        dt_row = dt_col.T[0:nhd, :]
        la_row = la_col.T[0:nhd, :]
        cum_row = jnp.where(rowid < nh, _xdot(la_row, ge), _xdot(la_row, le))
        tot_row = jnp.sum(la_row, axis=1, keepdims=True)
        return la_col, dt_row, cum_row, tot_row

    def update_state(rows, d, h_ref, dt_row, cum_row, tot_row):
        bm_t = xc_ref[4, rows, :].T
        w_row = jnp.exp2(tot_row - cum_row) * dt_row
        dec = jnp.broadcast_to(jnp.exp2(tot_row), (nhd, CHUNK))
        shp = (SSD_STATE, LANES)
        keep = lax.broadcasted_iota(jnp.int32, shp, 1) < SSD_HEAD_DIM
        for pr in range(npair):
            g = (2 * pr) // hpg
            xs_p = xc_ref[pr, rows, :].astype(BF16)
            bg = bm_t[g * SSD_STATE:(g + 1) * SSD_STATE, :]
            hd0 = d * nh + 2 * pr
            s0 = _bdot((bg * w_row[hd0:hd0 + 1, :]).astype(BF16), xs_p)
            s1 = _bdot((bg * w_row[hd0 + 1:hd0 + 2, :]).astype(BF16), xs_p)
            dec_p = jnp.where(keep, jnp.broadcast_to(dec[hd0:hd0 + 1, :], shp),
                              jnp.broadcast_to(dec[hd0 + 1:hd0 + 2, :], shp))
            h_ref[pr] = dec_p * h_ref[pr] + jnp.where(keep, s0, s1)

    def phase0():
        _conv_to_cache(xbc_ref, xbcp_ref, xbcn_ref, xpad_ref, xc_ref, cw_ref, cb_ref, tt, nt, _silu)

        def body(i, c):
            ci = NCH - 1 - i
            stash_ref[tt * NCH + ci] = hbk_ref[...]
            _, dt_row, cum_row, tot_row = row_gates(ci)
            update_state(_seq_rows(tt, ci), 1, hbk_ref, dt_row, cum_row, tot_row)
            return c
        lax.fori_loop(0, NCH, body, 0, unroll=SSD_UNROLL)

    def phase1():
        def body(ci, c):
            gc = tt * NCH + ci
            rows = _seq_rows(tt, ci)
            orow = _chunk_rows(ci)
            la_col, dt_row, cum_row, tot_row = row_gates(ci)
            cum_col = jnp.where(colid < nh, _xdot(le, la_col), _xdot(ge, la_col))
            bm_t = xc_ref[4, rows, :].T
            cm = xc_ref[5, rows, :]
            gmats, y_inter = [], []
            for g in range(SSD_GROUPS):
                cm_g = cm[:, g * SSD_STATE:(g + 1) * SSD_STATE].astype(BF16)
                gmats.append(_bdot(cm_g, bm_t[g * SSD_STATE:(g + 1) * SSD_STATE, :].astype(BF16)))
                prs = range(g * hpg // 2, (g + 1) * hpg // 2)
                h_cat = jnp.concatenate([hf_ref[pr] for pr in prs] + [stash_ref[gc, pr] for pr in prs], axis=1)
                y_inter.append(_bdot(cm_g, h_cat.astype(BF16)))
            y_parts = []
            for pr in range(npair):
                g = (2 * pr) // hpg
                xs_f = xc_ref[pr, rows, :]
                xs_p = xs_f.astype(BF16)
                ys, ef, eb = [], [], []
                for hl in range(2):
                    hf = 2 * pr + hl
                    hb = nh + hf
                    ci_f = jnp.broadcast_to(cum_col[:, hf:hf + 1], (CHUNK, CHUNK))
                    ci_b = jnp.broadcast_to(cum_col[:, hb:hb + 1], (CHUNK, CHUNK))
                    dec_f = jnp.where(lower, jnp.exp2(ci_f - cum_row[hf:hf + 1, :]), 0.0)
                    dec_b = jnp.where(upper, jnp.exp2(ci_b - cum_row[hb:hb + 1, :]), 0.0)
                    mmat = gmats[g] * (dec_f * dt_row[hf:hf + 1, :] + dec_b * dt_row[hb:hb + 1, :])
                    ys.append(_bdot(mmat.astype(BF16), xs_p))
                    ef.append(jnp.exp2(ci_f))
                    eb.append(jnp.exp2(ci_b))
                k = pr - g * hpg // 2
                yi = y_inter[g]
                y = (jnp.where(left, ys[0], ys[1])
                     + yi[:, k * LANES:(k + 1) * LANES] * jnp.where(left, ef[0], ef[1])
                     + yi[:, (hpg // 2 + k) * LANES:(hpg // 2 + k + 1) * LANES] * jnp.where(left, eb[0], eb[1])
                     + dexp_ref[:, pr * LANES:(pr + 1) * LANES] * xs_f)
                y_parts.append(y)
            y = jnp.concatenate(y_parts, axis=1)
            y = y * _silu(z_ref[orow, :])
            y = y * lax.rsqrt(jnp.mean(y * y, axis=-1, keepdims=True) + NORM_EPS) * nw_ref[...]
            o_ref[orow, :] = y.astype(o_ref.dtype)
            update_state(rows, 0, hf_ref, dt_row, cum_row, tot_row)
            return c
        lax.fori_loop(0, NCH, body, 0, unroll=SSD_UNROLL)

    return init, phase0, phase1


def _ssd_part(nb, seq, xbc, dt, z, cw, cb, dtb, alog, dexp, nw):
    nt = seq // SEQ_TILE
    ns = SSD_CONV_W // LANES
    prev, nxt = _slab_halo_specs_p0(ns, nt, nb * seq)
    in_specs = [_slab_spec_p0(ns, nt), prev, nxt, _both_spec(CHUNK, nt), _fwd_spec(SSD_W, nt)] + \
               [_const_spec(a.shape) for a in (cw, cb, dtb, alog, dexp, nw)]
    state = (SSD_HEADS // 2, SSD_STATE, LANES)
    scratch = [pltpu.VMEM((ns, SEQ_TILE + 2 * SUBLANES, LANES), F32), pltpu.VMEM((ns, seq, LANES), F32),
               pltpu.VMEM(state, F32), pltpu.VMEM(state, F32), pltpu.VMEM((seq // CHUNK,) + state, F32)]
    return "ssd", _ssd_body, (xbc, xbc, xbc, dt, z, cw, cb, dtb, alog, dexp, nw), in_specs, scratch


def _ret_body(rq_ref, rk_ref, rv_ref, rg_ref, cos_ref, sin_ref, dmat_ref, gin_ref, gst_ref, g128_ref, nw_ref,
              o_ref, q_ref, kt_ref, h_ref, hb_ref, *, p, t, tt, nt):
    nh = RET_HEADS
    half = RET_QK // 2
    hw = nh * half

    def init():
        h_ref[...] = jnp.zeros_like(h_ref)

    def rope(x):
        a, b = x[:, :hw], x[:, hw:]
        c, s = cos_ref[...], sin_ref[...]
        return jnp.concatenate([a * c - b * s, b * c + a * s], axis=1)

    def head_rows(kt, h):
        return jnp.concatenate([kt[h * half:(h + 1) * half, :], kt[hw + h * half:hw + (h + 1) * half, :]], axis=0)

    def widen(hc, h):
        n = hc.shape[1]
        sizes = [h * half, hw - half, hw - (h + 1) * half]
        z0, z1, z2 = [jnp.zeros((r, n), hc.dtype) for r in sizes]
        parts = [z0, hc[:half, :], z1, hc[half:, :], z2]
        return jnp.concatenate([q for q in parts if q.shape[0] > 0], axis=0)

    def update_state(gc, vrow, d):
        v = rv_ref[vrow, :]
        kt = kt_ref[gc]
        for h in range(nh):
            ks = (head_rows(kt, h) * gst_ref[d, h]).astype(BF16)
            s = _bdot(ks, v[:, h * RET_V:(h + 1) * RET_V].astype(BF16))
            h_ref[d, h] = g128_ref[h] * h_ref[d, h] + s

    def phase0():
        k = rope(rk_ref[...])
        for c in range(NCH):
            kt_ref[tt * NCH + c] = k[c * CHUNK:(c + 1) * CHUNK, :].T

        def body(i, c):
            ci = NCH - 1 - i
            gc = tt * NCH + ci
            hb_ref[gc] = h_ref[1]
            update_state(gc, _chunk_rows(ci), 1)
            return c
        lax.fori_loop(0, NCH, body, 0, unroll=RET_UNROLL)

    def phase1():
        q_ref[...] = rope(rq_ref[...]) * (RET_QK ** -0.5)
        lane = lax.broadcasted_iota(jnp.int32, (CHUNK, 2 * hw), 1) % hw

        def body(ci, c):
            gc = tt * NCH + ci
            rows = _chunk_rows(ci)
            q = q_ref[rows, :]
            v = rv_ref[rows, :]
            kt = kt_ref[gc].astype(BF16)
            outs = []
            for h in range(nh):
                qh = jnp.where((lane >= h * half) & (lane < (h + 1) * half), q, 0.0).astype(BF16)
                s = _bdot(qh, kt)
                o = _bdot((s * dmat_ref[h]).astype(BF16), v[:, h * RET_V:(h + 1) * RET_V].astype(BF16))
                h_cat = jnp.concatenate([widen(h_ref[0, h], h), widen(hb_ref[gc, h], h)], axis=1).astype(BF16)
                oi = _bdot(qh, h_cat)
                o = o + oi[:, :RET_V] * gin_ref[0, h] + oi[:, RET_V:] * gin_ref[1, h]
                outs.append(_head_norm(o))
            y = jnp.concatenate(outs, axis=1) * nw_ref[...]
            o_ref[rows, :] = (_silu(rg_ref[rows, :]) * y).astype(o_ref.dtype)
            update_state(gc, rows, 0)
            return c
        lax.fori_loop(0, NCH, body, 0, unroll=RET_UNROLL)

    return init, phase0, phase1


def _ret_part(nb, seq, rq, rk, rv, rg, cos, sin, dmat, gin, gst, g128, nw):
    nt = seq // SEQ_TILE
    w = RET_HEADS * RET_QK
    tab_both = pl.BlockSpec((SEQ_TILE, w // 2), lambda b, p, t: (_tile_pos(p, t, nt), 0))
    p0_spec = pl.BlockSpec((SEQ_TILE, w), lambda b, p, t: (b * nt + _p0_tile(p, t, nt), 0))
    in_specs = [_fwd_spec(w, nt), p0_spec, _both_spec(512, nt), _fwd_spec(512, nt), tab_both, tab_both] + \
               [_const_spec(a.shape) for a in (dmat, gin, gst, g128, nw)]
    scratch = [pltpu.VMEM((SEQ_TILE, w), F32), pltpu.VMEM((seq // CHUNK, w, CHUNK), F32),
               pltpu.VMEM((2, RET_HEADS, RET_QK, RET_V), F32),
               pltpu.VMEM((seq // CHUNK, RET_HEADS, RET_QK, RET_V), F32)]
    return "ret", _ret_body, (rq, rk, rv, rg, cos, sin, dmat, gin, gst, g128, nw), in_specs, scratch


def _mlstm_body(qk_ref, qkp_ref, qkn_ref, v_ref, gi_ref, gf_ref, mo_ref, cw_ref, cb_ref, gbi_ref, gbf_ref, nwt_ref,
                o_ref, xpad_ref, qk_s, c_ref, m_ref, c_stash, m_stash, *, p, t, tt, nt):
    nh = ML_HEADS
    nhd = 2 * nh

    def init():
        c_ref[...] = jnp.zeros_like(c_ref)
        m_ref[...] = jnp.zeros_like(m_ref)

    valid_b, valid_f = _tri_masks()
    le = valid_b.astype(F32)
    ge = valid_f.astype(F32)
    rowid8 = lax.broadcasted_iota(jnp.int32, (nhd, CHUNK), 0)
    rowid = lax.broadcasted_iota(jnp.int32, (CHUNK, CHUNK), 0)
    top = rowid < ML_QK
    ones_blk = jnp.ones((ML_NROWS, CHUNK), F32)
    srows = ML_V + ML_NROWS

    def row_gates(ci):
        rows = _chunk_rows(ci)
        gi_col = gi_ref[rows, :] + gbi_ref[...]
        li_row = gi_col.T[0:nhd, :]
        lf_row = -_softplus(-(gf_ref[rows, :] + gbf_ref[...]).T[0:nhd, :])
        b_row = jnp.where(rowid8 < nh, _xdot(lf_row, ge), _xdot(lf_row, le))
        tot = jnp.sum(lf_row, axis=1, keepdims=True)
        return gi_col, li_row, b_row, tot

    def update_state(srow, vrow, d, li_row, b_row, tot):
        w_row = tot - b_row + li_row
        m_loc = jnp.max(w_row, axis=1, keepdims=True)
        e_row = jnp.exp(w_row - m_loc)
        m_prev = m_ref[...]
        m_new = jnp.maximum(tot + m_prev, m_loc)
        dec = jnp.exp(tot + m_prev - m_new)
        inj = jnp.exp(m_loc - m_new)
        v = v_ref[vrow, :]
        for pr in range(nh // 2):
            k_pair = (qk_s[nh // 2 + pr, srow, :] * (ML_QK ** -0.5)).astype(BF16)
            for hl in range(2):
                h = 2 * pr + hl
                hd = d * nh + h
                e_b = jnp.broadcast_to(e_row[hd:hd + 1, :], (srows, CHUNK))
                vt = v[:, h * ML_V:(h + 1) * ML_V].T
                lhs = (jnp.concatenate([vt, ones_blk], axis=0) * e_b).astype(BF16)
                s = _bdot(lhs, k_pair)
                c_ref[hd] = (jnp.broadcast_to(dec[hd:hd + 1, :], (srows, CHUNK)) * c_ref[hd]
                             + jnp.broadcast_to(inj[hd:hd + 1, :], (srows, CHUNK)) * s)
        keep = (rowid8 >= nh) if d == 0 else (rowid8 < nh)
        m_ref[...] = jnp.where(keep, m_prev, m_new)

    def direction_out(d, qts, k_pairs, vts, r_col, b_row, c_aug, m_prev):
        mask = valid_f if d == 0 else valid_b
        outs = []
        for h in range(nh):
            hd = d * nh + h
            rj = jnp.where(mask, r_col[:, hd:hd + 1], -jnp.inf)
            u = jnp.maximum(m_prev[hd:hd + 1, :], jnp.max(rj, axis=0, keepdims=True))
            st = _bdot(k_pairs[h // 2], qts[h])
            pt = (st * jnp.exp(rj - u)).astype(BF16)
            lhs = jnp.concatenate([vts[h], ones_blk], axis=0).astype(BF16)
            nd = _bdot(lhs, pt) + jnp.exp(m_prev[hd:hd + 1, :] - u) * _bdot(c_aug[h].astype(BF16), qts[h])
            floor = jnp.exp(-(b_row[hd:hd + 1, :] + u))
            den = jnp.maximum(jnp.abs(nd[ML_V:ML_V + 1, :]), floor)
            outs.append(nd[0:ML_V, :] / den)
        return outs

    def phase0():
        _conv_to_cache(qk_ref, qkp_ref, qkn_ref, xpad_ref, qk_s, cw_ref, cb_ref, tt, nt, _silu)

        def body(i, c):
            ci = NCH - 1 - i
            gc = tt * NCH + ci
            c_stash[gc] = c_ref[nh:nhd]
            m_stash[gc] = m_ref[...]
            _, li_row, b_row, tot = row_gates(ci)
            update_state(_seq_rows(tt, ci), _chunk_rows(ci), 1, li_row, b_row, tot)
            return c
        lax.fori_loop(0, NCH, body, 0, unroll=ML_UNROLL)

    def phase1():
        def body(ci, c):
            gc = tt * NCH + ci
            srow = _seq_rows(tt, ci)
            rows = _chunk_rows(ci)
            gi_col, li_row, b_row, tot = row_gates(ci)
            r_col = gi_col - _rows_to_cols(b_row)
            qts, k_pairs, vts = [], [], []
            v = v_ref[rows, :]
            for pr in range(nh // 2):
                qt = qk_s[pr, srow, :].T
                qts.append(jnp.where(top, qt, 0.0).astype(BF16))
                qts.append(jnp.where(top, 0.0, qt).astype(BF16))
                k_pairs.append((qk_s[nh // 2 + pr, srow, :] * (ML_QK ** -0.5)).astype(BF16))
            for h in range(nh):
                vts.append(v[:, h * ML_V:(h + 1) * ML_V].T)
            hf = direction_out(0, qts, k_pairs, vts, r_col, b_row, c_ref[0:nh], m_ref[...])
            hb = direction_out(1, qts, k_pairs, vts, r_col, b_row, c_stash[gc], m_stash[gc])
            ys = []
            for h in range(nh):
                x = hf[h] + hb[h]
                mu = jnp.mean(x, axis=0, keepdims=True)
                xc = x - mu
                var = jnp.mean(xc * xc, axis=0, keepdims=True)
                ys.append((xc * lax.rsqrt(var + NORM_EPS) * nwt_ref[h]).T)
            y = jnp.concatenate(ys, axis=1)
            o_ref[rows, :] = (_sigmoid(mo_ref[rows, :]) * y).astype(o_ref.dtype)
            update_state(srow, rows, 0, li_row, b_row, tot)
            return c
        lax.fori_loop(0, NCH, body, 0, unroll=ML_UNROLL)

    return init, phase0, phase1


def _mlstm_part(nb, seq, mqk, mv, mgi, mgf, mo, cw, cb, gbi, gbf, nwt):
    nt = seq // SEQ_TILE
    ns = 2 * ML_HEADS * ML_QK // LANES
    prev, nxt = _slab_halo_specs_p0(ns, nt, nb * seq)
    in_specs = [_slab_spec_p0(ns, nt), prev, nxt, _both_spec(512, nt), _both_spec(CHUNK, nt), _both_spec(CHUNK, nt),
                _fwd_spec(512, nt)] + [_const_spec(a.shape) for a in (cw, cb, gbi, gbf, nwt)]
    scratch = [pltpu.VMEM((ns, SEQ_TILE + 2 * SUBLANES, LANES), F32), pltpu.VMEM((ns, seq, LANES), F32),
               pltpu.VMEM((2 * ML_HEADS, ML_V + ML_NROWS, CHUNK), F32),
               pltpu.VMEM((2 * ML_HEADS, CHUNK), F32),
               pltpu.VMEM((seq // CHUNK, ML_HEADS, ML_V + ML_NROWS, CHUNK), F32),
               pltpu.VMEM((seq // CHUNK, 2 * ML_HEADS, CHUNK), F32)]
    return "mlstm", _mlstm_body, (mqk, mqk, mqk, mv, mgi, mgf, mo, cw, cb, gbi, gbf, nwt), in_specs, scratch


def _pad_cols(a, width):
    return jnp.pad(a, [(0, 0)] * (a.ndim - 1) + [(0, width - a.shape[-1])])


def _prepare(w_in, lru_conv_w, lru_conv_b, lru_gate_w, lru_gate_b, lru_lambda, ssd_conv_w, ssd_conv_b, ssd_dt_bias,
             ssd_a_log, ssd_d, ssd_norm_w, ret_norm_w, mlstm_conv_w, mlstm_conv_b, mlstm_gate_b, mlstm_norm_w,
             w_branch, w_out, w_ffn_in, w_ffn_out, ln_g, ln_b):
    sp = {}
    off = 0
    for name, w in (("xa", 512), ("ga", 512), ("z", 512), ("xbc", 768), ("dt", 16), ("rq", 256), ("rk", 256),
                    ("rv", 512), ("rg", 512), ("mqk", 512), ("mv", 512), ("mo", 512), ("mg", 16), ("mix", 4096)):
        sp[name] = w_in[:, :, off:off + w]
        off += w
    mg = sp["mg"].reshape(DEPTH, D_MODEL, 2, 2, ML_HEADS)

    def halves_major(w):
        w = w.reshape(DEPTH, D_MODEL, RET_HEADS, 2, RET_QK // 2)
        return jnp.moveaxis(w, 3, 2).reshape(DEPTH, D_MODEL, RET_HEADS * RET_QK)

    cols = {"dt": _pad_cols(sp["dt"], CHUNK), "rq": halves_major(sp["rq"]), "rk": halves_major(sp["rk"]),
            "mgi": _pad_cols(mg[:, :, :, 0, :].reshape(DEPTH, D_MODEL, 8), CHUNK),
            "mgf": _pad_cols(mg[:, :, :, 1, :].reshape(DEPTH, D_MODEL, 8), CHUNK)}
    wp = jnp.concatenate([cols[n] if n in cols else sp[n] for n, _ in PROJ_OUTS], axis=-1).astype(BF16)
    gbi = _pad_cols(mlstm_gate_b[:, :, 0, :].reshape(DEPTH, 1, 8), CHUNK)
    gbf = _pad_cols(mlstm_gate_b[:, :, 1, :].reshape(DEPTH, 1, 8), CHUNK)
    ml_nwt = jnp.broadcast_to(mlstm_norm_w.reshape(DEPTH, ML_HEADS, ML_V, 1), (DEPTH, ML_HEADS, ML_V, CHUNK))

    eye = jnp.eye(LRU_BLOCKS, dtype=F32)
    dense = jnp.einsum("ldghij,hk->ldghikj", lru_gate_w, eye).reshape(DEPTH, 2, 2, LRU_W, LRU_W)
    lru_wg = jnp.concatenate([dense[:, :, 0], dense[:, :, 1]], axis=-1)

    return dict(
        wp=wp, wmix=sp["mix"].astype(BF16),
        ffn_i=w_ffn_in.astype(BF16), ffn_d=w_ffn_out.astype(BF16),
        ln_g=ln_g[:, :, None, :], ln_b=ln_b[:, :, None, :],
        wb=w_branch.astype(BF16), wo=w_out.astype(BF16),
        lru_cw=lru_conv_w, lru_cb=lru_conv_b[:, None, :], lru_wg=(0.5 * lru_wg).astype(BF16),
        lru_gb=0.5 * lru_gate_b.reshape(DEPTH, 2, 1, 2 * LRU_W), lru_lam=lru_lambda[:, :, None, :],
        ssd_cw=ssd_conv_w, ssd_cb=ssd_conv_b[:, None, :],
        ssd_dtb=_pad_cols(ssd_dt_bias.reshape(DEPTH, 1, 16), CHUNK),
        ssd_alog=_pad_cols(ssd_a_log.reshape(DEPTH, 1, 16), CHUNK),
        ssd_dexp=jnp.repeat(ssd_d, SSD_HEAD_DIM, axis=-1)[:, None, :], ssd_nw=ssd_norm_w[:, None, :],
        ret_nw=ret_norm_w[:, None, :],
        ml_cw=mlstm_conv_w, ml_cb=mlstm_conv_b[:, None, :], ml_gbi=gbi, ml_gbf=gbf, ml_nwt=ml_nwt,
    )


def _ret_tables(seq):
    d = RET_QK
    inv = ROPE_BASE ** (-jnp.arange(0, d, 2, dtype=F32) / d)
    ang = jnp.arange(seq, dtype=F32)[:, None] * inv[None, :]
    cos = jnp.tile(jnp.cos(ang), (1, RET_HEADS))
    sin = jnp.tile(jnp.sin(ang), (1, RET_HEADS))
    lg = jnp.log1p(-jnp.exp2(-5.0 - jnp.arange(RET_HEADS, dtype=F32)))[:, None, None]
    pos = jnp.arange(CHUNK, dtype=F32)
    dmat = jnp.exp(lg * jnp.abs(pos[:, None] - pos[None, :])[None])
    ones = jnp.ones((1, 1, RET_V), F32)
    gin = jnp.stack([jnp.exp(lg * (pos + 1.0)[None, :, None]) * ones,
                     jnp.exp(lg * (CHUNK - pos)[None, :, None]) * ones])
    ones = jnp.ones((1, RET_QK, 1), F32)
    gst = jnp.stack([jnp.exp(lg * (CHUNK - 1.0 - pos)[None, None, :]) * ones,
                     jnp.exp(lg * pos[None, None, :]) * ones])
    g128 = jnp.exp(lg * float(CHUNK)) * jnp.ones((1, RET_QK, RET_V), F32)
    return cos, sin, dmat, gin, gst, g128


def _trunk(x3, pw):
    nb, seq, _ = x3.shape
    x = x3.reshape(nb * seq, D_MODEL)
    tabs = _ret_tables(seq)
    for l in range(DEPTH):
        x = _ffn_call(x, pw["ffn_i"][l, 0], pw["ffn_d"][l, 0], pw["ln_g"][l, 0], pw["ln_b"][l, 0])
        pr = dict(zip([n for n, _ in PROJ_OUTS], _proj_call(x, pw["wp"][l])))
        parts = dict(
            lru=_lru_part(nb, seq, pr["xa"], pr["ga"], pw["lru_cw"][l], pw["lru_cb"][l], pw["lru_wg"][l],
                          pw["lru_gb"][l], pw["lru_lam"][l]),
            ssd=_ssd_part(nb, seq, pr["xbc"], pr["dt"], pr["z"], pw["ssd_cw"][l], pw["ssd_cb"][l], pw["ssd_dtb"][l],
                          pw["ssd_alog"][l], pw["ssd_dexp"][l], pw["ssd_nw"][l]),
            ret=_ret_part(nb, seq, pr["rq"], pr["rk"], pr["rv"], pr["rg"], *tabs, pw["ret_nw"][l]),
            mlstm=_mlstm_part(nb, seq, pr["mqk"], pr["mv"], pr["mgi"], pr["mgf"], pr["mo"], pw["ml_cw"][l],
                              pw["ml_cb"][l], pw["ml_gbi"][l], pw["ml_gbf"][l], pw["ml_nwt"][l]))
        ys = {}
        for group in MIXER_GROUPS:
            ys.update(zip(group, _mixers_call(nb, seq, [parts[n] for n in group])))
        x = _merge_ffn_call(x, (ys["lru"], ys["ssd"], ys["ret"], ys["mlstm"]), pw["wmix"][l], pw["wb"][l], pw["wo"][l],
                            pw["ln_g"][l, 1], pw["ln_b"][l, 1], pw["ffn_i"][l, 1], pw["ffn_d"][l, 1], pw["ln_g"][l, 2],
                            pw["ln_b"][l, 2])
    return x.reshape(nb, seq, D_MODEL)


def kernel(x_prompt, x_sample, w_in, lru_conv_w, lru_conv_b, lru_gate_w, lru_gate_b, lru_lambda, ssd_conv_w,
           ssd_conv_b, ssd_dt_bias, ssd_a_log, ssd_d, ssd_norm_w, ret_norm_w, mlstm_conv_w, mlstm_conv_b,
           mlstm_gate_b, mlstm_norm_w, w_branch, w_out, w_ffn_in, w_ffn_out, ln_g, ln_b):
    pw = _prepare(w_in, lru_conv_w, lru_conv_b, lru_gate_w, lru_gate_b, lru_lambda, ssd_conv_w, ssd_conv_b,
                  ssd_dt_bias, ssd_a_log, ssd_d, ssd_norm_w, ret_norm_w, mlstm_conv_w, mlstm_conv_b, mlstm_gate_b,
                  mlstm_norm_w, w_branch, w_out, w_ffn_in, w_ffn_out, ln_g, ln_b)
    return _trunk(x_prompt, pw), _trunk(x_sample, pw)
```

```python
import functools

import jax
import jax.numpy as jnp
from jax import lax
from jax.experimental import pallas as pl
from jax.experimental.pallas import tpu as pltpu

F32 = jnp.float32
BF16 = jnp.bfloat16
HIGHEST = lax.Precision.HIGHEST

D_MODEL = 1024
DEPTH = 4
D_FF = 2816
CHUNK = 128
LRU_W = 512
LRU_BLOCKS = 8
LRU_C = 8.0
SSD_HEADS = 8
SSD_HEAD_DIM = 64
SSD_W = 512
SSD_STATE = 64
SSD_GROUPS = 2
SSD_CONV_W = SSD_W + 2 * SSD_GROUPS * SSD_STATE
RET_HEADS = 4
RET_QK = 64
RET_V = 128
ROPE_BASE = 10000.0
ML_HEADS = 4
ML_QK = 64
ML_V = 128
DN_ALPHA = (2.0 * DEPTH) ** 0.25
NORM_EPS = 1e-5
LOG2E = 1.4426950408889634
TINY = 1.1754944e-38

VMEM_LIMIT_BYTES = 56 * 1024 * 1024
FF_CHUNK = 256
TOKEN_TILE = 512
SEQ_TILE = 512
NCH = SEQ_TILE // CHUNK
SUBLANES = 8
LANES = 128
ML_NROWS = 16
MIXER_GROUPS = ((("ret", "lru"), True), (("mlstm", "ssd"), False))

PROJ_OUTS = (("xa", 512), ("ga", 512), ("z", 512), ("xbc", 768), ("dt", 128), ("rq", 256), ("rk", 256),
             ("rv", 512), ("rg", 512), ("mqk", 512), ("mv", 512), ("mo", 512), ("mgi", 128), ("mgf", 128))
SLAB_OUTS = ("xa", "ga", "xbc", "mqk")


def _cparams(sem):
    return pltpu.CompilerParams(dimension_semantics=sem, vmem_limit_bytes=VMEM_LIMIT_BYTES)


def _const_spec(shape):
    nd = len(shape)
    return pl.BlockSpec(shape, lambda *_: (0,) * nd, pipeline_mode=pl.Buffered(1))


def _bdot(a, b):
    return jnp.dot(a, b, preferred_element_type=F32)


def _xdot(a, b):
    return jnp.dot(a, b, preferred_element_type=F32, precision=HIGHEST)


def _sigmoid(x):
    return 0.5 * jnp.tanh(0.5 * x) + 0.5


def _silu(x):
    return x * _sigmoid(x)


def _softplus(x):
    return jnp.maximum(x, 0.0) + jnp.log1p(jnp.exp(-jnp.abs(x)))


def _layer_norm(y, g, b):
    mu = jnp.mean(y, axis=-1, keepdims=True)
    yc = y - mu
    var = jnp.mean(yc * yc, axis=-1, keepdims=True)
    return yc * lax.rsqrt(var + NORM_EPS) * g + b


def _ffn_kernel(x_ref, wi_ref, wd_ref, g_ref, b_ref, o_ref, acc_ref):
    x = x_ref[...]
    xb = x.astype(BF16)
    nf = D_FF // FF_CHUNK
    for f in range(nf):
        gate = _bdot(xb, wi_ref[:, f * FF_CHUNK:(f + 1) * FF_CHUNK])
        up = _bdot(xb, wi_ref[:, D_FF + f * FF_CHUNK:D_FF + (f + 1) * FF_CHUNK])
        h = (_silu(gate) * up).astype(BF16)
        part = _bdot(h, wd_ref[f * FF_CHUNK:(f + 1) * FF_CHUNK, :])
        if f == 0:
            acc_ref[...] = part
        else:
            acc_ref[...] += part
    y = DN_ALPHA * x + 0.5 * acc_ref[...]
    o_ref[...] = _layer_norm(y, g_ref[...], b_ref[...])


def _ffn_call(x, wi, wd, g, b):
    m = x.shape[0]
    tm = TOKEN_TILE
    return pl.pallas_call(
        _ffn_kernel,
        out_shape=jax.ShapeDtypeStruct((m, D_MODEL), F32),
        grid=(m // tm,),
        in_specs=[pl.BlockSpec((tm, D_MODEL), lambda i: (i, 0)),
                  _const_spec(wi.shape), _const_spec(wd.shape), _const_spec(g.shape), _const_spec(b.shape)],
        out_specs=pl.BlockSpec((tm, D_MODEL), lambda i: (i, 0)),
        scratch_shapes=[pltpu.VMEM((tm, D_MODEL), F32)],
        compiler_params=_cparams(("parallel",)),
        name="ffn",
    )(x, wi, wd, g, b)


def _proj_kernel(x_ref, w_ref, *o_refs):
    xb = x_ref[...].astype(BF16)
    off = 0
    for o_ref in o_refs:
        if len(o_ref.shape) == 3:
            w = o_ref.shape[0] * LANES
            res = _bdot(xb, w_ref[:, off:off + w])
            for k in range(o_ref.shape[0]):
                o_ref[k] = res[:, k * LANES:(k + 1) * LANES]
            off += w
        else:
            w = o_ref.shape[1]
            o_ref[...] = _bdot(xb, w_ref[:, off:off + w])
            off += w


def _proj_call(x, wp):
    m = x.shape[0]
    tm = TOKEN_TILE
    shapes, specs = [], []
    for name, w in PROJ_OUTS:
        if name in SLAB_OUTS:
            shapes.append(jax.ShapeDtypeStruct((w // LANES, m, LANES), F32))
            specs.append(pl.BlockSpec((w // LANES, tm, LANES), lambda i: (0, i, 0)))
        else:
            shapes.append(jax.ShapeDtypeStruct((m, w), F32))
            specs.append(pl.BlockSpec((tm, w), lambda i: (i, 0)))
    return pl.pallas_call(
        _proj_kernel,
        out_shape=shapes,
        grid=(m // tm,),
        in_specs=[pl.BlockSpec((tm, D_MODEL), lambda i: (i, 0)), _const_spec(wp.shape)],
        out_specs=specs,
        compiler_params=_cparams(("parallel",)),
        name="proj",
    )(x, wp)


def _merge_ffn_kernel(x_ref, ya_ref, yb_ref, yc_ref, yd_ref, wg_ref, wb_ref, wo_ref, g1_ref, b1_ref,
                      wi_ref, wd_ref, g2_ref, b2_ref, o_ref, acc_ref):
    x = x_ref[...]
    xb = x.astype(BF16)
    merged = None
    for i, y_ref in enumerate((ya_ref, yb_ref, yc_ref, yd_ref)):
        gate = _sigmoid(_bdot(xb, wg_ref[:, i * D_MODEL:(i + 1) * D_MODEL]))
        term = gate * _bdot(y_ref[...], wb_ref[i])
        merged = term if merged is None else merged + term
    out = _bdot(merged.astype(BF16), wo_ref[...])
    x2 = _layer_norm(DN_ALPHA * x + out, g1_ref[...], b1_ref[...])
    x2b = x2.astype(BF16)
    for f in range(D_FF // FF_CHUNK):
        gate = _bdot(x2b, wi_ref[:, f * FF_CHUNK:(f + 1) * FF_CHUNK])
        up = _bdot(x2b, wi_ref[:, D_FF + f * FF_CHUNK:D_FF + (f + 1) * FF_CHUNK])
        h = (_silu(gate) * up).astype(BF16)
        part = _bdot(h, wd_ref[f * FF_CHUNK:(f + 1) * FF_CHUNK, :])
        if f == 0:
            acc_ref[...] = part
        else:
            acc_ref[...] += part
    o_ref[...] = _layer_norm(DN_ALPHA * x2 + 0.5 * acc_ref[...], g2_ref[...], b2_ref[...])


def _merge_ffn_call(x, ys, wg, wb, wo, g1, b1, wi, wd, g2, b2):
    m = x.shape[0]
    tm = TOKEN_TILE
    consts = (wg, wb, wo, g1, b1, wi, wd, g2, b2)
    return pl.pallas_call(
        _merge_ffn_kernel,
        out_shape=jax.ShapeDtypeStruct((m, D_MODEL), F32),
        grid=(m // tm,),
        in_specs=[pl.BlockSpec((tm, D_MODEL), lambda i: (i, 0))]
                 + [pl.BlockSpec((tm, 512), lambda i: (i, 0)) for _ in range(4)]
                 + [_const_spec(c.shape) for c in consts],
        out_specs=pl.BlockSpec((tm, D_MODEL), lambda i: (i, 0)),
        scratch_shapes=[pltpu.VMEM((tm, D_MODEL), F32)],
        compiler_params=_cparams(("parallel",)),
        name="merge_ffn",
    )(x, *ys, *consts)


def _tile_pos(p, t, nt):
    return t + (1 - p) * (nt - 1 - 2 * t)


def _p0_tile(p, t, nt):
    return (1 - p) * (nt - 1 - t)


def _both_spec(w, nt):
    return pl.BlockSpec((SEQ_TILE, w), lambda b, p, t: (b * nt + _tile_pos(p, t, nt), 0))


def _fwd_spec(w, nt):
    return pl.BlockSpec((SEQ_TILE, w), lambda b, p, t: (b * nt + p * t, 0))


def _slab_spec_p0(ns, nt):
    return pl.BlockSpec((ns, SEQ_TILE, LANES), lambda b, p, t: (0, b * nt + _p0_tile(p, t, nt), 0))


def _slab_spec_p1(ns, nt):
    return pl.BlockSpec((ns, SEQ_TILE, LANES), lambda b, p, t: (0, b * nt + p * t, 0))


def _slab_halo_specs_p0(ns, nt, n_rows):
    per = SEQ_TILE // SUBLANES
    last = n_rows // SUBLANES - 1
    prev = pl.BlockSpec((ns, SUBLANES, LANES),
                        lambda b, p, t: (0, jnp.maximum((b * nt + _p0_tile(p, t, nt)) * per - 1, 0), 0))
    nxt = pl.BlockSpec((ns, SUBLANES, LANES),
                       lambda b, p, t: (0, jnp.minimum((b * nt + _p0_tile(p, t, nt) + 1) * per, last), 0))
    return prev, nxt


def _conv_to_cache(x_ref, prev_ref, next_ref, xpad_ref, cache_ref, w_ref, b_ref, tt, nt, act):
    ns = x_ref.shape[0]
    for k in range(ns):
        xpad_ref[k, 0:SUBLANES, :] = jnp.where(tt == 0, 0.0, prev_ref[k])
        xpad_ref[k, SUBLANES:SUBLANES + SEQ_TILE, :] = x_ref[k]
        xpad_ref[k, SUBLANES + SEQ_TILE:2 * SUBLANES + SEQ_TILE, :] = jnp.where(tt == nt - 1, 0.0, next_ref[k])
    base = tt * SEQ_TILE
    for k in range(ns):
        w = w_ref[:, k * LANES:(k + 1) * LANES]
        bias = b_ref[:, k * LANES:(k + 1) * LANES]
        for r in range(0, SEQ_TILE, CHUNK):
            taps = [xpad_ref[k, SUBLANES - 2 + j + r:SUBLANES - 2 + j + r + CHUNK, :] for j in range(4)]
            y = w[0:1, :] * taps[0] + w[1:2, :] * taps[1] + w[2:3, :] * taps[2] + w[3:4, :] * taps[3] + bias
            cache_ref[k, pl.ds(pl.multiple_of(base + r, CHUNK), CHUNK), :] = act(y)


def _mixers_kernel(*refs, nt, layout, interleave):
    p = pl.program_id(1)
    t = pl.program_id(2)
    tt = _tile_pos(p, t, nt)
    n_in = sum(ni for _, ni, _ in layout)
    n_out = len(layout)
    parts, i_in, i_sc = [], 0, n_in + n_out
    for k, (body, ni, ns) in enumerate(layout):
        parts.append(body(*refs[i_in:i_in + ni], refs[n_in + k], *refs[i_sc:i_sc + ns], p=p, t=t, tt=tt, nt=nt))
        i_in += ni
        i_sc += ns

    @pl.when(t == 0)
    def _():
        for init, _, _ in parts:
            init()

    def run(chunk_lists):
        if interleave:
            order = [thunk for step in zip(*chunk_lists) for thunk in step]
        else:
            order = [thunk for chunks in chunk_lists for thunk in chunks]
        for thunk in order:
            thunk()

    @pl.when(p == 0)
    def _():
        run([phase0() for _, phase0, _ in parts])

    @pl.when(p == 1)
    def _():
        run([phase1() for _, _, phase1 in parts])


def _mixers_call(nb, seq, parts, interleave):
    nt = seq // SEQ_TILE
    layout = tuple((body, len(ops), len(scr)) for _, body, ops, _, scr in parts)
    out_spec = pl.BlockSpec((SEQ_TILE, 512), lambda b, p, t: (b * nt + p * t, 0))
    return pl.pallas_call(
        functools.partial(_mixers_kernel, nt=nt, layout=layout, interleave=interleave),
        out_shape=[jax.ShapeDtypeStruct((nb * seq, 512), BF16) for _ in parts],
        grid=(nb, 2, nt),
        in_specs=[sp for _, _, _, specs, _ in parts for sp in specs],
        out_specs=[out_spec for _ in parts],
        scratch_shapes=[sc for _, _, _, _, scr in parts for sc in scr],
        compiler_params=_cparams(("arbitrary", "arbitrary", "arbitrary")),
        name="_".join(name for name, _, _, _, _ in parts),
    )(*[op for _, _, ops, _, _ in parts for op in ops])


def _tri_masks():
    ii = lax.broadcasted_iota(jnp.int32, (CHUNK, CHUNK), 0)
    jj = lax.broadcasted_iota(jnp.int32, (CHUNK, CHUNK), 1)
    return jj <= ii, jj >= ii


def _rows_to_cols(x):
    r = x.shape[0]
    return jnp.concatenate([x, jnp.zeros((CHUNK - r, x.shape[1]), x.dtype)], axis=0).T


def _chunk_rows(ci):
    return pl.ds(pl.multiple_of(ci * CHUNK, CHUNK), CHUNK)


def _seq_rows(tt, ci):
    return pl.ds(pl.multiple_of(tt * SEQ_TILE + ci * CHUNK, CHUNK), CHUNK)


def _head_norm(x):
    mu = jnp.mean(x, axis=-1, keepdims=True)
    xc = x - mu
    var = jnp.mean(xc * xc, axis=-1, keepdims=True)
    return xc * lax.rsqrt(var + NORM_EPS)


def _scan_rows(a, b, reverse):
    n = a.shape[0]
    row = lax.broadcasted_iota(jnp.int32, a.shape, 0)
    s = 1
    while s < n:
        if reverse:
            a_sh = pltpu.roll(a, n - s, 0)
            b_sh = pltpu.roll(b, n - s, 0)
            valid = row < n - s
        else:
            a_sh = pltpu.roll(a, s, 0)
            b_sh = pltpu.roll(b, s, 0)
            valid = row >= s
        b = a * jnp.where(valid, b_sh, 0.0) + b
        a = a * jnp.where(valid, a_sh, 1.0)
        s *= 2
    return a, b


LRU_GROUPS = CHUNK // SUBLANES


def _load_interleaved(ref, row0):
    ns = ref.shape[0]
    return jnp.concatenate(
        [jnp.concatenate([ref[k, pl.ds(row0 + s, LRU_GROUPS, stride=SUBLANES), :] for k in range(ns)], axis=1)
         for s in range(SUBLANES)], axis=0)


def _lru_body(xa_ref, xap_ref, xan_ref, ga_ref, cw_ref, cb_ref, wg_ref, gb_ref, lam_ref, o_ref,
              xpad_ref, xc_ref, carry_ref, hb_ref, y_ref, *, p, t, tt, nt):
    ng = LRU_GROUPS

    def init():
        carry_ref[...] = jnp.zeros_like(carry_ref)

    def direction(ci, d):
        xc = _load_interleaved(xc_ref, pl.multiple_of(tt * SEQ_TILE + ci * CHUNK, CHUNK))
        th = jnp.tanh(_bdot(xc.astype(BF16), wg_ref[d]) + gb_ref[d])
        c_half = (-0.5 * LRU_C) * _softplus(-lam_ref[d])
        log_a = c_half * th[:, :LRU_W] + c_half
        a = jnp.exp(log_a)
        v = -jnp.tanh(log_a) * (a * a + 1.0)
        ix = (th[:, LRU_W:] + 1.0) * (0.5 * xc)
        b = (v * lax.rsqrt(jnp.maximum(v, TINY))) * ix
        sa = [a[s * ng:(s + 1) * ng, :] for s in range(SUBLANES)]
        sb = [b[s * ng:(s + 1) * ng, :] for s in range(SUBLANES)]
        order = range(SUBLANES - 2, -1, -1) if d == 1 else range(1, SUBLANES)
        for s in order:
            q = s + 1 if d == 1 else s - 1
            sb[s] = sa[s] * sb[q] + sb[s]
            sa[s] = sa[s] * sa[q]
        edge = 0 if d == 1 else SUBLANES - 1
        ga, gb = _scan_rows(sa[edge], sb[edge], reverse=(d == 1))
        row = lax.broadcasted_iota(jnp.int32, ga.shape, 0)
        if d == 1:
            ga = jnp.where(row == ng - 1, 1.0, pltpu.roll(ga, ng - 1, 0))
            gb = jnp.where(row == ng - 1, 0.0, pltpu.roll(gb, ng - 1, 0))
        else:
            ga = jnp.where(row == 0, 1.0, pltpu.roll(ga, 1, 0))
            gb = jnp.where(row == 0, 0.0, pltpu.roll(gb, 1, 0))
        h_in = ga * carry_ref[...] + gb
        hs = [sa[s] * h_in + sb[s] for s in range(SUBLANES)]
        carry_ref[...] = hs[0][0:1, :] if d == 1 else hs[SUBLANES - 1][ng - 1:ng, :]
        return jnp.concatenate(hs, axis=0)

    def phase0():
        _conv_to_cache(xa_ref, xap_ref, xan_ref, xpad_ref, xc_ref, cw_ref, cb_ref, tt, nt, lambda v: v)

        def body(i, c):
            ci = NCH - 1 - i
            hb_ref[tt * NCH + ci] = direction(ci, 1)
            return c
        return [functools.partial(body, i, 0) for i in range(NCH)]

    def phase1():
        def body(ci, c):
            h = direction(ci, 0) + hb_ref[tt * NCH + ci]
            y = jax.nn.gelu(_load_interleaved(ga_ref, pl.multiple_of(ci * CHUNK, CHUNK))) * h
            for s in range(SUBLANES):
                for k in range(LRU_W // LANES):
                    y_ref[k, pl.ds(s, ng, stride=SUBLANES), :] = y[s * ng:(s + 1) * ng, k * LANES:(k + 1) * LANES]
            o_ref[_chunk_rows(ci), :] = jnp.concatenate(
                [y_ref[k] for k in range(LRU_W // LANES)], axis=1).astype(o_ref.dtype)
            return c
        return [functools.partial(body, i, 0) for i in range(NCH)]

    return init, phase0, phase1


def _lru_part(nb, seq, xa, ga, cw, cb, wg, gb, lam):
    nt = seq // SEQ_TILE
    ns = LRU_W // LANES
    prev, nxt = _slab_halo_specs_p0(ns, nt, nb * seq)
    in_specs = [_slab_spec_p0(ns, nt), prev, nxt, _slab_spec_p1(ns, nt),
                _const_spec(cw.shape), _const_spec(cb.shape), _const_spec(wg.shape),
                _const_spec(gb.shape), _const_spec(lam.shape)]
    scratch = [pltpu.VMEM((ns, SEQ_TILE + 2 * SUBLANES, LANES), F32), pltpu.VMEM((ns, seq, LANES), F32),
               pltpu.VMEM((1, LRU_W), F32), pltpu.VMEM((seq // CHUNK, CHUNK, LRU_W), F32),
               pltpu.VMEM((ns, CHUNK, LANES), F32)]
    return "lru", _lru_body, (xa, xa, xa, ga, cw, cb, wg, gb, lam), in_specs, scratch


def _ssd_body(xbc_ref, xbcp_ref, xbcn_ref, dt_ref, z_ref, cw_ref, cb_ref, dtb_ref, alog_ref, dexp_ref, nw_ref,
              o_ref, xpad_ref, xc_ref, hf_ref, hbk_ref, stash_ref, *, p, t, tt, nt):
    nh = SSD_HEADS
    nhd = 2 * nh
    npair = nh // 2
    hpg = nh // SSD_GROUPS

    def init():
        hf_ref[...] = jnp.zeros_like(hf_ref)
        hbk_ref[...] = jnp.zeros_like(hbk_ref)

    lower, upper = _tri_masks()
    le = lower.astype(F32)
    ge = upper.astype(F32)
    lane = lax.broadcasted_iota(jnp.int32, (1, CHUNK), 1)
    neg_a = jnp.where(lane < nhd, -LOG2E * jnp.exp(alog_ref[...]), 0.0)
    rowid = lax.broadcasted_iota(jnp.int32, (nhd, CHUNK), 0)
    colid = lax.broadcasted_iota(jnp.int32, (CHUNK, CHUNK), 1)
    left = colid < SSD_HEAD_DIM

    def row_gates(ci):
        dt_col = _softplus(dt_ref[_chunk_rows(ci), :] + dtb_ref[...])
        la_col = dt_col * neg_a
        dt_row = dt_col.T[0:nhd, :]
        la_row = la_col.T[0:nhd, :]
        cum_row = jnp.where(rowid < nh, _xdot(la_row, ge), _xdot(la_row, le))
        tot_row = jnp.sum(la_row, axis=1, keepdims=True)
        return dt_row, cum_row, tot_row

    def update_state(rows, d, h_ref, dt_row, cum_row, tot_row):
        bm_t = xc_ref[4, rows, :].T
        w_row = jnp.exp2(tot_row - cum_row) * dt_row
        dec = jnp.broadcast_to(jnp.exp2(tot_row), (nhd, CHUNK))
        shp = (SSD_STATE, LANES)
        keep = lax.broadcasted_iota(jnp.int32, shp, 1) < SSD_HEAD_DIM
        for pr in range(npair):
            g = (2 * pr) // hpg
            xs_p = xc_ref[pr, rows, :].astype(BF16)
            bg = bm_t[g * SSD_STATE:(g + 1) * SSD_STATE, :]
            hd0 = d * nh + 2 * pr
            s0 = _bdot((bg * w_row[hd0:hd0 + 1, :]).astype(BF16), xs_p)
            s1 = _bdot((bg * w_row[hd0 + 1:hd0 + 2, :]).astype(BF16), xs_p)
            dec_p = jnp.where(keep, jnp.broadcast_to(dec[hd0:hd0 + 1, :], shp),
                              jnp.broadcast_to(dec[hd0 + 1:hd0 + 2, :], shp))
            h_ref[pr] = dec_p * h_ref[pr] + jnp.where(keep, s0, s1)

    def phase0():
        _conv_to_cache(xbc_ref, xbcp_ref, xbcn_ref, xpad_ref, xc_ref, cw_ref, cb_ref, tt, nt, _silu)

        def body(i, c):
            ci = NCH - 1 - i
            stash_ref[tt * NCH + ci] = hbk_ref[...]
            dt_row, cum_row, tot_row = row_gates(ci)
            update_state(_seq_rows(tt, ci), 1, hbk_ref, dt_row, cum_row, tot_row)
            return c
        return [functools.partial(body, i, 0) for i in range(NCH)]

    def phase1():
        def body(ci, c):
            gc = tt * NCH + ci
            rows = _seq_rows(tt, ci)
            orow = _chunk_rows(ci)
            dt_row, cum_row, tot_row = row_gates(ci)
            cum_col = _rows_to_cols(cum_row)
            bm_t = xc_ref[4, rows, :].T
            cm = xc_ref[5, rows, :]
            gmats, y_inter = [], []
            for g in range(SSD_GROUPS):
                cm_g = cm[:, g * SSD_STATE:(g + 1) * SSD_STATE].astype(BF16)
                gmats.append(_bdot(cm_g, bm_t[g * SSD_STATE:(g + 1) * SSD_STATE, :].astype(BF16)))
                prs = range(g * hpg // 2, (g + 1) * hpg // 2)
                h_cat = jnp.concatenate([hf_ref[pr] for pr in prs] + [stash_ref[gc, pr] for pr in prs], axis=1)
                y_inter.append(_bdot(cm_g, h_cat.astype(BF16)))
            y_parts = []
            for pr in range(npair):
                g = (2 * pr) // hpg
                xs_f = xc_ref[pr, rows, :]
                xs_p = xs_f.astype(BF16)
                ys, ef, eb = [], [], []
                for hl in range(2):
                    hf = 2 * pr + hl
                    hb = nh + hf
                    ci_f = jnp.broadcast_to(cum_col[:, hf:hf + 1], (CHUNK, CHUNK))
                    ci_b = jnp.broadcast_to(cum_col[:, hb:hb + 1], (CHUNK, CHUNK))
                    dec_f = jnp.where(lower, jnp.exp2(ci_f - cum_row[hf:hf + 1, :]), 0.0)
                    dec_b = jnp.where(upper, jnp.exp2(ci_b - cum_row[hb:hb + 1, :]), 0.0)
                    mmat = gmats[g] * (dec_f * dt_row[hf:hf + 1, :] + dec_b * dt_row[hb:hb + 1, :])
                    ys.append(_bdot(mmat.astype(BF16), xs_p))
                    ef.append(jnp.exp2(ci_f))
                    eb.append(jnp.exp2(ci_b))
                k = pr - g * hpg // 2
                yi = y_inter[g]
                y = (jnp.where(left, ys[0], ys[1])
                     + yi[:, k * LANES:(k + 1) * LANES] * jnp.where(left, ef[0], ef[1])
                     + yi[:, (hpg // 2 + k) * LANES:(hpg // 2 + k + 1) * LANES] * jnp.where(left, eb[0], eb[1])
                     + dexp_ref[:, pr * LANES:(pr + 1) * LANES] * xs_f)
                y_parts.append(y)
            y = jnp.concatenate(y_parts, axis=1)
            y = y * _silu(z_ref[orow, :])
            y = y * lax.rsqrt(jnp.mean(y * y, axis=-1, keepdims=True) + NORM_EPS) * nw_ref[...]
            o_ref[orow, :] = y.astype(o_ref.dtype)
            update_state(rows, 0, hf_ref, dt_row, cum_row, tot_row)
            return c
        return [functools.partial(body, i, 0) for i in range(NCH)]

    return init, phase0, phase1


def _ssd_part(nb, seq, xbc, dt, z, cw, cb, dtb, alog, dexp, nw):
    nt = seq // SEQ_TILE
    ns = SSD_CONV_W // LANES
    prev, nxt = _slab_halo_specs_p0(ns, nt, nb * seq)
    in_specs = [_slab_spec_p0(ns, nt), prev, nxt, _both_spec(CHUNK, nt), _fwd_spec(SSD_W, nt)] + \
               [_const_spec(a.shape) for a in (cw, cb, dtb, alog, dexp, nw)]
    state = (SSD_HEADS // 2, SSD_STATE, LANES)
    scratch = [pltpu.VMEM((ns, SEQ_TILE + 2 * SUBLANES, LANES), F32), pltpu.VMEM((ns, seq, LANES), F32),
               pltpu.VMEM(state, F32), pltpu.VMEM(state, F32), pltpu.VMEM((seq // CHUNK,) + state, F32)]
    return "ssd", _ssd_body, (xbc, xbc, xbc, dt, z, cw, cb, dtb, alog, dexp, nw), in_specs, scratch


def _ret_body(rq_ref, rk_ref, rv_ref, rg_ref, cos_ref, sin_ref, dmat_ref, gin_ref, gst_ref, g128_ref, nw_ref,
              o_ref, q_ref, kt_ref, h_ref, hb_ref, *, p, t, tt, nt):
    nh = RET_HEADS
    half = RET_QK // 2
    hw = nh * half

    def init():
        h_ref[...] = jnp.zeros_like(h_ref)

    def rope(x):
        a, b = x[:, :hw], x[:, hw:]
        c, s = cos_ref[...], sin_ref[...]
        return jnp.concatenate([a * c - b * s, b * c + a * s], axis=1)

    def head_rows(kt, h):
        return jnp.concatenate([kt[h * half:(h + 1) * half, :], kt[hw + h * half:hw + (h + 1) * half, :]], axis=0)

    def widen(hc, h):
        n = hc.shape[1]
        sizes = [h * half, hw - half, hw - (h + 1) * half]
        z0, z1, z2 = [jnp.zeros((r, n), hc.dtype) for r in sizes]
        parts = [z0, hc[:half, :], z1, hc[half:, :], z2]
        return jnp.concatenate([q for q in parts if q.shape[0] > 0], axis=0)

    def update_state(gc, vrow, d):
        v = rv_ref[vrow, :]
        kt = kt_ref[gc]
        for h in range(nh):
            ks = (head_rows(kt, h) * gst_ref[d, h]).astype(BF16)
            s = _bdot(ks, v[:, h * RET_V:(h + 1) * RET_V].astype(BF16))
            h_ref[d, h] = g128_ref[h] * h_ref[d, h] + s

    def phase0():
        k = rope(rk_ref[...])
        for c in range(NCH):
            kt_ref[tt * NCH + c] = k[c * CHUNK:(c + 1) * CHUNK, :].T

        def body(i, c):
            ci = NCH - 1 - i
            gc = tt * NCH + ci
            hb_ref[gc] = h_ref[1]
            update_state(gc, _chunk_rows(ci), 1)
            return c
        return [functools.partial(body, i, 0) for i in range(NCH)]

    def phase1():
        q_ref[...] = rope(rq_ref[...]) * (RET_QK ** -0.5)
        lane = lax.broadcasted_iota(jnp.int32, (CHUNK, 2 * hw), 1) % hw

        def body(ci, c):
            gc = tt * NCH + ci
            rows = _chunk_rows(ci)
            q = q_ref[rows, :]
            v = rv_ref[rows, :]
            kt = kt_ref[gc].astype(BF16)
            outs = []
            for h in range(nh):
                qh = jnp.where((lane >= h * half) & (lane < (h + 1) * half), q, 0.0).astype(BF16)
                s = _bdot(qh, kt)
                o = _bdot((s * dmat_ref[h]).astype(BF16), v[:, h * RET_V:(h + 1) * RET_V].astype(BF16))
                h_cat = jnp.concatenate([widen(h_ref[0, h], h), widen(hb_ref[gc, h], h)], axis=1).astype(BF16)
                oi = _bdot(qh, h_cat)
                o = o + oi[:, :RET_V] * gin_ref[0, h] + oi[:, RET_V:] * gin_ref[1, h]
                outs.append(_head_norm(o))
            y = jnp.concatenate(outs, axis=1) * nw_ref[...]
            o_ref[rows, :] = (_silu(rg_ref[rows, :]) * y).astype(o_ref.dtype)
            update_state(gc, rows, 0)
            return c
        return [functools.partial(body, i, 0) for i in range(NCH)]

    return init, phase0, phase1


def _ret_part(nb, seq, rq, rk, rv, rg, cos, sin, dmat, gin, gst, g128, nw):
    nt = seq // SEQ_TILE
    w = RET_HEADS * RET_QK
    tab_both = pl.BlockSpec((SEQ_TILE, w // 2), lambda b, p, t: (_tile_pos(p, t, nt), 0))
    p0_spec = pl.BlockSpec((SEQ_TILE, w), lambda b, p, t: (b * nt + _p0_tile(p, t, nt), 0))
    in_specs = [_fwd_spec(w, nt), p0_spec, _both_spec(512, nt), _fwd_spec(512, nt), tab_both, tab_both] + \
               [_const_spec(a.shape) for a in (dmat, gin, gst, g128, nw)]
    scratch = [pltpu.VMEM((SEQ_TILE, w), F32), pltpu.VMEM((seq // CHUNK, w, CHUNK), F32),
               pltpu.VMEM((2, RET_HEADS, RET_QK, RET_V), F32),
               pltpu.VMEM((seq // CHUNK, RET_HEADS, RET_QK, RET_V), F32)]
    return "ret", _ret_body, (rq, rk, rv, rg, cos, sin, dmat, gin, gst, g128, nw), in_specs, scratch


def _mlstm_body(qk_ref, qkp_ref, qkn_ref, v_ref, gi_ref, gf_ref, mo_ref, cw_ref, cb_ref, gbi_ref, gbf_ref, nwt_ref,
                o_ref, xpad_ref, qk_s, c_ref, m_ref, c_stash, m_stash, *, p, t, tt, nt):
    nh = ML_HEADS
    nhd = 2 * nh

    def init():
        c_ref[...] = jnp.zeros_like(c_ref)
        m_ref[...] = jnp.zeros_like(m_ref)

    valid_b, valid_f = _tri_masks()
    le = valid_b.astype(F32)
    ge = valid_f.astype(F32)
    rowid8 = lax.broadcasted_iota(jnp.int32, (nhd, CHUNK), 0)
    rowid = lax.broadcasted_iota(jnp.int32, (CHUNK, CHUNK), 0)
    top = rowid < ML_QK
    ones_blk = jnp.ones((ML_NROWS, CHUNK), F32)
    srows = ML_V + ML_NROWS

    def row_gates(ci):
        rows = _chunk_rows(ci)
        gi_col = gi_ref[rows, :] + gbi_ref[...]
        li_row = gi_col.T[0:nhd, :]
        lf_row = -_softplus(-(gf_ref[rows, :] + gbf_ref[...]).T[0:nhd, :])
        b_row = jnp.where(rowid8 < nh, _xdot(lf_row, ge), _xdot(lf_row, le))
        tot = jnp.sum(lf_row, axis=1, keepdims=True)
        return gi_col, li_row, b_row, tot

    def update_state(srow, vrow, d, li_row, b_row, tot):
        w_row = tot - b_row + li_row
        m_loc = jnp.max(w_row, axis=1, keepdims=True)
        e_row = jnp.exp(w_row - m_loc)
        m_prev = m_ref[...]
        m_new = jnp.maximum(tot + m_prev, m_loc)
        dec = jnp.exp(tot + m_prev - m_new)
        inj = jnp.exp(m_loc - m_new)
        v = v_ref[vrow, :]
        for pr in range(nh // 2):
            k_pair = (qk_s[nh // 2 + pr, srow, :] * (ML_QK ** -0.5)).astype(BF16)
            for hl in range(2):
                h = 2 * pr + hl
                hd = d * nh + h
                e_b = jnp.broadcast_to(e_row[hd:hd + 1, :], (srows, CHUNK))
                vt = v[:, h * ML_V:(h + 1) * ML_V].T
                lhs = (jnp.concatenate([vt, ones_blk], axis=0) * e_b).astype(BF16)
                s = _bdot(lhs, k_pair)
                c_ref[hd] = (jnp.broadcast_to(dec[hd:hd + 1, :], (srows, CHUNK)) * c_ref[hd]
                             + jnp.broadcast_to(inj[hd:hd + 1, :], (srows, CHUNK)) * s)
        keep = (rowid8 >= nh) if d == 0 else (rowid8 < nh)
        m_ref[...] = jnp.where(keep, m_prev, m_new)

    def direction_out(d, qts, k_pairs, vts, r_col, b_row, c_aug, m_prev):
        mask = valid_f if d == 0 else valid_b
        outs = []
        for h in range(nh):
            hd = d * nh + h
            rj = jnp.where(mask, r_col[:, hd:hd + 1], -jnp.inf)
            u = jnp.maximum(m_prev[hd:hd + 1, :], jnp.max(rj, axis=0, keepdims=True))
            st = _bdot(k_pairs[h // 2], qts[h])
            pt = (st * jnp.exp(rj - u)).astype(BF16)
            lhs = jnp.concatenate([vts[h], ones_blk], axis=0).astype(BF16)
            nd = _bdot(lhs, pt) + jnp.exp(m_prev[hd:hd + 1, :] - u) * _bdot(c_aug[h].astype(BF16), qts[h])
            floor = jnp.exp(-(b_row[hd:hd + 1, :] + u))
            den = jnp.maximum(jnp.abs(nd[ML_V:ML_V + 1, :]), floor)
            outs.append(nd[0:ML_V, :] / den)
        return outs

    def phase0():
        _conv_to_cache(qk_ref, qkp_ref, qkn_ref, xpad_ref, qk_s, cw_ref, cb_ref, tt, nt, _silu)

        def body(i, c):
            ci = NCH - 1 - i
            gc = tt * NCH + ci
            c_stash[gc] = c_ref[nh:nhd]
            m_stash[gc] = m_ref[...]
            _, li_row, b_row, tot = row_gates(ci)
            update_state(_seq_rows(tt, ci), _chunk_rows(ci), 1, li_row, b_row, tot)
            return c
        return [functools.partial(body, i, 0) for i in range(NCH)]

    def phase1():
        def body(ci, c):
            gc = tt * NCH + ci
            srow = _seq_rows(tt, ci)
            rows = _chunk_rows(ci)
            gi_col, li_row, b_row, tot = row_gates(ci)
            r_col = gi_col - _rows_to_cols(b_row)
            qts, k_pairs, vts = [], [], []
            v = v_ref[rows, :]
            for pr in range(nh // 2):
                qt = qk_s[pr, srow, :].T
                qts.append(jnp.where(top, qt, 0.0).astype(BF16))
                qts.append(jnp.where(top, 0.0, qt).astype(BF16))
                k_pairs.append((qk_s[nh // 2 + pr, srow, :] * (ML_QK ** -0.5)).astype(BF16))
            for h in range(nh):
                vts.append(v[:, h * ML_V:(h + 1) * ML_V].T)
            hf = direction_out(0, qts, k_pairs, vts, r_col, b_row, c_ref[0:nh], m_ref[...])
            hb = direction_out(1, qts, k_pairs, vts, r_col, b_row, c_stash[gc], m_stash[gc])
            ys = []
            for h in range(nh):
                x = hf[h] + hb[h]
                mu = jnp.mean(x, axis=0, keepdims=True)
                xc = x - mu
                var = jnp.mean(xc * xc, axis=0, keepdims=True)
                ys.append((xc * lax.rsqrt(var + NORM_EPS) * nwt_ref[h]).T)
            y = jnp.concatenate(ys, axis=1)
            o_ref[rows, :] = (_sigmoid(mo_ref[rows, :]) * y).astype(o_ref.dtype)
            update_state(srow, rows, 0, li_row, b_row, tot)
            return c
        return [functools.partial(body, i, 0) for i in range(NCH)]

    return init, phase0, phase1


def _mlstm_part(nb, seq, mqk, mv, mgi, mgf, mo, cw, cb, gbi, gbf, nwt):
    nt = seq // SEQ_TILE
    ns = 2 * ML_HEADS * ML_QK // LANES
    prev, nxt = _slab_halo_specs_p0(ns, nt, nb * seq)
    in_specs = [_slab_spec_p0(ns, nt), prev, nxt, _both_spec(512, nt), _both_spec(CHUNK, nt), _both_spec(CHUNK, nt),
                _fwd_spec(512, nt)] + [_const_spec(a.shape) for a in (cw, cb, gbi, gbf, nwt)]
    scratch = [pltpu.VMEM((ns, SEQ_TILE + 2 * SUBLANES, LANES), F32), pltpu.VMEM((ns, seq, LANES), F32),
               pltpu.VMEM((2 * ML_HEADS, ML_V + ML_NROWS, CHUNK), F32),
               pltpu.VMEM((2 * ML_HEADS, CHUNK), F32),
               pltpu.VMEM((seq // CHUNK, ML_HEADS, ML_V + ML_NROWS, CHUNK), F32),
               pltpu.VMEM((seq // CHUNK, 2 * ML_HEADS, CHUNK), F32)]
    return "mlstm", _mlstm_body, (mqk, mqk, mqk, mv, mgi, mgf, mo, cw, cb, gbi, gbf, nwt), in_specs, scratch


def _pad_cols(a, width):
    return jnp.pad(a, [(0, 0)] * (a.ndim - 1) + [(0, width - a.shape[-1])])


def _prepare(w_in, lru_conv_w, lru_conv_b, lru_gate_w, lru_gate_b, lru_lambda, ssd_conv_w, ssd_conv_b, ssd_dt_bias,
             ssd_a_log, ssd_d, ssd_norm_w, ret_norm_w, mlstm_conv_w, mlstm_conv_b, mlstm_gate_b, mlstm_norm_w,
             w_branch, w_out, w_ffn_in, w_ffn_out, ln_g, ln_b):
    sp = {}
    off = 0
    for name, w in (("xa", 512), ("ga", 512), ("z", 512), ("xbc", 768), ("dt", 16), ("rq", 256), ("rk", 256),
                    ("rv", 512), ("rg", 512), ("mqk", 512), ("mv", 512), ("mo", 512), ("mg", 16), ("mix", 4096)):
        sp[name] = w_in[:, :, off:off + w]
        off += w
    mg = sp["mg"].reshape(DEPTH, D_MODEL, 2, 2, ML_HEADS)

    def halves_major(w):
        w = w.reshape(DEPTH, D_MODEL, RET_HEADS, 2, RET_QK // 2)
        return jnp.moveaxis(w, 3, 2).reshape(DEPTH, D_MODEL, RET_HEADS * RET_QK)

    cols = {"dt": _pad_cols(sp["dt"], CHUNK), "rq": halves_major(sp["rq"]), "rk": halves_major(sp["rk"]),
            "mgi": _pad_cols(mg[:, :, :, 0, :].reshape(DEPTH, D_MODEL, 8), CHUNK),
            "mgf": _pad_cols(mg[:, :, :, 1, :].reshape(DEPTH, D_MODEL, 8), CHUNK)}
    wp = jnp.concatenate([cols[n] if n in cols else sp[n] for n, _ in PROJ_OUTS], axis=-1).astype(BF16)
    gbi = _pad_cols(mlstm_gate_b[:, :, 0, :].reshape(DEPTH, 1, 8), CHUNK)
    gbf = _pad_cols(mlstm_gate_b[:, :, 1, :].reshape(DEPTH, 1, 8), CHUNK)
    ml_nwt = jnp.broadcast_to(mlstm_norm_w.reshape(DEPTH, ML_HEADS, ML_V, 1), (DEPTH, ML_HEADS, ML_V, CHUNK))

    eye = jnp.eye(LRU_BLOCKS, dtype=F32)
    dense = jnp.einsum("ldghij,hk->ldghikj", lru_gate_w, eye).reshape(DEPTH, 2, 2, LRU_W, LRU_W)
    lru_wg = jnp.concatenate([dense[:, :, 0], dense[:, :, 1]], axis=-1)

    return dict(
        wp=wp, wmix=sp["mix"].astype(BF16),
        ffn_i=w_ffn_in.astype(BF16), ffn_d=w_ffn_out.astype(BF16),
        ln_g=ln_g[:, :, None, :], ln_b=ln_b[:, :, None, :],
        wb=w_branch.astype(BF16), wo=w_out.astype(BF16),
        lru_cw=lru_conv_w, lru_cb=lru_conv_b[:, None, :], lru_wg=(0.5 * lru_wg).astype(BF16),
        lru_gb=0.5 * lru_gate_b.reshape(DEPTH, 2, 1, 2 * LRU_W), lru_lam=lru_lambda[:, :, None, :],
        ssd_cw=ssd_conv_w, ssd_cb=ssd_conv_b[:, None, :],
        ssd_dtb=_pad_cols(ssd_dt_bias.reshape(DEPTH, 1, 16), CHUNK),
        ssd_alog=_pad_cols(ssd_a_log.reshape(DEPTH, 1, 16), CHUNK),
        ssd_dexp=jnp.repeat(ssd_d, SSD_HEAD_DIM, axis=-1)[:, None, :], ssd_nw=ssd_norm_w[:, None, :],
        ret_nw=ret_norm_w[:, None, :],
        ml_cw=mlstm_conv_w, ml_cb=mlstm_conv_b[:, None, :], ml_gbi=gbi, ml_gbf=gbf, ml_nwt=ml_nwt,
    )


def _ret_tables(seq):
    d = RET_QK
    inv = ROPE_BASE ** (-jnp.arange(0, d, 2, dtype=F32) / d)
    ang = jnp.arange(seq, dtype=F32)[:, None] * inv[None, :]
    cos = jnp.tile(jnp.cos(ang), (1, RET_HEADS))
    sin = jnp.tile(jnp.sin(ang), (1, RET_HEADS))
    lg = jnp.log1p(-jnp.exp2(-5.0 - jnp.arange(RET_HEADS, dtype=F32)))[:, None, None]
    pos = jnp.arange(CHUNK, dtype=F32)
    dmat = jnp.exp(lg * jnp.abs(pos[:, None] - pos[None, :])[None])
    ones = jnp.ones((1, 1, RET_V), F32)
    gin = jnp.stack([jnp.exp(lg * (pos + 1.0)[None, :, None]) * ones,
                     jnp.exp(lg * (CHUNK - pos)[None, :, None]) * ones])
    ones = jnp.ones((1, RET_QK, 1), F32)
    gst = jnp.stack([jnp.exp(lg * (CHUNK - 1.0 - pos)[None, None, :]) * ones,
                     jnp.exp(lg * pos[None, None, :]) * ones])
    g128 = jnp.exp(lg * float(CHUNK)) * jnp.ones((1, RET_QK, RET_V), F32)
    return cos, sin, dmat, gin, gst, g128


def _trunk(x3, pw):
    nb, seq, _ = x3.shape
    x = x3.reshape(nb * seq, D_MODEL)
    tabs = _ret_tables(seq)
    for l in range(DEPTH):
        x = _ffn_call(x, pw["ffn_i"][l, 0], pw["ffn_d"][l, 0], pw["ln_g"][l, 0], pw["ln_b"][l, 0])
        pr = dict(zip([n for n, _ in PROJ_OUTS], _proj_call(x, pw["wp"][l])))
        parts = dict(
            lru=_lru_part(nb, seq, pr["xa"], pr["ga"], pw["lru_cw"][l], pw["lru_cb"][l], pw["lru_wg"][l],
                          pw["lru_gb"][l], pw["lru_lam"][l]),
            ssd=_ssd_part(nb, seq, pr["xbc"], pr["dt"], pr["z"], pw["ssd_cw"][l], pw["ssd_cb"][l], pw["ssd_dtb"][l],
                          pw["ssd_alog"][l], pw["ssd_dexp"][l], pw["ssd_nw"][l]),
            ret=_ret_part(nb, seq, pr["rq"], pr["rk"], pr["rv"], pr["rg"], *tabs, pw["ret_nw"][l]),
            mlstm=_mlstm_part(nb, seq, pr["mqk"], pr["mv"], pr["mgi"], pr["mgf"], pr["mo"], pw["ml_cw"][l],
                              pw["ml_cb"][l], pw["ml_gbi"][l], pw["ml_gbf"][l], pw["ml_nwt"][l]))
        ys = {}
        for group, interleave in MIXER_GROUPS:
            ys.update(zip(group, _mixers_call(nb, seq, [parts[n] for n in group], interleave)))
        x = _merge_ffn_call(x, (ys["lru"], ys["ssd"], ys["ret"], ys["mlstm"]), pw["wmix"][l], pw["wb"][l], pw["wo"][l],
                            pw["ln_g"][l, 1], pw["ln_b"][l, 1], pw["ffn_i"][l, 1], pw["ffn_d"][l, 1], pw["ln_g"][l, 2],
                            pw["ln_b"][l, 2])
    return x.reshape(nb, seq, D_MODEL)


def kernel(x_prompt, x_sample, w_in, lru_conv_w, lru_conv_b, lru_gate_w, lru_gate_b, lru_lambda, ssd_conv_w,
           ssd_conv_b, ssd_dt_bias, ssd_a_log, ssd_d, ssd_norm_w, ret_norm_w, mlstm_conv_w, mlstm_conv_b,
           mlstm_gate_b, mlstm_norm_w, w_branch, w_out, w_ffn_in, w_ffn_out, ln_g, ln_b):
    pw = _prepare(w_in, lru_conv_w, lru_conv_b, lru_gate_w, lru_gate_b, lru_lambda, ssd_conv_w, ssd_conv_b,
                  ssd_dt_bias, ssd_a_log, ssd_d, ssd_norm_w, ret_norm_w, mlstm_conv_w, mlstm_conv_b, mlstm_gate_b,
                  mlstm_norm_w, w_branch, w_out, w_ffn_in, w_ffn_out, ln_g, ln_b)
    return _trunk(x_prompt, pw), _trunk(x_sample, pw)
```

```python
import functools

import jax
import jax.numpy as jnp
from jax import lax
from jax.experimental import pallas as pl
from jax.experimental.pallas import tpu as pltpu

F32 = jnp.float32
BF16 = jnp.bfloat16
HIGHEST = lax.Precision.HIGHEST

D_MODEL = 1024
DEPTH = 4
D_FF = 2816
CHUNK = 128
LRU_W = 512
LRU_BLOCKS = 8
LRU_C = 8.0
SSD_HEADS = 8
SSD_HEAD_DIM = 64
SSD_W = 512
SSD_STATE = 64
SSD_GROUPS = 2
SSD_CONV_W = SSD_W + 2 * SSD_GROUPS * SSD_STATE
RET_HEADS = 4
RET_QK = 64
RET_V = 128
ROPE_BASE = 10000.0
ML_HEADS = 4
ML_QK = 64
ML_V = 128
DN_ALPHA = (2.0 * DEPTH) ** 0.25
NORM_EPS = 1e-5
LOG2E = 1.4426950408889634
TINY = 1.1754944e-38

VMEM_LIMIT_BYTES = 56 * 1024 * 1024
FF_CHUNK = 256
TOKEN_TILE = 512
SEQ_TILE = 512
NCH = SEQ_TILE // CHUNK
SUBLANES = 8
LANES = 128
ML_NROWS = 16
MIXER_GROUPS = ((("ret", "lru"), True), (("mlstm", "ssd"), False))

PROJ_OUTS = (("xa", 512), ("ga", 512), ("z", 512), ("xbc", 768), ("dt", 128), ("rq", 256), ("rk", 256),
             ("rv", 512), ("rg", 512), ("mqk", 512), ("mv", 512), ("mo", 512), ("mgi", 128), ("mgf", 128))
SLAB_OUTS = ("xa", "ga", "xbc", "mqk")
NARROW_OUTS = ("rv",)


def _cparams(sem):
    return pltpu.CompilerParams(dimension_semantics=sem, vmem_limit_bytes=VMEM_LIMIT_BYTES)


def _const_spec(shape):
    nd = len(shape)
    return pl.BlockSpec(shape, lambda *_: (0,) * nd, pipeline_mode=pl.Buffered(1))


def _bdot(a, b):
    return jnp.dot(a, b, preferred_element_type=F32)


def _xdot(a, b):
    return jnp.dot(a, b, preferred_element_type=F32, precision=HIGHEST)


def _sigmoid(x):
    return 0.5 * jnp.tanh(0.5 * x) + 0.5


def _silu(x):
    return x * _sigmoid(x)


def _softplus(x):
    return jnp.maximum(x, 0.0) + jnp.log1p(jnp.exp(-jnp.abs(x)))


def _layer_norm(y, g, b):
    mu = jnp.mean(y, axis=-1, keepdims=True)
    yc = y - mu
    var = jnp.mean(yc * yc, axis=-1, keepdims=True)
    return yc * lax.rsqrt(var + NORM_EPS) * g + b


def _ffn_kernel(x_ref, wi_ref, wd_ref, g_ref, b_ref, o_ref, acc_ref):
    x = x_ref[...]
    xb = x.astype(BF16)
    nf = D_FF // FF_CHUNK
    for f in range(nf):
        gate = _bdot(xb, wi_ref[:, f * FF_CHUNK:(f + 1) * FF_CHUNK])
        up = _bdot(xb, wi_ref[:, D_FF + f * FF_CHUNK:D_FF + (f + 1) * FF_CHUNK])
        h = (_silu(gate) * up).astype(BF16)
        part = _bdot(h, wd_ref[f * FF_CHUNK:(f + 1) * FF_CHUNK, :])
        if f == 0:
            acc_ref[...] = part
        else:
            acc_ref[...] += part
    y = DN_ALPHA * x + 0.5 * acc_ref[...]
    o_ref[...] = _layer_norm(y, g_ref[...], b_ref[...])


def _ffn_call(x, wi, wd, g, b):
    m = x.shape[0]
    tm = TOKEN_TILE
    return pl.pallas_call(
        _ffn_kernel,
        out_shape=jax.ShapeDtypeStruct((m, D_MODEL), F32),
        grid=(m // tm,),
        in_specs=[pl.BlockSpec((tm, D_MODEL), lambda i: (i, 0)),
                  _const_spec(wi.shape), _const_spec(wd.shape), _const_spec(g.shape), _const_spec(b.shape)],
        out_specs=pl.BlockSpec((tm, D_MODEL), lambda i: (i, 0)),
        scratch_shapes=[pltpu.VMEM((tm, D_MODEL), F32)],
        compiler_params=_cparams(("parallel",)),
        name="ffn",
    )(x, wi, wd, g, b)


def _proj_kernel(x_ref, w_ref, *o_refs):
    xb = x_ref[...].astype(BF16)
    off = 0
    for o_ref in o_refs:
        if len(o_ref.shape) == 3:
            w = o_ref.shape[0] * LANES
            res = _bdot(xb, w_ref[:, off:off + w])
            for k in range(o_ref.shape[0]):
                o_ref[k] = res[:, k * LANES:(k + 1) * LANES]
            off += w
        else:
            w = o_ref.shape[1]
            o_ref[...] = _bdot(xb, w_ref[:, off:off + w]).astype(o_ref.dtype)
            off += w


def _proj_call(x, wp):
    m = x.shape[0]
    tm = TOKEN_TILE
    shapes, specs = [], []
    for name, w in PROJ_OUTS:
        if name in SLAB_OUTS:
            shapes.append(jax.ShapeDtypeStruct((w // LANES, m, LANES), F32))
            specs.append(pl.BlockSpec((w // LANES, tm, LANES), lambda i: (0, i, 0)))
        else:
            shapes.append(jax.ShapeDtypeStruct((m, w), BF16 if name in NARROW_OUTS else F32))
            specs.append(pl.BlockSpec((tm, w), lambda i: (i, 0)))
    return pl.pallas_call(
        _proj_kernel,
        out_shape=shapes,
        grid=(m // tm,),
        in_specs=[pl.BlockSpec((tm, D_MODEL), lambda i: (i, 0)), _const_spec(wp.shape)],
        out_specs=specs,
        compiler_params=_cparams(("parallel",)),
        name="proj",
    )(x, wp)


def _merge_ffn_kernel(x_ref, ya_ref, yb_ref, yc_ref, yd_ref, wg_ref, wb_ref, wo_ref, g1_ref, b1_ref,
                      wi_ref, wd_ref, g2_ref, b2_ref, o_ref, acc_ref):
    x = x_ref[...]
    xb = x.astype(BF16)
    merged = None
    for i, y_ref in enumerate((ya_ref, yb_ref, yc_ref, yd_ref)):
        gate = _sigmoid(_bdot(xb, wg_ref[:, i * D_MODEL:(i + 1) * D_MODEL]))
        term = gate * _bdot(y_ref[...], wb_ref[i])
        merged = term if merged is None else merged + term
    out = _bdot(merged.astype(BF16), wo_ref[...])
    x2 = _layer_norm(DN_ALPHA * x + out, g1_ref[...], b1_ref[...])
    x2b = x2.astype(BF16)
    for f in range(D_FF // FF_CHUNK):
        gate = _bdot(x2b, wi_ref[:, f * FF_CHUNK:(f + 1) * FF_CHUNK])
        up = _bdot(x2b, wi_ref[:, D_FF + f * FF_CHUNK:D_FF + (f + 1) * FF_CHUNK])
        h = (_silu(gate) * up).astype(BF16)
        part = _bdot(h, wd_ref[f * FF_CHUNK:(f + 1) * FF_CHUNK, :])
        if f == 0:
            acc_ref[...] = part
        else:
            acc_ref[...] += part
    o_ref[...] = _layer_norm(DN_ALPHA * x2 + 0.5 * acc_ref[...], g2_ref[...], b2_ref[...])


def _merge_ffn_call(x, ys, wg, wb, wo, g1, b1, wi, wd, g2, b2):
    m = x.shape[0]
    tm = TOKEN_TILE
    consts = (wg, wb, wo, g1, b1, wi, wd, g2, b2)
    return pl.pallas_call(
        _merge_ffn_kernel,
        out_shape=jax.ShapeDtypeStruct((m, D_MODEL), F32),
        grid=(m // tm,),
        in_specs=[pl.BlockSpec((tm, D_MODEL), lambda i: (i, 0))]
                 + [pl.BlockSpec((tm, 512), lambda i: (i, 0)) for _ in range(4)]
                 + [_const_spec(c.shape) for c in consts],
        out_specs=pl.BlockSpec((tm, D_MODEL), lambda i: (i, 0)),
        scratch_shapes=[pltpu.VMEM((tm, D_MODEL), F32)],
        compiler_params=_cparams(("parallel",)),
        name="merge_ffn",
    )(x, *ys, *consts)


def _tile_pos(p, t, nt):
    return t + (1 - p) * (nt - 1 - 2 * t)


def _p0_tile(p, t, nt):
    return (1 - p) * (nt - 1 - t)


def _both_spec(w, nt):
    return pl.BlockSpec((SEQ_TILE, w), lambda b, p, t: (b * nt + _tile_pos(p, t, nt), 0))


def _fwd_spec(w, nt):
    return pl.BlockSpec((SEQ_TILE, w), lambda b, p, t: (b * nt + p * t, 0))


def _slab_spec_p0(ns, nt):
    return pl.BlockSpec((ns, SEQ_TILE, LANES), lambda b, p, t: (0, b * nt + _p0_tile(p, t, nt), 0))


def _slab_spec_p1(ns, nt):
    return pl.BlockSpec((ns, SEQ_TILE, LANES), lambda b, p, t: (0, b * nt + p * t, 0))


def _slab_halo_specs_p0(ns, nt, n_rows):
    per = SEQ_TILE // SUBLANES
    last = n_rows // SUBLANES - 1
    prev = pl.BlockSpec((ns, SUBLANES, LANES),
                        lambda b, p, t: (0, jnp.maximum((b * nt + _p0_tile(p, t, nt)) * per - 1, 0), 0))
    nxt = pl.BlockSpec((ns, SUBLANES, LANES),
                       lambda b, p, t: (0, jnp.minimum((b * nt + _p0_tile(p, t, nt) + 1) * per, last), 0))
    return prev, nxt


def _conv_to_cache(x_ref, prev_ref, next_ref, xpad_ref, cache_ref, w_ref, b_ref, tt, nt, act):
    ns = x_ref.shape[0]
    for k in range(ns):
        xpad_ref[k, 0:SUBLANES, :] = jnp.where(tt == 0, 0.0, prev_ref[k])
        xpad_ref[k, SUBLANES:SUBLANES + SEQ_TILE, :] = x_ref[k]
        xpad_ref[k, SUBLANES + SEQ_TILE:2 * SUBLANES + SEQ_TILE, :] = jnp.where(tt == nt - 1, 0.0, next_ref[k])
    base = tt * SEQ_TILE
    for k in range(ns):
        w = w_ref[:, k * LANES:(k + 1) * LANES]
        bias = b_ref[:, k * LANES:(k + 1) * LANES]
        for r in range(0, SEQ_TILE, CHUNK):
            taps = [xpad_ref[k, SUBLANES - 2 + j + r:SUBLANES - 2 + j + r + CHUNK, :] for j in range(4)]
            y = w[0:1, :] * taps[0] + w[1:2, :] * taps[1] + w[2:3, :] * taps[2] + w[3:4, :] * taps[3] + bias
            cache_ref[k, pl.ds(pl.multiple_of(base + r, CHUNK), CHUNK), :] = act(y)


def _mixers_kernel(*refs, nt, layout, interleave):
    p = pl.program_id(1)
    t = pl.program_id(2)
    tt = _tile_pos(p, t, nt)
    n_in = sum(ni for _, ni, _ in layout)
    n_out = len(layout)
    parts, i_in, i_sc = [], 0, n_in + n_out
    for k, (body, ni, ns) in enumerate(layout):
        parts.append(body(*refs[i_in:i_in + ni], refs[n_in + k], *refs[i_sc:i_sc + ns], p=p, t=t, tt=tt, nt=nt))
        i_in += ni
        i_sc += ns

    @pl.when(t == 0)
    def _():
        for init, _, _ in parts:
            init()

    def run(chunk_lists):
        if interleave:
            order = [thunk for step in zip(*chunk_lists) for thunk in step]
        else:
            order = [thunk for chunks in chunk_lists for thunk in chunks]
        for thunk in order:
            thunk()

    @pl.when(p == 0)
    def _():
        run([phase0() for _, phase0, _ in parts])

    @pl.when(p == 1)
    def _():
        run([phase1() for _, _, phase1 in parts])


def _mixers_call(nb, seq, parts, interleave):
    nt = seq // SEQ_TILE
    layout = tuple((body, len(ops), len(scr)) for _, body, ops, _, scr in parts)
    out_spec = pl.BlockSpec((SEQ_TILE, 512), lambda b, p, t: (b * nt + p * t, 0))
    return pl.pallas_call(
        functools.partial(_mixers_kernel, nt=nt, layout=layout, interleave=interleave),
        out_shape=[jax.ShapeDtypeStruct((nb * seq, 512), BF16) for _ in parts],
        grid=(nb, 2, nt),
        in_specs=[sp for _, _, _, specs, _ in parts for sp in specs],
        out_specs=[out_spec for _ in parts],
        scratch_shapes=[sc for _, _, _, _, scr in parts for sc in scr],
        compiler_params=_cparams(("arbitrary", "arbitrary", "arbitrary")),
        name="_".join(name for name, _, _, _, _ in parts),
    )(*[op for _, _, ops, _, _ in parts for op in ops])


def _tri_masks():
    ii = lax.broadcasted_iota(jnp.int32, (CHUNK, CHUNK), 0)
    jj = lax.broadcasted_iota(jnp.int32, (CHUNK, CHUNK), 1)
    return jj <= ii, jj >= ii


def _rows_to_cols(x):
    r = x.shape[0]
    return jnp.concatenate([x, jnp.zeros((CHUNK - r, x.shape[1]), x.dtype)], axis=0).T


def _chunk_rows(ci):
    return pl.ds(pl.multiple_of(ci * CHUNK, CHUNK), CHUNK)


def _seq_rows(tt, ci):
    return pl.ds(pl.multiple_of(tt * SEQ_TILE + ci * CHUNK, CHUNK), CHUNK)


def _head_norm(x):
    mu = jnp.mean(x, axis=-1, keepdims=True)
    xc = x - mu
    var = jnp.mean(xc * xc, axis=-1, keepdims=True)
    return xc * lax.rsqrt(var + NORM_EPS)


def _scan_rows(a, b, reverse):
    n = a.shape[0]
    row = lax.broadcasted_iota(jnp.int32, a.shape, 0)
    s = 1
    while s < n:
        if reverse:
            a_sh = pltpu.roll(a, n - s, 0)
            b_sh = pltpu.roll(b, n - s, 0)
            valid = row < n - s
        else:
            a_sh = pltpu.roll(a, s, 0)
            b_sh = pltpu.roll(b, s, 0)
            valid = row >= s
        b = a * jnp.where(valid, b_sh, 0.0) + b
        a = a * jnp.where(valid, a_sh, 1.0)
        s *= 2
    return a, b


LRU_GROUPS = CHUNK // SUBLANES


def _load_interleaved(ref, row0):
    ns = ref.shape[0]
    return jnp.concatenate(
        [jnp.concatenate([ref[k, pl.ds(row0 + s, LRU_GROUPS, stride=SUBLANES), :] for k in range(ns)], axis=1)
         for s in range(SUBLANES)], axis=0)


def _lru_body(xa_ref, xap_ref, xan_ref, ga_ref, cw_ref, cb_ref, wg_ref, gb_ref, lam_ref, o_ref,
              xpad_ref, xc_ref, carry_ref, hb_ref, y_ref, *, p, t, tt, nt):
    ng = LRU_GROUPS

    def init():
        carry_ref[...] = jnp.zeros_like(carry_ref)

    def direction(ci, d):
        xc = _load_interleaved(xc_ref, pl.multiple_of(tt * SEQ_TILE + ci * CHUNK, CHUNK))
        th = jnp.tanh(_bdot(xc.astype(BF16), wg_ref[d]) + gb_ref[d])
        c_half = (-0.5 * LRU_C) * _softplus(-lam_ref[d])
        log_a = c_half * th[:, :LRU_W] + c_half
        a = jnp.exp(log_a)
        v = -jnp.tanh(log_a) * (a * a + 1.0)
        ix = (th[:, LRU_W:] + 1.0) * (0.5 * xc)
        b = (v * lax.rsqrt(jnp.maximum(v, TINY))) * ix
        sa = [a[s * ng:(s + 1) * ng, :] for s in range(SUBLANES)]
        sb = [b[s * ng:(s + 1) * ng, :] for s in range(SUBLANES)]
        order = range(SUBLANES - 2, -1, -1) if d == 1 else range(1, SUBLANES)
        for s in order:
            q = s + 1 if d == 1 else s - 1
            sb[s] = sa[s] * sb[q] + sb[s]
            sa[s] = sa[s] * sa[q]
        edge = 0 if d == 1 else SUBLANES - 1
        ga, gb = _scan_rows(sa[edge], sb[edge], reverse=(d == 1))
        row = lax.broadcasted_iota(jnp.int32, ga.shape, 0)
        if d == 1:
            ga = jnp.where(row == ng - 1, 1.0, pltpu.roll(ga, ng - 1, 0))
            gb = jnp.where(row == ng - 1, 0.0, pltpu.roll(gb, ng - 1, 0))
        else:
            ga = jnp.where(row == 0, 1.0, pltpu.roll(ga, 1, 0))
            gb = jnp.where(row == 0, 0.0, pltpu.roll(gb, 1, 0))
        h_in = ga * carry_ref[...] + gb
        hs = [sa[s] * h_in + sb[s] for s in range(SUBLANES)]
        carry_ref[...] = hs[0][0:1, :] if d == 1 else hs[SUBLANES - 1][ng - 1:ng, :]
        return jnp.concatenate(hs, axis=0)

    def phase0():
        _conv_to_cache(xa_ref, xap_ref, xan_ref, xpad_ref, xc_ref, cw_ref, cb_ref, tt, nt, lambda v: v)

        def body(i, c):
            ci = NCH - 1 - i
            hb_ref[tt * NCH + ci] = direction(ci, 1)
            return c
        return [functools.partial(body, i, 0) for i in range(NCH)]

    def phase1():
        def body(ci, c):
            h = direction(ci, 0) + hb_ref[tt * NCH + ci]
            y = jax.nn.gelu(_load_interleaved(ga_ref, pl.multiple_of(ci * CHUNK, CHUNK))) * h
            for s in range(SUBLANES):
                for k in range(LRU_W // LANES):
                    y_ref[k, pl.ds(s, ng, stride=SUBLANES), :] = y[s * ng:(s + 1) * ng, k * LANES:(k + 1) * LANES]
            o_ref[_chunk_rows(ci), :] = jnp.concatenate(
                [y_ref[k] for k in range(LRU_W // LANES)], axis=1).astype(o_ref.dtype)
            return c
        return [functools.partial(body, i, 0) for i in range(NCH)]

    return init, phase0, phase1


def _lru_part(nb, seq, xa, ga, cw, cb, wg, gb, lam):
    nt = seq // SEQ_TILE
    ns = LRU_W // LANES
    prev, nxt = _slab_halo_specs_p0(ns, nt, nb * seq)
    in_specs = [_slab_spec_p0(ns, nt), prev, nxt, _slab_spec_p1(ns, nt),
                _const_spec(cw.shape), _const_spec(cb.shape), _const_spec(wg.shape),
                _const_spec(gb.shape), _const_spec(lam.shape)]
    scratch = [pltpu.VMEM((ns, SEQ_TILE + 2 * SUBLANES, LANES), F32), pltpu.VMEM((ns, seq, LANES), F32),
               pltpu.VMEM((1, LRU_W), F32), pltpu.VMEM((seq // CHUNK, CHUNK, LRU_W), F32),
               pltpu.VMEM((ns, CHUNK, LANES), F32)]
    return "lru", _lru_body, (xa, xa, xa, ga, cw, cb, wg, gb, lam), in_specs, scratch


def _ssd_body(xbc_ref, xbcp_ref, xbcn_ref, dt_ref, z_ref, cw_ref, cb_ref, dtb_ref, alog_ref, dexp_ref, nw_ref,
              o_ref, xpad_ref, xc_ref, hf_ref, hbk_ref, stash_ref, *, p, t, tt, nt):
    nh = SSD_HEADS
    nhd = 2 * nh
    npair = nh // 2
    hpg = nh // SSD_GROUPS

    def init():
        hf_ref[...] = jnp.zeros_like(hf_ref)
        hbk_ref[...] = jnp.zeros_like(hbk_ref)

    lower, upper = _tri_masks()
    le = lower.astype(F32)
    ge = upper.astype(F32)
    lane = lax.broadcasted_iota(jnp.int32, (1, CHUNK), 1)
    neg_a = jnp.where(lane < nhd, -LOG2E * jnp.exp(alog_ref[...]), 0.0)
    rowid = lax.broadcasted_iota(jnp.int32, (nhd, CHUNK), 0)
    colid = lax.broadcasted_iota(jnp.int32, (CHUNK, CHUNK), 1)
    left = colid < SSD_HEAD_DIM

    def row_gates(ci):
        dt_col = _softplus(dt_ref[_chunk_rows(ci), :] + dtb_ref[...])
        la_col = dt_col * neg_a
        dt_row = dt_col.T[0:nhd, :]
        la_row = la_col.T[0:nhd, :]
        cum_row = jnp.where(rowid < nh, _xdot(la_row, ge), _xdot(la_row, le))
        tot_row = jnp.sum(la_row, axis=1, keepdims=True)
        return dt_row, cum_row, tot_row

    def update_state(rows, d, h_ref, dt_row, cum_row, tot_row):
        bm_t = xc_ref[4, rows, :].T
        w_row = jnp.exp2(tot_row - cum_row) * dt_row
        dec = jnp.broadcast_to(jnp.exp2(tot_row), (nhd, CHUNK))
        shp = (SSD_STATE, LANES)
        keep = lax.broadcasted_iota(jnp.int32, shp, 1) < SSD_HEAD_DIM
        for pr in range(npair):
            g = (2 * pr) // hpg
            xs_p = xc_ref[pr, rows, :].astype(BF16)
            bg = bm_t[g * SSD_STATE:(g + 1) * SSD_STATE, :]
            hd0 = d * nh + 2 * pr
            s0 = _bdot((bg * w_row[hd0:hd0 + 1, :]).astype(BF16), xs_p)
            s1 = _bdot((bg * w_row[hd0 + 1:hd0 + 2, :]).astype(BF16), xs_p)
            dec_p = jnp.where(keep, jnp.broadcast_to(dec[hd0:hd0 + 1, :], shp),
                              jnp.broadcast_to(dec[hd0 + 1:hd0 + 2, :], shp))
            h_ref[pr] = dec_p * h_ref[pr] + jnp.where(keep, s0, s1)

    def phase0():
        _conv_to_cache(xbc_ref, xbcp_ref, xbcn_ref, xpad_ref, xc_ref, cw_ref, cb_ref, tt, nt, _silu)

        def body(i, c):
            ci = NCH - 1 - i
            stash_ref[tt * NCH + ci] = hbk_ref[...]
            dt_row, cum_row, tot_row = row_gates(ci)
            update_state(_seq_rows(tt, ci), 1, hbk_ref, dt_row, cum_row, tot_row)
            return c
        return [functools.partial(body, i, 0) for i in range(NCH)]

    def phase1():
        def body(ci, c):
            gc = tt * NCH + ci
            rows = _seq_rows(tt, ci)
            orow = _chunk_rows(ci)
            dt_row, cum_row, tot_row = row_gates(ci)
            cum_col = _rows_to_cols(cum_row)
            bm_t = xc_ref[4, rows, :].T
            cm = xc_ref[5, rows, :]
            gmats, y_inter = [], []
            for g in range(SSD_GROUPS):
                cm_g = cm[:, g * SSD_STATE:(g + 1) * SSD_STATE].astype(BF16)
                gmats.append(_bdot(cm_g, bm_t[g * SSD_STATE:(g + 1) * SSD_STATE, :].astype(BF16)))
                prs = range(g * hpg // 2, (g + 1) * hpg // 2)
                h_cat = jnp.concatenate([hf_ref[pr] for pr in prs] + [stash_ref[gc, pr] for pr in prs], axis=1)
                y_inter.append(_bdot(cm_g, h_cat.astype(BF16)))
            y_parts = []
            for pr in range(npair):
                g = (2 * pr) // hpg
                xs_f = xc_ref[pr, rows, :]
                xs_p = xs_f.astype(BF16)
                ys, ef, eb = [], [], []
                for hl in range(2):
                    hf = 2 * pr + hl
                    hb = nh + hf
                    ci_f = jnp.broadcast_to(cum_col[:, hf:hf + 1], (CHUNK, CHUNK))
                    ci_b = jnp.broadcast_to(cum_col[:, hb:hb + 1], (CHUNK, CHUNK))
                    dec_f = jnp.where(lower, jnp.exp2(ci_f - cum_row[hf:hf + 1, :]), 0.0)
                    dec_b = jnp.where(upper, jnp.exp2(ci_b - cum_row[hb:hb + 1, :]), 0.0)
                    mmat = gmats[g] * (dec_f * dt_row[hf:hf + 1, :] + dec_b * dt_row[hb:hb + 1, :])
                    ys.append(_bdot(mmat.astype(BF16), xs_p))
                    ef.append(jnp.exp2(ci_f))
                    eb.append(jnp.exp2(ci_b))
                k = pr - g * hpg // 2
                yi = y_inter[g]
                y = (jnp.where(left, ys[0], ys[1])
                     + yi[:, k * LANES:(k + 1) * LANES] * jnp.where(left, ef[0], ef[1])
                     + yi[:, (hpg // 2 + k) * LANES:(hpg // 2 + k + 1) * LANES] * jnp.where(left, eb[0], eb[1])
                     + dexp_ref[:, pr * LANES:(pr + 1) * LANES] * xs_f)
                y_parts.append(y)
            y = jnp.concatenate(y_parts, axis=1)
            y = y * _silu(z_ref[orow, :])
            y = y * lax.rsqrt(jnp.mean(y * y, axis=-1, keepdims=True) + NORM_EPS) * nw_ref[...]
            o_ref[orow, :] = y.astype(o_ref.dtype)
            update_state(rows, 0, hf_ref, dt_row, cum_row, tot_row)
            return c
        return [functools.partial(body, i, 0) for i in range(NCH)]

    return init, phase0, phase1


def _ssd_part(nb, seq, xbc, dt, z, cw, cb, dtb, alog, dexp, nw):
    nt = seq // SEQ_TILE
    ns = SSD_CONV_W // LANES
    prev, nxt = _slab_halo_specs_p0(ns, nt, nb * seq)
    in_specs = [_slab_spec_p0(ns, nt), prev, nxt, _both_spec(CHUNK, nt), _fwd_spec(SSD_W, nt)] + \
               [_const_spec(a.shape) for a in (cw, cb, dtb, alog, dexp, nw)]
    state = (SSD_HEADS // 2, SSD_STATE, LANES)
    scratch = [pltpu.VMEM((ns, SEQ_TILE + 2 * SUBLANES, LANES), F32), pltpu.VMEM((ns, seq, LANES), F32),
               pltpu.VMEM(state, F32), pltpu.VMEM(state, F32), pltpu.VMEM((seq // CHUNK,) + state, F32)]
    return "ssd", _ssd_body, (xbc, xbc, xbc, dt, z, cw, cb, dtb, alog, dexp, nw), in_specs, scratch


def _ret_body(rq_ref, rk_ref, rv_ref, rg_ref, cos_ref, sin_ref, dmat_ref, gin_ref, gst_ref, g128_ref, nw_ref,
              o_ref, q_ref, kt_ref, h_ref, hb_ref, *, p, t, tt, nt):
    nh = RET_HEADS
    half = RET_QK // 2
    hw = nh * half

    def init():
        h_ref[...] = jnp.zeros_like(h_ref)

    def rope(x):
        a, b = x[:, :hw], x[:, hw:]
        c, s = cos_ref[...], sin_ref[...]
        return jnp.concatenate([a * c - b * s, b * c + a * s], axis=1)

    def head_rows(kt, h):
        return jnp.concatenate([kt[h * half:(h + 1) * half, :], kt[hw + h * half:hw + (h + 1) * half, :]], axis=0)

    def widen(hc, h):
        n = hc.shape[1]
        sizes = [h * half, hw - half, hw - (h + 1) * half]
        z0, z1, z2 = [jnp.zeros((r, n), hc.dtype) for r in sizes]
        parts = [z0, hc[:half, :], z1, hc[half:, :], z2]
        return jnp.concatenate([q for q in parts if q.shape[0] > 0], axis=0)

    def update_state(gc, vrow, d):
        v = rv_ref[vrow, :]
        kt = kt_ref[gc]
        for h in range(nh):
            ks = (head_rows(kt, h) * gst_ref[d, h]).astype(BF16)
            s = _bdot(ks, v[:, h * RET_V:(h + 1) * RET_V].astype(BF16))
            h_ref[d, h] = g128_ref[h] * h_ref[d, h] + s

    def phase0():
        k = rope(rk_ref[...])
        for c in range(NCH):
            kt_ref[tt * NCH + c] = k[c * CHUNK:(c + 1) * CHUNK, :].T

        def body(i, c):
            ci = NCH - 1 - i
            gc = tt * NCH + ci
            hb_ref[gc] = h_ref[1]
            update_state(gc, _chunk_rows(ci), 1)
            return c
        return [functools.partial(body, i, 0) for i in range(NCH)]

    def phase1():
        q_ref[...] = rope(rq_ref[...]) * (RET_QK ** -0.5)
        lane = lax.broadcasted_iota(jnp.int32, (CHUNK, 2 * hw), 1) % hw

        def body(ci, c):
            gc = tt * NCH + ci
            rows = _chunk_rows(ci)
            q = q_ref[rows, :]
            v = rv_ref[rows, :]
            kt = kt_ref[gc].astype(BF16)
            outs = []
            for h in range(nh):
                qh = jnp.where((lane >= h * half) & (lane < (h + 1) * half), q, 0.0).astype(BF16)
                s = _bdot(qh, kt)
                o = _bdot((s * dmat_ref[h]).astype(BF16), v[:, h * RET_V:(h + 1) * RET_V].astype(BF16))
                h_cat = jnp.concatenate([widen(h_ref[0, h], h), widen(hb_ref[gc, h], h)], axis=1).astype(BF16)
                oi = _bdot(qh, h_cat)
                o = o + oi[:, :RET_V] * gin_ref[0, h] + oi[:, RET_V:] * gin_ref[1, h]
                outs.append(_head_norm(o))
            y = jnp.concatenate(outs, axis=1) * nw_ref[...]
            o_ref[rows, :] = (_silu(rg_ref[rows, :]) * y).astype(o_ref.dtype)
            update_state(gc, rows, 0)
            return c
        return [functools.partial(body, i, 0) for i in range(NCH)]

    return init, phase0, phase1


def _ret_part(nb, seq, rq, rk, rv, rg, cos, sin, dmat, gin, gst, g128, nw):
    nt = seq // SEQ_TILE
    w = RET_HEADS * RET_QK
    tab_both = pl.BlockSpec((SEQ_TILE, w // 2), lambda b, p, t: (_tile_pos(p, t, nt), 0))
    p0_spec = pl.BlockSpec((SEQ_TILE, w), lambda b, p, t: (b * nt + _p0_tile(p, t, nt), 0))
    in_specs = [_fwd_spec(w, nt), p0_spec, _both_spec(512, nt), _fwd_spec(512, nt), tab_both, tab_both] + \
               [_const_spec(a.shape) for a in (dmat, gin, gst, g128, nw)]
    scratch = [pltpu.VMEM((SEQ_TILE, w), F32), pltpu.VMEM((seq // CHUNK, w, CHUNK), F32),
               pltpu.VMEM((2, RET_HEADS, RET_QK, RET_V), F32),
               pltpu.VMEM((seq // CHUNK, RET_HEADS, RET_QK, RET_V), F32)]
    return "ret", _ret_body, (rq, rk, rv, rg, cos, sin, dmat, gin, gst, g128, nw), in_specs, scratch


def _mlstm_body(qk_ref, qkp_ref, qkn_ref, v_ref, gi_ref, gf_ref, mo_ref, cw_ref, cb_ref, gbi_ref, gbf_ref, nwt_ref,
                o_ref, xpad_ref, qk_s, c_ref, m_ref, c_stash, m_stash, *, p, t, tt, nt):
    nh = ML_HEADS
    nhd = 2 * nh

    def init():
        c_ref[...] = jnp.zeros_like(c_ref)
        m_ref[...] = jnp.zeros_like(m_ref)

    valid_b, valid_f = _tri_masks()
    le = valid_b.astype(F32)
    ge = valid_f.astype(F32)
    rowid8 = lax.broadcasted_iota(jnp.int32, (nhd, CHUNK), 0)
    rowid = lax.broadcasted_iota(jnp.int32, (CHUNK, CHUNK), 0)
    top = rowid < ML_QK
    ones_blk = jnp.ones((ML_NROWS, CHUNK), F32)
    srows = ML_V + ML_NROWS

    def row_gates(ci):
        rows = _chunk_rows(ci)
        gi_col = gi_ref[rows, :] + gbi_ref[...]
        li_row = gi_col.T[0:nhd, :]
        lf_row = -_softplus(-(gf_ref[rows, :] + gbf_ref[...]).T[0:nhd, :])
        b_row = jnp.where(rowid8 < nh, _xdot(lf_row, ge), _xdot(lf_row, le))
        tot = jnp.sum(lf_row, axis=1, keepdims=True)
        return gi_col, li_row, b_row, tot

    def update_state(srow, vrow, d, li_row, b_row, tot):
        w_row = tot - b_row + li_row
        m_loc = jnp.max(w_row, axis=1, keepdims=True)
        e_row = jnp.exp(w_row - m_loc)
        m_prev = m_ref[...]
        m_new = jnp.maximum(tot + m_prev, m_loc)
        dec = jnp.exp(tot + m_prev - m_new)
        inj = jnp.exp(m_loc - m_new)
        v = v_ref[vrow, :]
        for pr in range(nh // 2):
            k_pair = (qk_s[nh // 2 + pr, srow, :] * (ML_QK ** -0.5)).astype(BF16)
            for hl in range(2):
                h = 2 * pr + hl
                hd = d * nh + h
                e_b = jnp.broadcast_to(e_row[hd:hd + 1, :], (srows, CHUNK))
                vt = v[:, h * ML_V:(h + 1) * ML_V].T
                lhs = (jnp.concatenate([vt, ones_blk], axis=0) * e_b).astype(BF16)
                s = _bdot(lhs, k_pair)
                c_ref[hd] = (jnp.broadcast_to(dec[hd:hd + 1, :], (srows, CHUNK)) * c_ref[hd]
                             + jnp.broadcast_to(inj[hd:hd + 1, :], (srows, CHUNK)) * s)
        keep = (rowid8 >= nh) if d == 0 else (rowid8 < nh)
        m_ref[...] = jnp.where(keep, m_prev, m_new)

    def direction_out(d, qts, k_pairs, vts, r_col, b_row, c_aug, m_prev):
        mask = valid_f if d == 0 else valid_b
        outs = []
        for h in range(nh):
            hd = d * nh + h
            rj = jnp.where(mask, r_col[:, hd:hd + 1], -jnp.inf)
            u = jnp.maximum(m_prev[hd:hd + 1, :], jnp.max(rj, axis=0, keepdims=True))
            st = _bdot(k_pairs[h // 2], qts[h])
            pt = (st * jnp.exp(rj - u)).astype(BF16)
            lhs = jnp.concatenate([vts[h], ones_blk], axis=0).astype(BF16)
            nd = _bdot(lhs, pt) + jnp.exp(m_prev[hd:hd + 1, :] - u) * _bdot(c_aug[h].astype(BF16), qts[h])
            floor = jnp.exp(-(b_row[hd:hd + 1, :] + u))
            den = jnp.maximum(jnp.abs(nd[ML_V:ML_V + 1, :]), floor)
            outs.append(nd[0:ML_V, :] / den)
        return outs

    def phase0():
        _conv_to_cache(qk_ref, qkp_ref, qkn_ref, xpad_ref, qk_s, cw_ref, cb_ref, tt, nt, _silu)

        def body(i, c):
            ci = NCH - 1 - i
            gc = tt * NCH + ci
            c_stash[gc] = c_ref[nh:nhd]
            m_stash[gc] = m_ref[...]
            _, li_row, b_row, tot = row_gates(ci)
            update_state(_seq_rows(tt, ci), _chunk_rows(ci), 1, li_row, b_row, tot)
            return c
        return [functools.partial(body, i, 0) for i in range(NCH)]

    def phase1():
        def body(ci, c):
            gc = tt * NCH + ci
            srow = _seq_rows(tt, ci)
            rows = _chunk_rows(ci)
            gi_col, li_row, b_row, tot = row_gates(ci)
            r_col = gi_col - _rows_to_cols(b_row)
            qts, k_pairs, vts = [], [], []
            v = v_ref[rows, :]
            for pr in range(nh // 2):
                qt = qk_s[pr, srow, :].T
                qts.append(jnp.where(top, qt, 0.0).astype(BF16))
                qts.append(jnp.where(top, 0.0, qt).astype(BF16))
                k_pairs.append((qk_s[nh // 2 + pr, srow, :] * (ML_QK ** -0.5)).astype(BF16))
            for h in range(nh):
                vts.append(v[:, h * ML_V:(h + 1) * ML_V].T)
            hf = direction_out(0, qts, k_pairs, vts, r_col, b_row, c_ref[0:nh], m_ref[...])
            hb = direction_out(1, qts, k_pairs, vts, r_col, b_row, c_stash[gc], m_stash[gc])
            ys = []
            for h in range(nh):
                x = hf[h] + hb[h]
                mu = jnp.mean(x, axis=0, keepdims=True)
                xc = x - mu
                var = jnp.mean(xc * xc, axis=0, keepdims=True)
                ys.append((xc * lax.rsqrt(var + NORM_EPS) * nwt_ref[h]).T)
            y = jnp.concatenate(ys, axis=1)
            o_ref[rows, :] = (_sigmoid(mo_ref[rows, :]) * y).astype(o_ref.dtype)
            update_state(srow, rows, 0, li_row, b_row, tot)
            return c
        return [functools.partial(body, i, 0) for i in range(NCH)]

    return init, phase0, phase1


def _mlstm_part(nb, seq, mqk, mv, mgi, mgf, mo, cw, cb, gbi, gbf, nwt):
    nt = seq // SEQ_TILE
    ns = 2 * ML_HEADS * ML_QK // LANES
    prev, nxt = _slab_halo_specs_p0(ns, nt, nb * seq)
    in_specs = [_slab_spec_p0(ns, nt), prev, nxt, _both_spec(512, nt), _both_spec(CHUNK, nt), _both_spec(CHUNK, nt),
                _fwd_spec(512, nt)] + [_const_spec(a.shape) for a in (cw, cb, gbi, gbf, nwt)]
    scratch = [pltpu.VMEM((ns, SEQ_TILE + 2 * SUBLANES, LANES), F32), pltpu.VMEM((ns, seq, LANES), F32),
               pltpu.VMEM((2 * ML_HEADS, ML_V + ML_NROWS, CHUNK), F32),
               pltpu.VMEM((2 * ML_HEADS, CHUNK), F32),
               pltpu.VMEM((seq // CHUNK, ML_HEADS, ML_V + ML_NROWS, CHUNK), F32),
               pltpu.VMEM((seq // CHUNK, 2 * ML_HEADS, CHUNK), F32)]
    return "mlstm", _mlstm_body, (mqk, mqk, mqk, mv, mgi, mgf, mo, cw, cb, gbi, gbf, nwt), in_specs, scratch


def _pad_cols(a, width):
    return jnp.pad(a, [(0, 0)] * (a.ndim - 1) + [(0, width - a.shape[-1])])


def _prepare(w_in, lru_conv_w, lru_conv_b, lru_gate_w, lru_gate_b, lru_lambda, ssd_conv_w, ssd_conv_b, ssd_dt_bias,
             ssd_a_log, ssd_d, ssd_norm_w, ret_norm_w, mlstm_conv_w, mlstm_conv_b, mlstm_gate_b, mlstm_norm_w,
             w_branch, w_out, w_ffn_in, w_ffn_out, ln_g, ln_b):
    sp = {}
    off = 0
    for name, w in (("xa", 512), ("ga", 512), ("z", 512), ("xbc", 768), ("dt", 16), ("rq", 256), ("rk", 256),
                    ("rv", 512), ("rg", 512), ("mqk", 512), ("mv", 512), ("mo", 512), ("mg", 16), ("mix", 4096)):
        sp[name] = w_in[:, :, off:off + w]
        off += w
    mg = sp["mg"].reshape(DEPTH, D_MODEL, 2, 2, ML_HEADS)

    def halves_major(w):
        w = w.reshape(DEPTH, D_MODEL, RET_HEADS, 2, RET_QK // 2)
        return jnp.moveaxis(w, 3, 2).reshape(DEPTH, D_MODEL, RET_HEADS * RET_QK)

    cols = {"dt": _pad_cols(sp["dt"], CHUNK), "rq": halves_major(sp["rq"]), "rk": halves_major(sp["rk"]),
            "mgi": _pad_cols(mg[:, :, :, 0, :].reshape(DEPTH, D_MODEL, 8), CHUNK),
            "mgf": _pad_cols(mg[:, :, :, 1, :].reshape(DEPTH, D_MODEL, 8), CHUNK)}
    wp = jnp.concatenate([cols[n] if n in cols else sp[n] for n, _ in PROJ_OUTS], axis=-1).astype(BF16)
    gbi = _pad_cols(mlstm_gate_b[:, :, 0, :].reshape(DEPTH, 1, 8), CHUNK)
    gbf = _pad_cols(mlstm_gate_b[:, :, 1, :].reshape(DEPTH, 1, 8), CHUNK)
    ml_nwt = jnp.broadcast_to(mlstm_norm_w.reshape(DEPTH, ML_HEADS, ML_V, 1), (DEPTH, ML_HEADS, ML_V, CHUNK))

    eye = jnp.eye(LRU_BLOCKS, dtype=F32)
    dense = jnp.einsum("ldghij,hk->ldghikj", lru_gate_w, eye).reshape(DEPTH, 2, 2, LRU_W, LRU_W)
    lru_wg = jnp.concatenate([dense[:, :, 0], dense[:, :, 1]], axis=-1)

    return dict(
        wp=wp, wmix=sp["mix"].astype(BF16),
        ffn_i=w_ffn_in.astype(BF16), ffn_d=w_ffn_out.astype(BF16),
        ln_g=ln_g[:, :, None, :], ln_b=ln_b[:, :, None, :],
        wb=w_branch.astype(BF16), wo=w_out.astype(BF16),
        lru_cw=lru_conv_w, lru_cb=lru_conv_b[:, None, :], lru_wg=(0.5 * lru_wg).astype(BF16),
        lru_gb=0.5 * lru_gate_b.reshape(DEPTH, 2, 1, 2 * LRU_W), lru_lam=lru_lambda[:, :, None, :],
        ssd_cw=ssd_conv_w, ssd_cb=ssd_conv_b[:, None, :],
        ssd_dtb=_pad_cols(ssd_dt_bias.reshape(DEPTH, 1, 16), CHUNK),
        ssd_alog=_pad_cols(ssd_a_log.reshape(DEPTH, 1, 16), CHUNK),
        ssd_dexp=jnp.repeat(ssd_d, SSD_HEAD_DIM, axis=-1)[:, None, :], ssd_nw=ssd_norm_w[:, None, :],
        ret_nw=ret_norm_w[:, None, :],
        ml_cw=mlstm_conv_w, ml_cb=mlstm_conv_b[:, None, :], ml_gbi=gbi, ml_gbf=gbf, ml_nwt=ml_nwt,
    )


def _ret_tables(seq):
    d = RET_QK
    inv = ROPE_BASE ** (-jnp.arange(0, d, 2, dtype=F32) / d)
    ang = jnp.arange(seq, dtype=F32)[:, None] * inv[None, :]
    cos = jnp.tile(jnp.cos(ang), (1, RET_HEADS))
    sin = jnp.tile(jnp.sin(ang), (1, RET_HEADS))
    lg = jnp.log1p(-jnp.exp2(-5.0 - jnp.arange(RET_HEADS, dtype=F32)))[:, None, None]
    pos = jnp.arange(CHUNK, dtype=F32)
    dmat = jnp.exp(lg * jnp.abs(pos[:, None] - pos[None, :])[None])
    ones = jnp.ones((1, 1, RET_V), F32)
    gin = jnp.stack([jnp.exp(lg * (pos + 1.0)[None, :, None]) * ones,
                     jnp.exp(lg * (CHUNK - pos)[None, :, None]) * ones])
    ones = jnp.ones((1, RET_QK, 1), F32)
    gst = jnp.stack([jnp.exp(lg * (CHUNK - 1.0 - pos)[None, None, :]) * ones,
                     jnp.exp(lg * pos[None, None, :]) * ones])
    g128 = jnp.exp(lg * float(CHUNK)) * jnp.ones((1, RET_QK, RET_V), F32)
    return cos, sin, dmat, gin, gst, g128


def _trunk(x3, pw):
    nb, seq, _ = x3.shape
    x = x3.reshape(nb * seq, D_MODEL)
    tabs = _ret_tables(seq)
    for l in range(DEPTH):
        x = _ffn_call(x, pw["ffn_i"][l, 0], pw["ffn_d"][l, 0], pw["ln_g"][l, 0], pw["ln_b"][l, 0])
        pr = dict(zip([n for n, _ in PROJ_OUTS], _proj_call(x, pw["wp"][l])))
        parts = dict(
            lru=_lru_part(nb, seq, pr["xa"], pr["ga"], pw["lru_cw"][l], pw["lru_cb"][l], pw["lru_wg"][l],
                          pw["lru_gb"][l], pw["lru_lam"][l]),
            ssd=_ssd_part(nb, seq, pr["xbc"], pr["dt"], pr["z"], pw["ssd_cw"][l], pw["ssd_cb"][l], pw["ssd_dtb"][l],
                          pw["ssd_alog"][l], pw["ssd_dexp"][l], pw["ssd_nw"][l]),
            ret=_ret_part(nb, seq, pr["rq"], pr["rk"], pr["rv"], pr["rg"], *tabs, pw["ret_nw"][l]),
            mlstm=_mlstm_part(nb, seq, pr["mqk"], pr["mv"], pr["mgi"], pr["mgf"], pr["mo"], pw["ml_cw"][l],
                              pw["ml_cb"][l], pw["ml_gbi"][l], pw["ml_gbf"][l], pw["ml_nwt"][l]))
        ys = {}
        for group, interleave in MIXER_GROUPS:
            ys.update(zip(group, _mixers_call(nb, seq, [parts[n] for n in group], interleave)))
        x = _merge_ffn_call(x, (ys["lru"], ys["ssd"], ys["ret"], ys["mlstm"]), pw["wmix"][l], pw["wb"][l], pw["wo"][l],
                            pw["ln_g"][l, 1], pw["ln_b"][l, 1], pw["ffn_i"][l, 1], pw["ffn_d"][l, 1], pw["ln_g"][l, 2],
                            pw["ln_b"][l, 2])
    return x.reshape(nb, seq, D_MODEL)


def kernel(x_prompt, x_sample, w_in, lru_conv_w, lru_conv_b, lru_gate_w, lru_gate_b, lru_lambda, ssd_conv_w,
           ssd_conv_b, ssd_dt_bias, ssd_a_log, ssd_d, ssd_norm_w, ret_norm_w, mlstm_conv_w, mlstm_conv_b,
           mlstm_gate_b, mlstm_norm_w, w_branch, w_out, w_ffn_in, w_ffn_out, ln_g, ln_b):
    pw = _prepare(w_in, lru_conv_w, lru_conv_b, lru_gate_w, lru_gate_b, lru_lambda, ssd_conv_w, ssd_conv_b,
                  ssd_dt_bias, ssd_a_log, ssd_d, ssd_norm_w, ret_norm_w, mlstm_conv_w, mlstm_conv_b, mlstm_gate_b,
                  mlstm_norm_w, w_branch, w_out, w_ffn_in, w_ffn_out, ln_g, ln_b)
    return _trunk(x_prompt, pw), _trunk(x_sample, pw)
```

```python
import functools

import jax
import jax.numpy as jnp
from jax import lax
from jax.experimental import pallas as pl
from jax.experimental.pallas import tpu as pltpu

F32 = jnp.float32
BF16 = jnp.bfloat16
HIGHEST = lax.Precision.HIGHEST

D_MODEL = 1024
DEPTH = 4
D_FF = 2816
CHUNK = 128
LRU_W = 512
LRU_BLOCKS = 8
LRU_C = 8.0
SSD_HEADS = 8
SSD_HEAD_DIM = 64
SSD_W = 512
SSD_STATE = 64
SSD_GROUPS = 2
SSD_CONV_W = SSD_W + 2 * SSD_GROUPS * SSD_STATE
RET_HEADS = 4
RET_QK = 64
RET_V = 128
ROPE_BASE = 10000.0
ML_HEADS = 4
ML_QK = 64
ML_V = 128
DN_ALPHA = (2.0 * DEPTH) ** 0.25
NORM_EPS = 1e-5
LOG2E = 1.4426950408889634
TINY = 1.1754944e-38

VMEM_LIMIT_BYTES = 56 * 1024 * 1024
FF_CHUNK = 256
TOKEN_TILE = 512
SEQ_TILE = 512
NCH = SEQ_TILE // CHUNK
SUBLANES = 8
LANES = 128
ML_NROWS = 16
MIXER_GROUPS = ((("ret", "lru"), True), (("mlstm", "ssd"), False))

PROJ_OUTS = (("xa", 512), ("ga", 512), ("z", 512), ("xbc", 768), ("dt", 128), ("rq", 256), ("rk", 256),
             ("rv", 512), ("rg", 512), ("mqk", 512), ("mv", 512), ("mo", 512), ("mgi", 128), ("mgf", 128))
SLAB_OUTS = ("xa", "ga", "xbc", "mqk")
NARROW_OUTS = ("rv",)


def _cparams(sem):
    return pltpu.CompilerParams(dimension_semantics=sem, vmem_limit_bytes=VMEM_LIMIT_BYTES)


def _const_spec(shape):
    nd = len(shape)
    return pl.BlockSpec(shape, lambda *_: (0,) * nd, pipeline_mode=pl.Buffered(1))


def _bdot(a, b):
    return jnp.dot(a, b, preferred_element_type=F32)


def _xdot(a, b):
    return jnp.dot(a, b, preferred_element_type=F32, precision=HIGHEST)


def _sigmoid(x):
    return 0.5 * jnp.tanh(0.5 * x) + 0.5


def _silu(x):
    return x * _sigmoid(x)


def _softplus(x):
    return jnp.maximum(x, 0.0) + jnp.log1p(jnp.exp(-jnp.abs(x)))


def _layer_norm(y, g, b):
    mu = jnp.mean(y, axis=-1, keepdims=True)
    yc = y - mu
    var = jnp.mean(yc * yc, axis=-1, keepdims=True)
    return yc * lax.rsqrt(var + NORM_EPS) * g + b


def _ffn_kernel(x_ref, wi_ref, wd_ref, g_ref, b_ref, o_ref, acc_ref):
    x = x_ref[...]
    xb = x.astype(BF16)
    nf = D_FF // FF_CHUNK
    for f in range(nf):
        gate = _bdot(xb, wi_ref[:, f * FF_CHUNK:(f + 1) * FF_CHUNK])
        up = _bdot(xb, wi_ref[:, D_FF + f * FF_CHUNK:D_FF + (f + 1) * FF_CHUNK])
        h = (_silu(gate) * up).astype(BF16)
        part = _bdot(h, wd_ref[f * FF_CHUNK:(f + 1) * FF_CHUNK, :])
        if f == 0:
            acc_ref[...] = part
        else:
            acc_ref[...] += part
    y = DN_ALPHA * x + acc_ref[...]
    o_ref[...] = _layer_norm(y, g_ref[...], b_ref[...])


def _ffn_call(x, wi, wd, g, b):
    m = x.shape[0]
    tm = TOKEN_TILE
    return pl.pallas_call(
        _ffn_kernel,
        out_shape=jax.ShapeDtypeStruct((m, D_MODEL), F32),
        grid=(m // tm,),
        in_specs=[pl.BlockSpec((tm, D_MODEL), lambda i: (i, 0)),
                  _const_spec(wi.shape), _const_spec(wd.shape), _const_spec(g.shape), _const_spec(b.shape)],
        out_specs=pl.BlockSpec((tm, D_MODEL), lambda i: (i, 0)),
        scratch_shapes=[pltpu.VMEM((tm, D_MODEL), F32)],
        compiler_params=_cparams(("parallel",)),
        name="ffn",
    )(x, wi, wd, g, b)


def _proj_kernel(x_ref, w_ref, *o_refs):
    xb = x_ref[...].astype(BF16)
    off = 0
    for o_ref in o_refs:
        if len(o_ref.shape) == 3:
            w = o_ref.shape[0] * LANES
            res = _bdot(xb, w_ref[:, off:off + w])
            for k in range(o_ref.shape[0]):
                o_ref[k] = res[:, k * LANES:(k + 1) * LANES]
            off += w
        else:
            w = o_ref.shape[1]
            o_ref[...] = _bdot(xb, w_ref[:, off:off + w]).astype(o_ref.dtype)
            off += w


def _proj_call(x, wp):
    m = x.shape[0]
    tm = TOKEN_TILE
    shapes, specs = [], []
    for name, w in PROJ_OUTS:
        if name in SLAB_OUTS:
            shapes.append(jax.ShapeDtypeStruct((w // LANES, m, LANES), F32))
            specs.append(pl.BlockSpec((w // LANES, tm, LANES), lambda i: (0, i, 0)))
        else:
            shapes.append(jax.ShapeDtypeStruct((m, w), BF16 if name in NARROW_OUTS else F32))
            specs.append(pl.BlockSpec((tm, w), lambda i: (i, 0)))
    return pl.pallas_call(
        _proj_kernel,
        out_shape=shapes,
        grid=(m // tm,),
        in_specs=[pl.BlockSpec((tm, D_MODEL), lambda i: (i, 0)), _const_spec(wp.shape)],
        out_specs=specs,
        compiler_params=_cparams(("parallel",)),
        name="proj",
    )(x, wp)


def _merge_ffn_kernel(x_ref, ya_ref, yb_ref, yc_ref, yd_ref, wg_ref, wb_ref, wo_ref, g1_ref, b1_ref,
                      wi_ref, wd_ref, g2_ref, b2_ref, o_ref, acc_ref):
    x = x_ref[...]
    xb = x.astype(BF16)
    merged = None
    for i, y_ref in enumerate((ya_ref, yb_ref, yc_ref, yd_ref)):
        gate = _sigmoid(_bdot(xb, wg_ref[:, i * D_MODEL:(i + 1) * D_MODEL]))
        term = gate * _bdot(y_ref[...], wb_ref[i])
        merged = term if merged is None else merged + term
    out = _bdot(merged.astype(BF16), wo_ref[...])
    x2 = _layer_norm(DN_ALPHA * x + out, g1_ref[...], b1_ref[...])
    x2b = x2.astype(BF16)
    for f in range(D_FF // FF_CHUNK):
        gate = _bdot(x2b, wi_ref[:, f * FF_CHUNK:(f + 1) * FF_CHUNK])
        up = _bdot(x2b, wi_ref[:, D_FF + f * FF_CHUNK:D_FF + (f + 1) * FF_CHUNK])
        h = (_silu(gate) * up).astype(BF16)
        part = _bdot(h, wd_ref[f * FF_CHUNK:(f + 1) * FF_CHUNK, :])
        if f == 0:
            acc_ref[...] = part
        else:
            acc_ref[...] += part
    o_ref[...] = _layer_norm(DN_ALPHA * x2 + acc_ref[...], g2_ref[...], b2_ref[...])


def _merge_ffn_call(x, ys, wg, wb, wo, g1, b1, wi, wd, g2, b2):
    m = x.shape[0]
    tm = TOKEN_TILE
    consts = (wg, wb, wo, g1, b1, wi, wd, g2, b2)
    return pl.pallas_call(
        _merge_ffn_kernel,
        out_shape=jax.ShapeDtypeStruct((m, D_MODEL), F32),
        grid=(m // tm,),
        in_specs=[pl.BlockSpec((tm, D_MODEL), lambda i: (i, 0))]
                 + [pl.BlockSpec((tm, 512), lambda i: (i, 0)) for _ in range(4)]
                 + [_const_spec(c.shape) for c in consts],
        out_specs=pl.BlockSpec((tm, D_MODEL), lambda i: (i, 0)),
        scratch_shapes=[pltpu.VMEM((tm, D_MODEL), F32)],
        compiler_params=_cparams(("parallel",)),
        name="merge_ffn",
    )(x, *ys, *consts)


def _tile_pos(p, t, nt):
    return t + (1 - p) * (nt - 1 - 2 * t)


def _p0_tile(p, t, nt):
    return (1 - p) * (nt - 1 - t)


def _both_spec(w, nt):
    return pl.BlockSpec((SEQ_TILE, w), lambda b, p, t: (b * nt + _tile_pos(p, t, nt), 0))


def _fwd_spec(w, nt):
    return pl.BlockSpec((SEQ_TILE, w), lambda b, p, t: (b * nt + p * t, 0))


def _slab_spec_p0(ns, nt):
    return pl.BlockSpec((ns, SEQ_TILE, LANES), lambda b, p, t: (0, b * nt + _p0_tile(p, t, nt), 0))


def _slab_spec_p1(ns, nt):
    return pl.BlockSpec((ns, SEQ_TILE, LANES), lambda b, p, t: (0, b * nt + p * t, 0))


def _slab_halo_specs_p0(ns, nt, n_rows):
    per = SEQ_TILE // SUBLANES
    last = n_rows // SUBLANES - 1
    prev = pl.BlockSpec((ns, SUBLANES, LANES),
                        lambda b, p, t: (0, jnp.maximum((b * nt + _p0_tile(p, t, nt)) * per - 1, 0), 0))
    nxt = pl.BlockSpec((ns, SUBLANES, LANES),
                       lambda b, p, t: (0, jnp.minimum((b * nt + _p0_tile(p, t, nt) + 1) * per, last), 0))
    return prev, nxt


def _conv_to_cache(x_ref, prev_ref, next_ref, xpad_ref, cache_ref, w_ref, b_ref, tt, nt, act):
    ns = x_ref.shape[0]
    for k in range(ns):
        xpad_ref[k, 0:SUBLANES, :] = jnp.where(tt == 0, 0.0, prev_ref[k])
        xpad_ref[k, SUBLANES:SUBLANES + SEQ_TILE, :] = x_ref[k]
        xpad_ref[k, SUBLANES + SEQ_TILE:2 * SUBLANES + SEQ_TILE, :] = jnp.where(tt == nt - 1, 0.0, next_ref[k])
    base = tt * SEQ_TILE
    for k in range(ns):
        w = w_ref[:, k * LANES:(k + 1) * LANES]
        bias = b_ref[:, k * LANES:(k + 1) * LANES]
        for r in range(0, SEQ_TILE, CHUNK):
            taps = [xpad_ref[k, SUBLANES - 2 + j + r:SUBLANES - 2 + j + r + CHUNK, :] for j in range(4)]
            y = w[0:1, :] * taps[0] + w[1:2, :] * taps[1] + w[2:3, :] * taps[2] + w[3:4, :] * taps[3] + bias
            cache_ref[k, pl.ds(pl.multiple_of(base + r, CHUNK), CHUNK), :] = act(y)


def _mixers_kernel(*refs, nt, layout, interleave):
    p = pl.program_id(1)
    t = pl.program_id(2)
    tt = _tile_pos(p, t, nt)
    n_in = sum(ni for _, ni, _ in layout)
    n_out = len(layout)
    parts, i_in, i_sc = [], 0, n_in + n_out
    for k, (body, ni, ns) in enumerate(layout):
        parts.append(body(*refs[i_in:i_in + ni], refs[n_in + k], *refs[i_sc:i_sc + ns], p=p, t=t, tt=tt, nt=nt))
        i_in += ni
        i_sc += ns

    @pl.when(t == 0)
    def _():
        for init, _, _ in parts:
            init()

    def run(chunk_lists):
        if interleave:
            order = [thunk for step in zip(*chunk_lists) for thunk in step]
        else:
            order = [thunk for chunks in chunk_lists for thunk in chunks]
        for thunk in order:
            thunk()

    @pl.when(p == 0)
    def _():
        run([phase0() for _, phase0, _ in parts])

    @pl.when(p == 1)
    def _():
        run([phase1() for _, _, phase1 in parts])


def _mixers_call(nb, seq, parts, interleave):
    nt = seq // SEQ_TILE
    layout = tuple((body, len(ops), len(scr)) for _, body, ops, _, scr in parts)
    out_spec = pl.BlockSpec((SEQ_TILE, 512), lambda b, p, t: (b * nt + p * t, 0))
    return pl.pallas_call(
        functools.partial(_mixers_kernel, nt=nt, layout=layout, interleave=interleave),
        out_shape=[jax.ShapeDtypeStruct((nb * seq, 512), BF16) for _ in parts],
        grid=(nb, 2, nt),
        in_specs=[sp for _, _, _, specs, _ in parts for sp in specs],
        out_specs=[out_spec for _ in parts],
        scratch_shapes=[sc for _, _, _, _, scr in parts for sc in scr],
        compiler_params=_cparams(("arbitrary", "arbitrary", "arbitrary")),
        name="_".join(name for name, _, _, _, _ in parts),
    )(*[op for _, _, ops, _, _ in parts for op in ops])


def _tri_masks():
    ii = lax.broadcasted_iota(jnp.int32, (CHUNK, CHUNK), 0)
    jj = lax.broadcasted_iota(jnp.int32, (CHUNK, CHUNK), 1)
    return jj <= ii, jj >= ii


def _rows_to_cols(x):
    r = x.shape[0]
    return jnp.concatenate([x, jnp.zeros((CHUNK - r, x.shape[1]), x.dtype)], axis=0).T


def _chunk_rows(ci):
    return pl.ds(pl.multiple_of(ci * CHUNK, CHUNK), CHUNK)


def _seq_rows(tt, ci):
    return pl.ds(pl.multiple_of(tt * SEQ_TILE + ci * CHUNK, CHUNK), CHUNK)


def _head_norm(x):
    mu = jnp.mean(x, axis=-1, keepdims=True)
    xc = x - mu
    var = jnp.mean(xc * xc, axis=-1, keepdims=True)
    return xc * lax.rsqrt(var + NORM_EPS)


def _scan_rows(a, b, reverse):
    n = a.shape[0]
    row = lax.broadcasted_iota(jnp.int32, a.shape, 0)
    s = 1
    while s < n:
        if reverse:
            a_sh = pltpu.roll(a, n - s, 0)
            b_sh = pltpu.roll(b, n - s, 0)
            valid = row < n - s
        else:
            a_sh = pltpu.roll(a, s, 0)
            b_sh = pltpu.roll(b, s, 0)
            valid = row >= s
        b = a * jnp.where(valid, b_sh, 0.0) + b
        a = a * jnp.where(valid, a_sh, 1.0)
        s *= 2
    return a, b


LRU_GROUPS = CHUNK // SUBLANES


def _load_interleaved(ref, row0):
    ns = ref.shape[0]
    return jnp.concatenate(
        [jnp.concatenate([ref[k, pl.ds(row0 + s, LRU_GROUPS, stride=SUBLANES), :] for k in range(ns)], axis=1)
         for s in range(SUBLANES)], axis=0)


def _lru_body(xa_ref, xap_ref, xan_ref, ga_ref, cw_ref, cb_ref, wg_ref, gb_ref, lam_ref, o_ref,
              xpad_ref, xc_ref, carry_ref, hb_ref, y_ref, *, p, t, tt, nt):
    ng = LRU_GROUPS

    def init():
        carry_ref[...] = jnp.zeros_like(carry_ref)

    def direction(ci, d):
        xc = _load_interleaved(xc_ref, pl.multiple_of(tt * SEQ_TILE + ci * CHUNK, CHUNK))
        th = jnp.tanh(_bdot(xc.astype(BF16), wg_ref[d]) + gb_ref[d])
        c_half = (-0.5 * LRU_C) * _softplus(-lam_ref[d])
        log_a = c_half * th[:, :LRU_W] + c_half
        a = jnp.exp(log_a)
        v = -jnp.tanh(log_a) * (a * a + 1.0)
        ix = (th[:, LRU_W:] + 1.0) * (0.5 * xc)
        b = (v * lax.rsqrt(jnp.maximum(v, TINY))) * ix
        sa = [a[s * ng:(s + 1) * ng, :] for s in range(SUBLANES)]
        sb = [b[s * ng:(s + 1) * ng, :] for s in range(SUBLANES)]
        order = range(SUBLANES - 2, -1, -1) if d == 1 else range(1, SUBLANES)
        for s in order:
            q = s + 1 if d == 1 else s - 1
            sb[s] = sa[s] * sb[q] + sb[s]
            sa[s] = sa[s] * sa[q]
        edge = 0 if d == 1 else SUBLANES - 1
        ga, gb = _scan_rows(sa[edge], sb[edge], reverse=(d == 1))
        row = lax.broadcasted_iota(jnp.int32, ga.shape, 0)
        if d == 1:
            ga = jnp.where(row == ng - 1, 1.0, pltpu.roll(ga, ng - 1, 0))
            gb = jnp.where(row == ng - 1, 0.0, pltpu.roll(gb, ng - 1, 0))
        else:
            ga = jnp.where(row == 0, 1.0, pltpu.roll(ga, 1, 0))
            gb = jnp.where(row == 0, 0.0, pltpu.roll(gb, 1, 0))
        h_in = ga * carry_ref[...] + gb
        hs = [sa[s] * h_in + sb[s] for s in range(SUBLANES)]
        carry_ref[...] = hs[0][0:1, :] if d == 1 else hs[SUBLANES - 1][ng - 1:ng, :]
        return jnp.concatenate(hs, axis=0)

    def phase0():
        _conv_to_cache(xa_ref, xap_ref, xan_ref, xpad_ref, xc_ref, cw_ref, cb_ref, tt, nt, lambda v: v)

        def body(i, c):
            ci = NCH - 1 - i
            hb_ref[tt * NCH + ci] = direction(ci, 1)
            return c
        return [functools.partial(body, i, 0) for i in range(NCH)]

    def phase1():
        def body(ci, c):
            h = direction(ci, 0) + hb_ref[tt * NCH + ci]
            y = jax.nn.gelu(_load_interleaved(ga_ref, pl.multiple_of(ci * CHUNK, CHUNK))) * h
            for s in range(SUBLANES):
                for k in range(LRU_W // LANES):
                    y_ref[k, pl.ds(s, ng, stride=SUBLANES), :] = y[s * ng:(s + 1) * ng, k * LANES:(k + 1) * LANES]
            o_ref[_chunk_rows(ci), :] = jnp.concatenate(
                [y_ref[k] for k in range(LRU_W // LANES)], axis=1).astype(o_ref.dtype)
            return c
        return [functools.partial(body, i, 0) for i in range(NCH)]

    return init, phase0, phase1


def _lru_part(nb, seq, xa, ga, cw, cb, wg, gb, lam):
    nt = seq // SEQ_TILE
    ns = LRU_W // LANES
    prev, nxt = _slab_halo_specs_p0(ns, nt, nb * seq)
    in_specs = [_slab_spec_p0(ns, nt), prev, nxt, _slab_spec_p1(ns, nt),
                _const_spec(cw.shape), _const_spec(cb.shape), _const_spec(wg.shape),
                _const_spec(gb.shape), _const_spec(lam.shape)]
    scratch = [pltpu.VMEM((ns, SEQ_TILE + 2 * SUBLANES, LANES), F32), pltpu.VMEM((ns, seq, LANES), F32),
               pltpu.VMEM((1, LRU_W), F32), pltpu.VMEM((seq // CHUNK, CHUNK, LRU_W), F32),
               pltpu.VMEM((ns, CHUNK, LANES), F32)]
    return "lru", _lru_body, (xa, xa, xa, ga, cw, cb, wg, gb, lam), in_specs, scratch


def _ssd_body(xbc_ref, xbcp_ref, xbcn_ref, dt_ref, z_ref, cw_ref, cb_ref, dtb_ref, alog_ref, dexp_ref, nw_ref,
              o_ref, xpad_ref, xc_ref, hf_ref, hbk_ref, stash_ref, *, p, t, tt, nt):
    nh = SSD_HEADS
    nhd = 2 * nh
    npair = nh // 2
    hpg = nh // SSD_GROUPS

    def init():
        hf_ref[...] = jnp.zeros_like(hf_ref)
        hbk_ref[...] = jnp.zeros_like(hbk_ref)

    lower, upper = _tri_masks()
    le = lower.astype(F32)
    ge = upper.astype(F32)
    lane = lax.broadcasted_iota(jnp.int32, (1, CHUNK), 1)
    neg_a = jnp.where(lane < nhd, -LOG2E * jnp.exp(alog_ref[...]), 0.0)
    rowid = lax.broadcasted_iota(jnp.int32, (nhd, CHUNK), 0)
    colid = lax.broadcasted_iota(jnp.int32, (CHUNK, CHUNK), 1)
    left = colid < SSD_HEAD_DIM

    def row_gates(ci):
        dt_col = _softplus(dt_ref[_chunk_rows(ci), :] + dtb_ref[...])
        la_col = dt_col * neg_a
        dt_row = dt_col.T[0:nhd, :]
        la_row = la_col.T[0:nhd, :]
        cum_row = jnp.where(rowid < nh, _xdot(la_row, ge), _xdot(la_row, le))
        tot_row = jnp.sum(la_row, axis=1, keepdims=True)
        return dt_row, cum_row, tot_row

    def update_state(rows, d, h_ref, dt_row, cum_row, tot_row):
        bm_t = xc_ref[4, rows, :].T
        w_row = jnp.exp2(tot_row - cum_row) * dt_row
        dec = jnp.broadcast_to(jnp.exp2(tot_row), (nhd, CHUNK))
        shp = (SSD_STATE, LANES)
        keep = lax.broadcasted_iota(jnp.int32, shp, 1) < SSD_HEAD_DIM
        for pr in range(npair):
            g = (2 * pr) // hpg
            xs_p = xc_ref[pr, rows, :].astype(BF16)
            bg = bm_t[g * SSD_STATE:(g + 1) * SSD_STATE, :]
            hd0 = d * nh + 2 * pr
            s0 = _bdot((bg * w_row[hd0:hd0 + 1, :]).astype(BF16), xs_p)
            s1 = _bdot((bg * w_row[hd0 + 1:hd0 + 2, :]).astype(BF16), xs_p)
            dec_p = jnp.where(keep, jnp.broadcast_to(dec[hd0:hd0 + 1, :], shp),
                              jnp.broadcast_to(dec[hd0 + 1:hd0 + 2, :], shp))
            h_ref[pr] = dec_p * h_ref[pr] + jnp.where(keep, s0, s1)

    def phase0():
        _conv_to_cache(xbc_ref, xbcp_ref, xbcn_ref, xpad_ref, xc_ref, cw_ref, cb_ref, tt, nt, _silu)

        def body(i, c):
            ci = NCH - 1 - i
            stash_ref[tt * NCH + ci] = hbk_ref[...]
            dt_row, cum_row, tot_row = row_gates(ci)
            update_state(_seq_rows(tt, ci), 1, hbk_ref, dt_row, cum_row, tot_row)
            return c
        return [functools.partial(body, i, 0) for i in range(NCH)]

    def phase1():
        def body(ci, c):
            gc = tt * NCH + ci
            rows = _seq_rows(tt, ci)
            orow = _chunk_rows(ci)
            dt_row, cum_row, tot_row = row_gates(ci)
            cum_col = _rows_to_cols(cum_row)
            bm_t = xc_ref[4, rows, :].T
            cm = xc_ref[5, rows, :]
            gmats, y_inter = [], []
            for g in range(SSD_GROUPS):
                cm_g = cm[:, g * SSD_STATE:(g + 1) * SSD_STATE].astype(BF16)
                gmats.append(_bdot(cm_g, bm_t[g * SSD_STATE:(g + 1) * SSD_STATE, :].astype(BF16)))
                prs = range(g * hpg // 2, (g + 1) * hpg // 2)
                h_cat = jnp.concatenate([hf_ref[pr] for pr in prs] + [stash_ref[gc, pr] for pr in prs], axis=1)
                y_inter.append(_bdot(cm_g, h_cat.astype(BF16)))
            y_parts = []
            for pr in range(npair):
                g = (2 * pr) // hpg
                xs_f = xc_ref[pr, rows, :]
                xs_p = xs_f.astype(BF16)
                ys, ef, eb = [], [], []
                for hl in range(2):
                    hf = 2 * pr + hl
                    hb = nh + hf
                    ci_f = jnp.broadcast_to(cum_col[:, hf:hf + 1], (CHUNK, CHUNK))
                    ci_b = jnp.broadcast_to(cum_col[:, hb:hb + 1], (CHUNK, CHUNK))
                    dec_f = jnp.where(lower, jnp.exp2(ci_f - cum_row[hf:hf + 1, :]), 0.0)
                    dec_b = jnp.where(upper, jnp.exp2(ci_b - cum_row[hb:hb + 1, :]), 0.0)
                    mmat = gmats[g] * (dec_f * dt_row[hf:hf + 1, :] + dec_b * dt_row[hb:hb + 1, :])
                    ys.append(_bdot(mmat.astype(BF16), xs_p))
                    ef.append(jnp.exp2(ci_f))
                    eb.append(jnp.exp2(ci_b))
                k = pr - g * hpg // 2
                yi = y_inter[g]
                y = (jnp.where(left, ys[0], ys[1])
                     + yi[:, k * LANES:(k + 1) * LANES] * jnp.where(left, ef[0], ef[1])
                     + yi[:, (hpg // 2 + k) * LANES:(hpg // 2 + k + 1) * LANES] * jnp.where(left, eb[0], eb[1])
                     + dexp_ref[:, pr * LANES:(pr + 1) * LANES] * xs_f)
                y_parts.append(y)
            y = jnp.concatenate(y_parts, axis=1)
            y = y * _silu(z_ref[orow, :])
            y = y * lax.rsqrt(jnp.mean(y * y, axis=-1, keepdims=True) + NORM_EPS) * nw_ref[...]
            o_ref[orow, :] = y.astype(o_ref.dtype)
            update_state(rows, 0, hf_ref, dt_row, cum_row, tot_row)
            return c
        return [functools.partial(body, i, 0) for i in range(NCH)]

    return init, phase0, phase1


def _ssd_part(nb, seq, xbc, dt, z, cw, cb, dtb, alog, dexp, nw):
    nt = seq // SEQ_TILE
    ns = SSD_CONV_W // LANES
    prev, nxt = _slab_halo_specs_p0(ns, nt, nb * seq)
    in_specs = [_slab_spec_p0(ns, nt), prev, nxt, _both_spec(CHUNK, nt), _fwd_spec(SSD_W, nt)] + \
               [_const_spec(a.shape) for a in (cw, cb, dtb, alog, dexp, nw)]
    state = (SSD_HEADS // 2, SSD_STATE, LANES)
    scratch = [pltpu.VMEM((ns, SEQ_TILE + 2 * SUBLANES, LANES), F32), pltpu.VMEM((ns, seq, LANES), F32),
               pltpu.VMEM(state, F32), pltpu.VMEM(state, F32), pltpu.VMEM((seq // CHUNK,) + state, F32)]
    return "ssd", _ssd_body, (xbc, xbc, xbc, dt, z, cw, cb, dtb, alog, dexp, nw), in_specs, scratch


def _ret_body(rq_ref, rk_ref, rv_ref, rg_ref, cos_ref, sin_ref, dmat_ref, gin_ref, gst_ref, g128_ref, nw_ref,
              o_ref, q_ref, kt_ref, h_ref, hb_ref, *, p, t, tt, nt):
    nh = RET_HEADS
    half = RET_QK // 2
    hw = nh * half

    def init():
        h_ref[...] = jnp.zeros_like(h_ref)

    def rope(x):
        a, b = x[:, :hw], x[:, hw:]
        c, s = cos_ref[...], sin_ref[...]
        return jnp.concatenate([a * c - b * s, b * c + a * s], axis=1)

    def head_rows(kt, h):
        return jnp.concatenate([kt[h * half:(h + 1) * half, :], kt[hw + h * half:hw + (h + 1) * half, :]], axis=0)

    def widen(hc, h):
        n = hc.shape[1]
        sizes = [h * half, hw - half, hw - (h + 1) * half]
        z0, z1, z2 = [jnp.zeros((r, n), hc.dtype) for r in sizes]
        parts = [z0, hc[:half, :], z1, hc[half:, :], z2]
        return jnp.concatenate([q for q in parts if q.shape[0] > 0], axis=0)

    def update_state(gc, vrow, d):
        v = rv_ref[vrow, :]
        kt = kt_ref[gc]
        for h in range(nh):
            ks = (head_rows(kt, h) * gst_ref[d, h]).astype(BF16)
            s = _bdot(ks, v[:, h * RET_V:(h + 1) * RET_V].astype(BF16))
            h_ref[d, h] = g128_ref[h] * h_ref[d, h] + s

    def phase0():
        k = rope(rk_ref[...])
        for c in range(NCH):
            kt_ref[tt * NCH + c] = k[c * CHUNK:(c + 1) * CHUNK, :].T

        def body(i, c):
            ci = NCH - 1 - i
            gc = tt * NCH + ci
            hb_ref[gc] = h_ref[1]
            update_state(gc, _chunk_rows(ci), 1)
            return c
        return [functools.partial(body, i, 0) for i in range(NCH)]

    def phase1():
        q_ref[...] = rope(rq_ref[...]) * (RET_QK ** -0.5)
        lane = lax.broadcasted_iota(jnp.int32, (CHUNK, 2 * hw), 1) % hw

        def body(ci, c):
            gc = tt * NCH + ci
            rows = _chunk_rows(ci)
            q = q_ref[rows, :]
            v = rv_ref[rows, :]
            kt = kt_ref[gc].astype(BF16)
            outs = []
            for h in range(nh):
                qh = jnp.where((lane >= h * half) & (lane < (h + 1) * half), q, 0.0).astype(BF16)
                s = _bdot(qh, kt)
                o = _bdot((s * dmat_ref[h]).astype(BF16), v[:, h * RET_V:(h + 1) * RET_V].astype(BF16))
                h_cat = jnp.concatenate([widen(h_ref[0, h], h), widen(hb_ref[gc, h], h)], axis=1).astype(BF16)
                oi = _bdot(qh, h_cat)
                o = o + oi[:, :RET_V] * gin_ref[0, h] + oi[:, RET_V:] * gin_ref[1, h]
                outs.append(_head_norm(o))
            y = jnp.concatenate(outs, axis=1) * nw_ref[...]
            o_ref[rows, :] = (_silu(rg_ref[rows, :]) * y).astype(o_ref.dtype)
            update_state(gc, rows, 0)
            return c
        return [functools.partial(body, i, 0) for i in range(NCH)]

    return init, phase0, phase1


def _ret_part(nb, seq, rq, rk, rv, rg, cos, sin, dmat, gin, gst, g128, nw):
    nt = seq // SEQ_TILE
    w = RET_HEADS * RET_QK
    tab_both = pl.BlockSpec((SEQ_TILE, w // 2), lambda b, p, t: (_tile_pos(p, t, nt), 0))
    p0_spec = pl.BlockSpec((SEQ_TILE, w), lambda b, p, t: (b * nt + _p0_tile(p, t, nt), 0))
    in_specs = [_fwd_spec(w, nt), p0_spec, _both_spec(512, nt), _fwd_spec(512, nt), tab_both, tab_both] + \
               [_const_spec(a.shape) for a in (dmat, gin, gst, g128, nw)]
    scratch = [pltpu.VMEM((SEQ_TILE, w), F32), pltpu.VMEM((seq // CHUNK, w, CHUNK), F32),
               pltpu.VMEM((2, RET_HEADS, RET_QK, RET_V), F32),
               pltpu.VMEM((seq // CHUNK, RET_HEADS, RET_QK, RET_V), F32)]
    return "ret", _ret_body, (rq, rk, rv, rg, cos, sin, dmat, gin, gst, g128, nw), in_specs, scratch


def _mlstm_body(qk_ref, qkp_ref, qkn_ref, v_ref, gi_ref, gf_ref, mo_ref, cw_ref, cb_ref, gbi_ref, gbf_ref, nwt_ref,
                o_ref, xpad_ref, qk_s, c_ref, m_ref, c_stash, m_stash, *, p, t, tt, nt):
    nh = ML_HEADS
    nhd = 2 * nh

    def init():
        c_ref[...] = jnp.zeros_like(c_ref)
        m_ref[...] = jnp.zeros_like(m_ref)

    valid_b, valid_f = _tri_masks()
    le = valid_b.astype(F32)
    ge = valid_f.astype(F32)
    rowid8 = lax.broadcasted_iota(jnp.int32, (nhd, CHUNK), 0)
    rowid = lax.broadcasted_iota(jnp.int32, (CHUNK, CHUNK), 0)
    top = rowid < ML_QK
    ones_blk = jnp.ones((ML_NROWS, CHUNK), F32)
    srows = ML_V + ML_NROWS

    def row_gates(ci):
        rows = _chunk_rows(ci)
        gi_col = gi_ref[rows, :] + gbi_ref[...]
        li_row = gi_col.T[0:nhd, :]
        lf_row = -_softplus(-(gf_ref[rows, :] + gbf_ref[...]).T[0:nhd, :])
        b_row = jnp.where(rowid8 < nh, _xdot(lf_row, ge), _xdot(lf_row, le))
        tot = jnp.sum(lf_row, axis=1, keepdims=True)
        return gi_col, li_row, b_row, tot

    def update_state(srow, vrow, d, li_row, b_row, tot):
        w_row = tot - b_row + li_row
        m_loc = jnp.max(w_row, axis=1, keepdims=True)
        e_row = jnp.exp(w_row - m_loc)
        m_prev = m_ref[...]
        m_new = jnp.maximum(tot + m_prev, m_loc)
        dec = jnp.exp(tot + m_prev - m_new)
        inj = jnp.exp(m_loc - m_new)
        v = v_ref[vrow, :]
        for pr in range(nh // 2):
            k_pair = (qk_s[nh // 2 + pr, srow, :] * (ML_QK ** -0.5)).astype(BF16)
            for hl in range(2):
                h = 2 * pr + hl
                hd = d * nh + h
                e_b = jnp.broadcast_to(e_row[hd:hd + 1, :], (srows, CHUNK))
                vt = v[:, h * ML_V:(h + 1) * ML_V].T
                lhs = (jnp.concatenate([vt, ones_blk], axis=0) * e_b).astype(BF16)
                s = _bdot(lhs, k_pair)
                c_ref[hd] = (jnp.broadcast_to(dec[hd:hd + 1, :], (srows, CHUNK)) * c_ref[hd]
                             + jnp.broadcast_to(inj[hd:hd + 1, :], (srows, CHUNK)) * s)
        keep = (rowid8 >= nh) if d == 0 else (rowid8 < nh)
        m_ref[...] = jnp.where(keep, m_prev, m_new)

    def direction_out(d, qts, k_pairs, vts, r_col, b_row, c_aug, m_prev):
        mask = valid_f if d == 0 else valid_b
        outs = []
        for h in range(nh):
            hd = d * nh + h
            rj = jnp.where(mask, r_col[:, hd:hd + 1], -jnp.inf)
            u = jnp.maximum(m_prev[hd:hd + 1, :], jnp.max(rj, axis=0, keepdims=True))
            st = _bdot(k_pairs[h // 2], qts[h])
            pt = (st * jnp.exp(rj - u)).astype(BF16)
            lhs = jnp.concatenate([vts[h], ones_blk], axis=0).astype(BF16)
            nd = _bdot(lhs, pt) + jnp.exp(m_prev[hd:hd + 1, :] - u) * _bdot(c_aug[h].astype(BF16), qts[h])
            floor = jnp.exp(-(b_row[hd:hd + 1, :] + u))
            den = jnp.maximum(jnp.abs(nd[ML_V:ML_V + 1, :]), floor)
            outs.append(nd[0:ML_V, :] / den)
        return outs

    def phase0():
        _conv_to_cache(qk_ref, qkp_ref, qkn_ref, xpad_ref, qk_s, cw_ref, cb_ref, tt, nt, _silu)

        def body(i, c):
            ci = NCH - 1 - i
            gc = tt * NCH + ci
            c_stash[gc] = c_ref[nh:nhd]
            m_stash[gc] = m_ref[...]
            _, li_row, b_row, tot = row_gates(ci)
            update_state(_seq_rows(tt, ci), _chunk_rows(ci), 1, li_row, b_row, tot)
            return c
        return [functools.partial(body, i, 0) for i in range(NCH)]

    def phase1():
        def body(ci, c):
            gc = tt * NCH + ci
            srow = _seq_rows(tt, ci)
            rows = _chunk_rows(ci)
            gi_col, li_row, b_row, tot = row_gates(ci)
            r_col = gi_col - _rows_to_cols(b_row)
            qts, k_pairs, vts = [], [], []
            v = v_ref[rows, :]
            for pr in range(nh // 2):
                qt = qk_s[pr, srow, :].T
                qts.append(jnp.where(top, qt, 0.0).astype(BF16))
                qts.append(jnp.where(top, 0.0, qt).astype(BF16))
                k_pairs.append((qk_s[nh // 2 + pr, srow, :] * (ML_QK ** -0.5)).astype(BF16))
            for h in range(nh):
                vts.append(v[:, h * ML_V:(h + 1) * ML_V].T)
            hf = direction_out(0, qts, k_pairs, vts, r_col, b_row, c_ref[0:nh], m_ref[...])
            hb = direction_out(1, qts, k_pairs, vts, r_col, b_row, c_stash[gc], m_stash[gc])
            ys = []
            for h in range(nh):
                x = hf[h] + hb[h]
                mu = jnp.mean(x, axis=0, keepdims=True)
                xc = x - mu
                var = jnp.mean(xc * xc, axis=0, keepdims=True)
                ys.append((xc * lax.rsqrt(var + NORM_EPS) * nwt_ref[h]).T)
            y = jnp.concatenate(ys, axis=1)
            o_ref[rows, :] = (_sigmoid(mo_ref[rows, :]) * y).astype(o_ref.dtype)
            update_state(srow, rows, 0, li_row, b_row, tot)
            return c
        return [functools.partial(body, i, 0) for i in range(NCH)]

    return init, phase0, phase1


def _mlstm_part(nb, seq, mqk, mv, mgi, mgf, mo, cw, cb, gbi, gbf, nwt):
    nt = seq // SEQ_TILE
    ns = 2 * ML_HEADS * ML_QK // LANES
    prev, nxt = _slab_halo_specs_p0(ns, nt, nb * seq)
    in_specs = [_slab_spec_p0(ns, nt), prev, nxt, _both_spec(512, nt), _both_spec(CHUNK, nt), _both_spec(CHUNK, nt),
                _fwd_spec(512, nt)] + [_const_spec(a.shape) for a in (cw, cb, gbi, gbf, nwt)]
    scratch = [pltpu.VMEM((ns, SEQ_TILE + 2 * SUBLANES, LANES), F32), pltpu.VMEM((ns, seq, LANES), F32),
               pltpu.VMEM((2 * ML_HEADS, ML_V + ML_NROWS, CHUNK), F32),
               pltpu.VMEM((2 * ML_HEADS, CHUNK), F32),
               pltpu.VMEM((seq // CHUNK, ML_HEADS, ML_V + ML_NROWS, CHUNK), F32),
               pltpu.VMEM((seq // CHUNK, 2 * ML_HEADS, CHUNK), F32)]
    return "mlstm", _mlstm_body, (mqk, mqk, mqk, mv, mgi, mgf, mo, cw, cb, gbi, gbf, nwt), in_specs, scratch


def _pad_cols(a, width):
    return jnp.pad(a, [(0, 0)] * (a.ndim - 1) + [(0, width - a.shape[-1])])


def _prepare(w_in, lru_conv_w, lru_conv_b, lru_gate_w, lru_gate_b, lru_lambda, ssd_conv_w, ssd_conv_b, ssd_dt_bias,
             ssd_a_log, ssd_d, ssd_norm_w, ret_norm_w, mlstm_conv_w, mlstm_conv_b, mlstm_gate_b, mlstm_norm_w,
             w_branch, w_out, w_ffn_in, w_ffn_out, ln_g, ln_b):
    sp = {}
    off = 0
    for name, w in (("xa", 512), ("ga", 512), ("z", 512), ("xbc", 768), ("dt", 16), ("rq", 256), ("rk", 256),
                    ("rv", 512), ("rg", 512), ("mqk", 512), ("mv", 512), ("mo", 512), ("mg", 16), ("mix", 4096)):
        sp[name] = w_in[:, :, off:off + w]
        off += w
    mg = sp["mg"].reshape(DEPTH, D_MODEL, 2, 2, ML_HEADS)

    def halves_major(w):
        w = w.reshape(DEPTH, D_MODEL, RET_HEADS, 2, RET_QK // 2)
        return jnp.moveaxis(w, 3, 2).reshape(DEPTH, D_MODEL, RET_HEADS * RET_QK)

    cols = {"dt": _pad_cols(sp["dt"], CHUNK), "rq": halves_major(sp["rq"]), "rk": halves_major(sp["rk"]),
            "mgi": _pad_cols(mg[:, :, :, 0, :].reshape(DEPTH, D_MODEL, 8), CHUNK),
            "mgf": _pad_cols(mg[:, :, :, 1, :].reshape(DEPTH, D_MODEL, 8), CHUNK)}
    wp = jnp.concatenate([cols[n] if n in cols else sp[n] for n, _ in PROJ_OUTS], axis=-1).astype(BF16)
    gbi = _pad_cols(mlstm_gate_b[:, :, 0, :].reshape(DEPTH, 1, 8), CHUNK)
    gbf = _pad_cols(mlstm_gate_b[:, :, 1, :].reshape(DEPTH, 1, 8), CHUNK)
    ml_nwt = jnp.broadcast_to(mlstm_norm_w.reshape(DEPTH, ML_HEADS, ML_V, 1), (DEPTH, ML_HEADS, ML_V, CHUNK))

    eye = jnp.eye(LRU_BLOCKS, dtype=F32)
    dense = jnp.einsum("ldghij,hk->ldghikj", lru_gate_w, eye).reshape(DEPTH, 2, 2, LRU_W, LRU_W)
    lru_wg = jnp.concatenate([dense[:, :, 0], dense[:, :, 1]], axis=-1)

    return dict(
        wp=wp, wmix=sp["mix"].astype(BF16),
        ffn_i=w_ffn_in.astype(BF16), ffn_d=(0.5 * w_ffn_out).astype(BF16),
        ln_g=ln_g[:, :, None, :], ln_b=ln_b[:, :, None, :],
        wb=w_branch.astype(BF16), wo=w_out.astype(BF16),
        lru_cw=lru_conv_w, lru_cb=lru_conv_b[:, None, :], lru_wg=(0.5 * lru_wg).astype(BF16),
        lru_gb=0.5 * lru_gate_b.reshape(DEPTH, 2, 1, 2 * LRU_W), lru_lam=lru_lambda[:, :, None, :],
        ssd_cw=ssd_conv_w, ssd_cb=ssd_conv_b[:, None, :],
        ssd_dtb=_pad_cols(ssd_dt_bias.reshape(DEPTH, 1, 16), CHUNK),
        ssd_alog=_pad_cols(ssd_a_log.reshape(DEPTH, 1, 16), CHUNK),
        ssd_dexp=jnp.repeat(ssd_d, SSD_HEAD_DIM, axis=-1)[:, None, :], ssd_nw=ssd_norm_w[:, None, :],
        ret_nw=ret_norm_w[:, None, :],
        ml_cw=mlstm_conv_w, ml_cb=mlstm_conv_b[:, None, :], ml_gbi=gbi, ml_gbf=gbf, ml_nwt=ml_nwt,
    )


def _ret_tables(seq):
    d = RET_QK
    inv = ROPE_BASE ** (-jnp.arange(0, d, 2, dtype=F32) / d)
    ang = jnp.arange(seq, dtype=F32)[:, None] * inv[None, :]
    cos = jnp.tile(jnp.cos(ang), (1, RET_HEADS))
    sin = jnp.tile(jnp.sin(ang), (1, RET_HEADS))
    lg = jnp.log1p(-jnp.exp2(-5.0 - jnp.arange(RET_HEADS, dtype=F32)))[:, None, None]
    pos = jnp.arange(CHUNK, dtype=F32)
    dmat = jnp.exp(lg * jnp.abs(pos[:, None] - pos[None, :])[None])
    ones = jnp.ones((1, 1, RET_V), F32)
    gin = jnp.stack([jnp.exp(lg * (pos + 1.0)[None, :, None]) * ones,
                     jnp.exp(lg * (CHUNK - pos)[None, :, None]) * ones])
    ones = jnp.ones((1, RET_QK, 1), F32)
    gst = jnp.stack([jnp.exp(lg * (CHUNK - 1.0 - pos)[None, None, :]) * ones,
                     jnp.exp(lg * pos[None, None, :]) * ones])
    g128 = jnp.exp(lg * float(CHUNK)) * jnp.ones((1, RET_QK, RET_V), F32)
    return cos, sin, dmat, gin, gst, g128


def _trunk(x3, pw):
    nb, seq, _ = x3.shape
    x = x3.reshape(nb * seq, D_MODEL)
    tabs = _ret_tables(seq)
    for l in range(DEPTH):
        x = _ffn_call(x, pw["ffn_i"][l, 0], pw["ffn_d"][l, 0], pw["ln_g"][l, 0], pw["ln_b"][l, 0])
        pr = dict(zip([n for n, _ in PROJ_OUTS], _proj_call(x, pw["wp"][l])))
        parts = dict(
            lru=_lru_part(nb, seq, pr["xa"], pr["ga"], pw["lru_cw"][l], pw["lru_cb"][l], pw["lru_wg"][l],
                          pw["lru_gb"][l], pw["lru_lam"][l]),
            ssd=_ssd_part(nb, seq, pr["xbc"], pr["dt"], pr["z"], pw["ssd_cw"][l], pw["ssd_cb"][l], pw["ssd_dtb"][l],
                          pw["ssd_alog"][l], pw["ssd_dexp"][l], pw["ssd_nw"][l]),
            ret=_ret_part(nb, seq, pr["rq"], pr["rk"], pr["rv"], pr["rg"], *tabs, pw["ret_nw"][l]),
            mlstm=_mlstm_part(nb, seq, pr["mqk"], pr["mv"], pr["mgi"], pr["mgf"], pr["mo"], pw["ml_cw"][l],
                              pw["ml_cb"][l], pw["ml_gbi"][l], pw["ml_gbf"][l], pw["ml_nwt"][l]))
        ys = {}
        for group, interleave in MIXER_GROUPS:
            ys.update(zip(group, _mixers_call(nb, seq, [parts[n] for n in group], interleave)))
        x = _merge_ffn_call(x, (ys["lru"], ys["ssd"], ys["ret"], ys["mlstm"]), pw["wmix"][l], pw["wb"][l], pw["wo"][l],
                            pw["ln_g"][l, 1], pw["ln_b"][l, 1], pw["ffn_i"][l, 1], pw["ffn_d"][l, 1], pw["ln_g"][l, 2],
                            pw["ln_b"][l, 2])
    return x.reshape(nb, seq, D_MODEL)


def kernel(x_prompt, x_sample, w_in, lru_conv_w, lru_conv_b, lru_gate_w, lru_gate_b, lru_lambda, ssd_conv_w,
           ssd_conv_b, ssd_dt_bias, ssd_a_log, ssd_d, ssd_norm_w, ret_norm_w, mlstm_conv_w, mlstm_conv_b,
           mlstm_gate_b, mlstm_norm_w, w_branch, w_out, w_ffn_in, w_ffn_out, ln_g, ln_b):
    pw = _prepare(w_in, lru_conv_w, lru_conv_b, lru_gate_w, lru_gate_b, lru_lambda, ssd_conv_w, ssd_conv_b,
                  ssd_dt_bias, ssd_a_log, ssd_d, ssd_norm_w, ret_norm_w, mlstm_conv_w, mlstm_conv_b, mlstm_gate_b,
                  mlstm_norm_w, w_branch, w_out, w_ffn_in, w_ffn_out, ln_g, ln_b)
    return _trunk(x_prompt, pw), _trunk(x_sample, pw)
```
